```python
import math, functools
import jax, jax.numpy as jnp
from jax import lax
import numpy as np

D_MODEL = 2048
BATCH = 4
SEQ = 2048
DEPTH = 2
DEC_BATCH = 8
DEC_SEQ = 16
PAST_LEN = 4096

CHUNK = 64
HEAD_DIM = 64
A_HEADS = 16
A_WIDTH = A_HEADS * HEAD_DIM
A_LEFT_CHUNKS = 8
A_ROWS = A_LEFT_CHUNKS * CHUNK
A_REL_CLIP = 256
B_HEADS = 16
B_KV_HEADS = 2
B_GROUP = B_HEADS // B_KV_HEADS
B_WIDTH = B_HEADS * HEAD_DIM
B_KV_WIDTH = B_KV_HEADS * HEAD_DIM
WINDOW = 128
B_LEFT_CHUNKS = WINDOW // CHUNK
B_ROWS = WINDOW
T5_BUCKETS = 32
T5_MAX_DIST = 128
D_FF = 4 * D_MODEL
SPLITS = [A_WIDTH, A_WIDTH, A_WIDTH, B_WIDTH, B_KV_WIDTH, B_KV_WIDTH, D_MODEL, D_MODEL]
N_IN = sum(SPLITS)
EPS = 1e-6
NEG_INF = -1e30

kernel_name = "hybrid_chunk_stream_encoder_step"


def rmsnorm(x, g):
    xf = x.astype(jnp.float32)
    y = xf * lax.rsqrt(jnp.mean(xf * xf, axis=-1, keepdims=True) + EPS)
    return (y * g.astype(jnp.float32)).astype(x.dtype)


def t5_bucket(rel):
    half = T5_BUCKETS // 2
    exact = half // 2
    ret = jnp.where(rel > 0, half, 0)
    n = jnp.abs(rel)
    large = exact + (jnp.log(jnp.maximum(n, 1).astype(jnp.float32) / exact)
                     / math.log(T5_MAX_DIST / exact) * (half - exact)).astype(jnp.int32)
    large = jnp.minimum(large, half - 1)
    return ret + jnp.where(n < exact, n, large)


def t5_rel_bias(q_off, k_off, table):
    bucket = t5_bucket(k_off[None, :] - q_off[:, None])
    b = jnp.transpose(table.astype(jnp.float32)[bucket], (2, 0, 1))
    return b.reshape(B_KV_HEADS, B_GROUP, q_off.shape[0], k_off.shape[0])


def chunk_rel_bias(q_off, k_off, table):
    rel = jnp.clip(q_off[:, None] - k_off[None, :], -A_REL_CLIP, A_REL_CLIP) + A_REL_CLIP
    return jnp.transpose(table.astype(jnp.float32)[rel], (2, 0, 1))[:, None]


def project(h, w_in):
    b, s = h.shape[:2]
    offs = np.cumsum(SPLITS)[:-1].tolist()
    qa, ka, va, qb, kb, vb, ga, gb = jnp.split(h @ w_in, offs, axis=-1)
    hd = lambda t, n: t.reshape(b, s, n, HEAD_DIM)
    return (hd(qa, A_HEADS), hd(ka, A_HEADS), hd(va, A_HEADS),
            hd(qb, B_HEADS), hd(kb, B_KV_HEADS), hd(vb, B_KV_HEADS), ga, gb)


def band_attend(q, k, v, bias, mask, sink):
    s = jnp.einsum('bcqkgd,bclkd->bckgql', q, k).astype(jnp.float32) * (HEAD_DIM ** -0.5) + bias
    s = jnp.where(mask[None, :, None, None], s, NEG_INF)
    if sink is None:
        p = jax.nn.softmax(s, axis=-1)
    else:
        kv, g = q.shape[3], q.shape[4]
        sk = jnp.broadcast_to(sink.astype(jnp.float32).reshape(kv, g, 1, 1), s.shape[:-1] + (1,))
        p = jax.nn.softmax(jnp.concatenate([s, sk], axis=-1), axis=-1)[..., :-1]
    return jnp.einsum('bckgql,bclkd->bcqkgd', p.astype(v.dtype), v)


def gather_band(x, n_prev):
    b, s = x.shape[:2]
    c = s // CHUNK
    xc = x.reshape(b, c, CHUNK, x.shape[2], x.shape[3])
    xp = jnp.pad(xc, ((0, 0), (n_prev, 0), (0, 0), (0, 0), (0, 0)))
    idx = jnp.arange(c)[:, None] + jnp.arange(n_prev + 1)[None, :]
    return xp[:, idx].reshape(b, c, (n_prev + 1) * CHUNK, x.shape[2], x.shape[3])


def prompt_attention(q, k, v, n_prev, bias_fn, sink):
    b, s, hq, _ = q.shape
    hkv = k.shape[2]
    c = s // CHUNK
    L = (n_prev + 1) * CHUNK
    qc = q.reshape(b, c, CHUNK, hkv, hq // hkv, HEAD_DIM)
    k_off = jnp.arange(L) - n_prev * CHUNK
    bias = bias_fn(jnp.arange(CHUNK), k_off)
    key_chunk = jnp.arange(c)[:, None] - n_prev + (jnp.arange(L) // CHUNK)[None, :]
    mask = (key_chunk >= 0)[:, None, :]
    o = band_attend(qc, gather_band(k, n_prev), gather_band(v, n_prev), bias, mask, sink)
    return o.reshape(b, s, hq * HEAD_DIM)


def sample_attention(q, k, v, cache_k, cache_v, bias_fn, sink):
    b, t, hq, _ = q.shape
    hkv = k.shape[2]
    n_c = cache_k.shape[1]
    k_all = jnp.concatenate([cache_k.astype(k.dtype), k], axis=1)
    v_all = jnp.concatenate([cache_v.astype(v.dtype), v], axis=1)
    qc = q.reshape(b, 1, t, hkv, hq // hkv, HEAD_DIM)
    bias = bias_fn(jnp.arange(t), jnp.arange(n_c + t) - n_c)
    mask = jnp.ones((1, 1, n_c + t), dtype=bool)
    o = band_attend(qc, k_all[:, None], v_all[:, None], bias, mask, sink)
    return o.reshape(b, t, hq * HEAD_DIM), k_all[:, -n_c:], v_all[:, -n_c:]


def merge(o_a, o_b, ga, gb, w_ao, w_bo, w_o):
    m = jax.nn.sigmoid(ga) * (o_a @ w_ao) + jax.nn.sigmoid(gb) * (o_b @ w_bo)
    return m @ w_o


def ffn(x, g_pre, g_post, w_up, w_down):
    u = jnp.square(jax.nn.relu(rmsnorm(x, g_pre) @ w_up))
    return x + rmsnorm(u @ w_down, g_post)


def setup_inputs(seed: int = 0) -> dict:
    key = jax.random.key(seed)
    ks = jax.random.split(key, 20)
    nrm = lambda k, shape, scale: jax.random.normal(k, shape, jnp.float32) * scale
    a_len = min(A_ROWS, PAST_LEN)
    b_len = min(B_ROWS, PAST_LEN)
    return {
        "x_prompt": nrm(ks[0], (BATCH, SEQ, D_MODEL), 1.0),
        "x_sample": nrm(ks[1], (DEC_BATCH, DEC_SEQ, D_MODEL), 1.0),
        "cache_a_k": nrm(ks[2], (DEPTH, DEC_BATCH, a_len, A_HEADS, HEAD_DIM), 1.0),
        "cache_a_v": nrm(ks[3], (DEPTH, DEC_BATCH, a_len, A_HEADS, HEAD_DIM), 1.0),
        "cache_b_k": nrm(ks[4], (DEPTH, DEC_BATCH, b_len, B_KV_HEADS, HEAD_DIM), 1.0),
        "cache_b_v": nrm(ks[5], (DEPTH, DEC_BATCH, b_len, B_KV_HEADS, HEAD_DIM), 1.0),
        "w_in": nrm(ks[6], (DEPTH, D_MODEL, N_IN), D_MODEL ** -0.5),
        "w_a_out": nrm(ks[7], (DEPTH, A_WIDTH, D_MODEL), A_WIDTH ** -0.5),
        "w_b_out": nrm(ks[8], (DEPTH, B_WIDTH, D_MODEL), B_WIDTH ** -0.5),
        "w_out": nrm(ks[9], (DEPTH, D_MODEL, D_MODEL), D_MODEL ** -0.5),
        "a_rel_table": nrm(ks[10], (DEPTH, 2 * A_REL_CLIP + 1, A_HEADS), 0.5),
        "t5_table": nrm(ks[11], (T5_BUCKETS, B_HEADS), 0.5),
        "b_sinks": nrm(ks[12], (DEPTH, B_HEADS), 1.0),
        "g_mix_pre": 1.0 + nrm(ks[13], (DEPTH, D_MODEL), 0.05),
        "g_mix_post": 1.0 + nrm(ks[14], (DEPTH, D_MODEL), 0.05),
        "g_ffn_pre": 1.0 + nrm(ks[15], (DEPTH, D_MODEL), 0.05),
        "g_ffn_post": 1.0 + nrm(ks[16], (DEPTH, D_MODEL), 0.05),
        "w_up": nrm(ks[17], (DEPTH, D_MODEL, D_FF), D_MODEL ** -0.5),
        "w_down": nrm(ks[18], (DEPTH, D_FF, D_MODEL), D_FF ** -0.5),
    }


def reference(x_prompt, x_sample, cache_a_k, cache_a_v, cache_b_k, cache_b_v,
              w_in, w_a_out, w_b_out, w_out, a_rel_table, t5_table, b_sinks,
              g_mix_pre, g_mix_post, g_ffn_pre, g_ffn_post, w_up, w_down):
    yp, ys = x_prompt, x_sample
    ak_p, av_p, bk_p, bv_p = [], [], [], []
    ak_s, av_s, bk_s, bv_s = [], [], [], []
    b_bias = functools.partial(t5_rel_bias, table=t5_table)
    for l in range(DEPTH):
        a_bias = functools.partial(chunk_rel_bias, table=a_rel_table[l])
        qa, ka, va, qb, kb, vb, ga, gb = project(rmsnorm(yp, g_mix_pre[l]), w_in[l])
        oa = prompt_attention(qa, ka, va, A_LEFT_CHUNKS, a_bias, None)
        ob = prompt_attention(qb, kb, vb, B_LEFT_CHUNKS, b_bias, b_sinks[l])
        yp = yp + rmsnorm(merge(oa, ob, ga, gb, w_a_out[l], w_b_out[l], w_out[l]), g_mix_post[l])
        na = min(A_ROWS, ka.shape[1])
        nb = min(B_ROWS, kb.shape[1])
        ak_p.append(ka[:, -na:]); av_p.append(va[:, -na:])
        bk_p.append(kb[:, -nb:]); bv_p.append(vb[:, -nb:])
        yp = ffn(yp, g_ffn_pre[l], g_ffn_post[l], w_up[l], w_down[l])
        qa, ka, va, qb, kb, vb, ga, gb = project(rmsnorm(ys, g_mix_pre[l]), w_in[l])
        oa, nak, nav = sample_attention(qa, ka, va, cache_a_k[l], cache_a_v[l], a_bias, None)
        ob, nbk, nbv = sample_attention(qb, kb, vb, cache_b_k[l], cache_b_v[l], b_bias, b_sinks[l])
        ys = ys + rmsnorm(merge(oa, ob, ga, gb, w_a_out[l], w_b_out[l], w_out[l]), g_mix_post[l])
        ak_s.append(nak); av_s.append(nav); bk_s.append(nbk); bv_s.append(nbv)
        ys = ffn(ys, g_ffn_pre[l], g_ffn_post[l], w_up[l], w_down[l])
    new_a_k_prompt = jnp.stack(ak_p)
    new_a_v_prompt = jnp.stack(av_p)
    new_b_k_prompt = jnp.stack(bk_p)
    new_b_v_prompt = jnp.stack(bv_p)
    new_a_k_sample = jnp.stack(ak_s)
    new_a_v_sample = jnp.stack(av_s)
    new_b_k_sample = jnp.stack(bk_s)
    new_b_v_sample = jnp.stack(bv_s)
    return (yp, ys, new_a_k_prompt, new_a_v_prompt, new_b_k_prompt, new_b_v_prompt,
            new_a_k_sample, new_a_v_sample, new_b_k_sample, new_b_v_sample)
```

```python
import functools
import math

import jax
import jax.numpy as jnp
from jax import lax
from jax.experimental import pallas as pl
from jax.experimental.pallas import tpu as pltpu

D_MODEL = 2048
CHUNK = 64
HEAD_DIM = 64
A_HEADS = 16
A_WIDTH = A_HEADS * HEAD_DIM
A_LEFT_CHUNKS = 8
A_REL_CLIP = 256
B_HEADS = 16
B_KV_HEADS = 2
B_GROUP = B_HEADS // B_KV_HEADS
B_WIDTH = B_HEADS * HEAD_DIM
B_KV_WIDTH = B_KV_HEADS * HEAD_DIM
B_LEFT_CHUNKS = 2
T5_BUCKETS = 32
T5_MAX_DIST = 128
D_FF = 4 * D_MODEL
EPS = 1e-6
NEG_INF = -1e30
SCALE = HEAD_DIM ** -0.5

LANES = 128
KV_COLS = 2 * A_WIDTH + 2 * B_KV_WIDTH
QG_COLS = A_WIDTH + B_WIDTH + 2 * D_MODEL

A_TQ = 256
B_TQ = 128
A_KBLOCKS = A_LEFT_CHUNKS * CHUNK // A_TQ + 1
B_KBLOCKS = B_LEFT_CHUNKS * CHUNK // B_TQ + 1
A_HEADS_PER_STEP = 8

VMEM_LIMIT = 56 * 1024 * 1024


def _params(sem, vmem=VMEM_LIMIT):
    return pltpu.CompilerParams(dimension_semantics=sem, vmem_limit_bytes=vmem)


def _rms_scale(x, g):
    return x * lax.rsqrt(jnp.mean(x * x, axis=-1, keepdims=True) + EPS) * g


def _norm_proj_kernel(x_ref, g_ref, w_ref, o32_ref, o16_ref, h_ref):
    @pl.when(pl.program_id(1) == 0)
    def _():
        h_ref[...] = _rms_scale(x_ref[...], g_ref[...]).astype(h_ref.dtype)

    acc = jnp.dot(h_ref[...], w_ref[...], preferred_element_type=jnp.float32)
    o32_ref[...] = acc
    o16_ref[...] = acc.astype(o16_ref.dtype)


def _norm_proj(x, g, w, *, tm, tn):
    m, d = x.shape
    n = w.shape[1]
    return pl.pallas_call(
        _norm_proj_kernel,
        grid=(m // tm, n // tn),
        in_specs=[
            pl.BlockSpec((tm, d), lambda i, j: (i, 0)),
            pl.BlockSpec((1, d), lambda i, j: (0, 0)),
            pl.BlockSpec((d, tn), lambda i, j: (0, j)),
        ],
        out_specs=[
            pl.BlockSpec((tm, tn), lambda i, j: (i, j)),
            pl.BlockSpec((tm, tn), lambda i, j: (i, j)),
            pl.BlockSpec((tm, d), lambda i, j: (i, 0)),
        ],
        out_shape=[
            jax.ShapeDtypeStruct((m, n), jnp.float32),
            jax.ShapeDtypeStruct((m, n), jnp.bfloat16),
            jax.ShapeDtypeStruct((m, d), jnp.bfloat16),
        ],
        compiler_params=_params(("parallel", "arbitrary")),
        name="norm_kv_proj",
    )(x, g, w)


def _matmul_kernel(a_ref, w_ref, o_ref):
    o_ref[...] = jnp.dot(a_ref[...], w_ref[...],
                         preferred_element_type=jnp.float32).astype(o_ref.dtype)


def _matmul(a, w, *, tm, tn):
    m, k = a.shape
    n = w.shape[1]
    return pl.pallas_call(
        _matmul_kernel,
        grid=(m // tm, n // tn),
        in_specs=[
            pl.BlockSpec((tm, k), lambda i, j: (i, 0)),
            pl.BlockSpec((k, tn), lambda i, j: (0, j)),
        ],
        out_specs=pl.BlockSpec((tm, tn), lambda i, j: (i, j)),
        out_shape=jax.ShapeDtypeStruct((m, n), jnp.bfloat16),
        compiler_params=_params(("parallel", "arbitrary")),
        name="qg_proj",
    )(a, w)


def _low_half():
    return lax.broadcasted_iota(jnp.int32, (1, LANES), 1) < HEAD_DIM


def _start_mask(tile_in_seq, n_kblocks, tq):
    lane = lax.broadcasted_iota(jnp.int32, (1, n_kblocks * tq), 1)
    first_valid = jnp.maximum(n_kblocks - 1 - tile_in_seq, 0) * tq
    return jnp.where(lane >= first_valid, 0.0, NEG_INF)


def _attn_a_kernel(q_ref, *refs, tiles_per_seq):
    k_refs = refs[:A_KBLOCKS]
    v_refs = refs[A_KBLOCKS:2 * A_KBLOCKS]
    bias_ref, o_ref = refs[2 * A_KBLOCKS:]
    start = _start_mask(pl.program_id(1) % tiles_per_seq, A_KBLOCKS, A_TQ)
    low = _low_half()
    for p in range(A_HEADS_PER_STEP // 2):
        sl = slice(p * LANES, (p + 1) * LANES)
        q2 = q_ref[:, sl]
        k2 = jnp.concatenate([r[:, sl] for r in k_refs], axis=0)
        v2 = jnp.concatenate([r[:, sl] for r in v_refs], axis=0)
        acc = None
        for hh in range(2):
            sel = low if hh == 0 else jnp.logical_not(low)
            qm = jnp.where(sel, q2, 0) * SCALE
            s = lax.dot_general(qm, k2, (((1,), (1,)), ((), ())),
                                preferred_element_type=jnp.float32)
            s = s + bias_ref[2 * p + hh] + start
            e = jnp.exp(s - jnp.max(s, axis=-1, keepdims=True))
            denom = jnp.sum(e, axis=-1, keepdims=True)
            vm = jnp.where(sel, v2, 0)
            o = jnp.dot(e.astype(v2.dtype), vm,
                        preferred_element_type=jnp.float32) / denom
            acc = o if acc is None else acc + o
        o_ref[:, sl] = acc.astype(o_ref.dtype)


def _attn_a(qg, kv16, bias, *, seq):
    m = qg.shape[0]
    tiles_per_seq = seq // A_TQ
    hw = A_HEADS_PER_STEP * HEAD_DIM
    n_hg = A_WIDTH // hw
    back = A_KBLOCKS - 1

    def kv_map(j, col0):
        def index_map(g, t):
            i = t % tiles_per_seq
            row = t - i + jnp.maximum(i - back + j, 0)
            return (row, col0 + g)
        return index_map

    k_specs = [pl.BlockSpec((A_TQ, hw), kv_map(j, 0)) for j in range(A_KBLOCKS)]
    v_specs = [pl.BlockSpec((A_TQ, hw), kv_map(j, n_hg)) for j in range(A_KBLOCKS)]
    return pl.pallas_call(
        functools.partial(_attn_a_kernel, tiles_per_seq=tiles_per_seq),
        grid=(n_hg, m // A_TQ),
        in_specs=[pl.BlockSpec((A_TQ, hw), lambda g, t: (t, g))] + k_specs + v_specs + [
            pl.BlockSpec((A_HEADS_PER_STEP, A_TQ, A_KBLOCKS * A_TQ),
                         lambda g, t: (g, 0, 0)),
        ],
        out_specs=pl.BlockSpec((A_TQ, hw), lambda g, t: (t, g)),
        out_shape=jax.ShapeDtypeStruct((m, A_WIDTH), jnp.bfloat16),
        compiler_params=_params(("parallel", "parallel")),
        name="attn_a",
    )(qg, *([kv16] * (2 * A_KBLOCKS)), bias)


def _dup_half(x, g):
    swapped = pltpu.roll(x, HEAD_DIM, 1)
    low = _low_half()
    return jnp.where(low, x, swapped) if g == 0 else jnp.where(low, swapped, x)


def _attn_b_kernel(sink_ref, q_ref, *refs, tiles_per_seq):
    k_refs = refs[:B_KBLOCKS]
    v_refs = refs[B_KBLOCKS:2 * B_KBLOCKS]
    bias_ref, o_ref = refs[2 * B_KBLOCKS:]
    start = _start_mask(pl.program_id(0) % tiles_per_seq, B_KBLOCKS, B_TQ)
    low = _low_half()
    k2 = jnp.concatenate([r[...] for r in k_refs], axis=0).astype(jnp.float32)
    v2 = jnp.concatenate([r[...] for r in v_refs], axis=0).astype(jnp.float32)
    pairs = B_GROUP // 2
    for g in range(B_KV_HEADS):
        kd = _dup_half(k2, g).astype(jnp.bfloat16)
        vd = _dup_half(v2, g).astype(jnp.bfloat16)
        stacked = []
        for p in range(pairs):
            c0 = (g * pairs + p) * LANES
            q2 = q_ref[:, c0:c0 + LANES]
            stacked.append(jnp.where(low, q2, 0) * SCALE)
            stacked.append(jnp.where(low, 0, q2) * SCALE)
        qs = jnp.concatenate(stacked, axis=0)
        s_all = lax.dot_general(qs, kd, (((1,), (1,)), ((), ())),
                                preferred_element_type=jnp.float32)
        probs, denoms = [], []
        for hl in range(B_GROUP):
            h = g * B_GROUP + hl
            s = s_all[hl * B_TQ:(hl + 1) * B_TQ] + bias_ref[h] + start
            sink = sink_ref[h]
            mx = jnp.maximum(jnp.max(s, axis=-1, keepdims=True), sink)
            e = jnp.exp(s - mx)
            denoms.append(jnp.sum(e, axis=-1, keepdims=True) + jnp.exp(sink - mx))
            probs.append(e.astype(jnp.bfloat16))
        r = jnp.dot(jnp.concatenate(probs, axis=0), vd,
                    preferred_element_type=jnp.float32)
        for p in range(pairs):
            c0 = (g * pairs + p) * LANES
            r0 = r[(2 * p) * B_TQ:(2 * p + 1) * B_TQ] / denoms[2 * p]
            r1 = r[(2 * p + 1) * B_TQ:(2 * p + 2) * B_TQ] / denoms[2 * p + 1]
            o_ref[:, c0:c0 + LANES] = jnp.where(low, r0, r1).astype(o_ref.dtype)


def _attn_b(qg, kv16, bias, sinks, *, seq):
    m = qg.shape[0]
    tiles_per_seq = seq // B_TQ
    back = B_KBLOCKS - 1
    kb_col = 2 * A_WIDTH // LANES
    vb_col = kb_col + 1

    def kv_map(j, col):
        def index_map(t, sink_ref):
            i = t % tiles_per_seq
            return (t - i + jnp.maximum(i - back + j, 0), col)
        return index_map

    k_specs = [pl.BlockSpec((B_TQ, LANES), kv_map(j, kb_col)) for j in range(B_KBLOCKS)]
    v_specs = [pl.BlockSpec((B_TQ, LANES), kv_map(j, vb_col)) for j in range(B_KBLOCKS)]
    return pl.pallas_call(
        functools.partial(_attn_b_kernel, tiles_per_seq=tiles_per_seq),
        grid_spec=pltpu.PrefetchScalarGridSpec(
            num_scalar_prefetch=1,
            grid=(m // B_TQ,),
            in_specs=[pl.BlockSpec((B_TQ, B_WIDTH), lambda t, s: (t, 1))]
            + k_specs + v_specs + [
                pl.BlockSpec((B_HEADS, B_TQ, B_KBLOCKS * B_TQ), lambda t, s: (0, 0, 0)),
            ],
            out_specs=pl.BlockSpec((B_TQ, B_WIDTH), lambda t, s: (t, 0)),
        ),
        out_shape=jax.ShapeDtypeStruct((m, B_WIDTH), jnp.bfloat16),
        compiler_params=_params(("parallel",)),
        name="attn_b",
    )(sinks, qg, *([kv16] * (2 * B_KBLOCKS)), bias)


def _attn_sample_kernel(qa_ref, qb_ref, kan_ref, van_ref, kbn_ref, vbn_ref,
                        cak_ref, cav_ref, cbk_ref, cbv_ref,
                        bias_ac_ref, bias_an_ref, bias_bc_ref, bias_bn_ref, sink_ref,
                        oa_ref, ob_ref, *, t_new):
    heads = A_HEADS
    rows = heads * t_new
    width = heads * HEAD_DIM
    row_head = lax.broadcasted_iota(jnp.int32, (rows, width), 0) // t_new
    lane_head = lax.broadcasted_iota(jnp.int32, (rows, width), 1) // HEAD_DIM
    own = row_head == lane_head

    def stack_q(q):
        return jnp.where(own, jnp.concatenate([q] * heads, axis=0), 0) * SCALE

    def pad_rows(x, n):
        return jnp.concatenate(
            [x, jnp.zeros((n - x.shape[0], x.shape[1]), x.dtype)], axis=0)

    def scores(qs, k):
        return lax.dot_general(qs, k, (((1,), (1,)), ((), ())),
                               preferred_element_type=jnp.float32)

    def finish(s_c, s_n, v_c, v_n, sink):
        mx = jnp.maximum(jnp.max(s_c, axis=-1, keepdims=True),
                         jnp.max(s_n, axis=-1, keepdims=True))
        if sink is not None:
            mx = jnp.maximum(mx, sink)
        e_c = jnp.exp(s_c - mx)
        e_n = jnp.exp(s_n - mx)
        denom = jnp.sum(e_c, axis=-1, keepdims=True) + jnp.sum(e_n, axis=-1, keepdims=True)
        if sink is not None:
            denom = denom + jnp.exp(sink - mx)
        o = jnp.dot(e_c.astype(jnp.bfloat16), v_c, preferred_element_type=jnp.float32)
        o = o + jnp.dot(e_n.astype(jnp.bfloat16), v_n, preferred_element_type=jnp.float32)
        o = jnp.where(own, o / denom, 0.0)
        return jnp.sum(o.reshape(heads, t_new, width), axis=0)

    qs = stack_q(qa_ref[...])
    k_c = cak_ref[...].astype(jnp.bfloat16)
    v_c = cav_ref[...].astype(jnp.bfloat16)
    k_n = pad_rows(kan_ref[...], LANES).astype(jnp.bfloat16)
    v_n = pad_rows(van_ref[...], LANES).astype(jnp.bfloat16)
    s_c = scores(qs, k_c) + bias_ac_ref[...]
    s_n = scores(qs, k_n) + bias_an_ref[...]
    oa_ref[...] = finish(s_c, s_n, v_c, v_n, None).astype(oa_ref.dtype)

    src = lax.broadcasted_iota(jnp.int32, (B_KV_WIDTH, width), 0)
    dst = lax.broadcasted_iota(jnp.int32, (B_KV_WIDTH, width), 1)
    expand = ((src % HEAD_DIM == dst % HEAD_DIM)
              & (src // HEAD_DIM == dst // (B_GROUP * HEAD_DIM))).astype(jnp.bfloat16)

    def widen(x):
        return jnp.dot(x.astype(jnp.bfloat16), expand,
                       preferred_element_type=jnp.float32).astype(jnp.bfloat16)

    qs = stack_q(qb_ref[...])
    k_c = widen(cbk_ref[...])
    v_c = widen(cbv_ref[...])
    k_n = widen(pad_rows(kbn_ref[...], LANES))
    v_n = widen(pad_rows(vbn_ref[...], LANES))
    s_c = scores(qs, k_c) + bias_bc_ref[...]
    s_n = scores(qs, k_n) + bias_bn_ref[...]
    ob_ref[...] = finish(s_c, s_n, v_c, v_n, sink_ref[...]).astype(ob_ref.dtype)


def _attn_sample(qg, kv32, cak, cav, cbk, cbv, bias_ac, bias_an, bias_bc, bias_bn,
                 sink_col, *, t_new):
    m = qg.shape[0]
    nb = m // t_new
    a_len = cak.shape[1]
    b_len = cbk.shape[1]
    rows = A_HEADS * t_new
    kb_col = 2 * A_WIDTH // LANES
    const = lambda shape: pl.BlockSpec(shape, lambda b: (0,) * len(shape))
    return pl.pallas_call(
        functools.partial(_attn_sample_kernel, t_new=t_new),
        grid=(nb,),
        in_specs=[
            pl.BlockSpec((t_new, A_WIDTH), lambda b: (b, 0)),
            pl.BlockSpec((t_new, B_WIDTH), lambda b: (b, 1)),
            pl.BlockSpec((t_new, A_WIDTH), lambda b: (b, 0)),
            pl.BlockSpec((t_new, A_WIDTH), lambda b: (b, 1)),
            pl.BlockSpec((t_new, LANES), lambda b: (b, kb_col)),
            pl.BlockSpec((t_new, LANES), lambda b: (b, kb_col + 1)),
            pl.BlockSpec((None, a_len, A_WIDTH), lambda b: (b, 0, 0)),
            pl.BlockSpec((None, a_len, A_WIDTH), lambda b: (b, 0, 0)),
            pl.BlockSpec((None, b_len, B_KV_WIDTH), lambda b: (b, 0, 0)),
            pl.BlockSpec((None, b_len, B_KV_WIDTH), lambda b: (b, 0, 0)),
            const((rows, a_len)),
            const((rows, LANES)),
            const((rows, b_len)),
            const((rows, LANES)),
            const((rows, 1)),
        ],
        out_specs=[
            pl.BlockSpec((t_new, A_WIDTH), lambda b: (b, 0)),
            pl.BlockSpec((t_new, B_WIDTH), lambda b: (b, 0)),
        ],
        out_shape=[
            jax.ShapeDtypeStruct((m, A_WIDTH), jnp.bfloat16),
            jax.ShapeDtypeStruct((m, B_WIDTH), jnp.bfloat16),
        ],
        compiler_params=_params(("parallel",)),
        name="attn_sample",
    )(qg, qg, kv32, kv32, kv32, kv32, cak, cav, cbk, cbv,
      bias_ac, bias_an, bias_bc, bias_bn, sink_col)


def _merge_kernel(oa_ref, ob_ref, ga_ref, gb_ref, x_ref, wa_ref, wb_ref, wo_ref,
                  g_ref, y_ref):
    ta = jnp.dot(oa_ref[...], wa_ref[...], preferred_element_type=jnp.float32)
    tb = jnp.dot(ob_ref[...], wb_ref[...], preferred_element_type=jnp.float32)
    mixed = (jax.nn.sigmoid(ga_ref[...].astype(jnp.float32)) * ta
             + jax.nn.sigmoid(gb_ref[...].astype(jnp.float32)) * tb)
    z = jnp.dot(mixed.astype(jnp.bfloat16), wo_ref[...],
                preferred_element_type=jnp.float32)
    y_ref[...] = x_ref[...] + _rms_scale(z, g_ref[...])


def _merge(oa, ob, qg, x, wa, wb, wo, g, *, tm):
    m, d = x.shape
    ga_col = (A_WIDTH + B_WIDTH) // d
    resident = lambda shape: pl.BlockSpec(shape, lambda i: (0, 0),
                                          pipeline_mode=pl.Buffered(1))
    return pl.pallas_call(
        _merge_kernel,
        grid=(m // tm,),
        in_specs=[
            pl.BlockSpec((tm, A_WIDTH), lambda i: (i, 0)),
            pl.BlockSpec((tm, B_WIDTH), lambda i: (i, 0)),
            pl.BlockSpec((tm, d), lambda i: (i, ga_col)),
            pl.BlockSpec((tm, d), lambda i: (i, ga_col + 1)),
            pl.BlockSpec((tm, d), lambda i: (i, 0)),
            resident(wa.shape),
            resident(wb.shape),
            resident(wo.shape),
            resident((1, d)),
        ],
        out_specs=pl.BlockSpec((tm, d), lambda i: (i, 0)),
        out_shape=jax.ShapeDtypeStruct((m, d), jnp.float32),
        compiler_params=_params(("parallel",)),
        name="merge",
    )(oa, ob, qg, qg, x, wa, wb, wo, g)


def _ffn_kernel(x_ref, gpre_ref, wu_ref, wd_ref, gpost_ref, y_ref, h_ref):
    f = pl.program_id(1)

    @pl.when(f == 0)
    def _():
        h_ref[...] = _rms_scale(x_ref[...], gpre_ref[...]).astype(h_ref.dtype)

    u = jnp.dot(h_ref[...], wu_ref[...], preferred_element_type=jnp.float32)
    u = jnp.square(jnp.maximum(u, 0.0)).astype(jnp.bfloat16)
    part = jnp.dot(u, wd_ref[...], preferred_element_type=jnp.float32)

    @pl.when(f == 0)
    def _():
        y_ref[...] = part

    @pl.when(f > 0)
    def _():
        y_ref[...] += part

    @pl.when(f == pl.num_programs(1) - 1)
    def _():
        y_ref[...] = x_ref[...] + _rms_scale(y_ref[...], gpost_ref[...])


def _ffn(x, gpre, wu, wd, gpost, *, tm, tf):
    m, d = x.shape
    dff = wu.shape[1]
    return pl.pallas_call(
        _ffn_kernel,
        grid=(m // tm, dff // tf),
        in_specs=[
            pl.BlockSpec((tm, d), lambda i, f: (i, 0)),
            pl.BlockSpec((1, d), lambda i, f: (0, 0)),
            pl.BlockSpec((d, tf), lambda i, f: (0, f)),
            pl.BlockSpec((tf, d), lambda i, f: (f, 0)),
            pl.BlockSpec((1, d), lambda i, f: (0, 0)),
        ],
        out_specs=pl.BlockSpec((tm, d), lambda i, f: (i, 0)),
        out_shape=jax.ShapeDtypeStruct((m, d), jnp.float32),
        scratch_shapes=[pltpu.VMEM((tm, d), jnp.bfloat16)],
        compiler_params=_params(("parallel", "arbitrary")),
        name="ffn",
    )(x, gpre, wu, wd, gpost)


def _t5_bucket(rel):
    half = T5_BUCKETS // 2
    exact = half // 2
    ret = jnp.where(rel > 0, half, 0)
    n = jnp.abs(rel)
    large = exact + (jnp.log(jnp.maximum(n, 1).astype(jnp.float32) / exact)
                     / math.log(T5_MAX_DIST / exact) * (half - exact)).astype(jnp.int32)
    large = jnp.minimum(large, half - 1)
    return ret + jnp.where(n < exact, n, large)


def _a_bias(q_off, k_off, table):
    rel = jnp.clip(q_off[:, None] - k_off[None, :], -A_REL_CLIP, A_REL_CLIP) + A_REL_CLIP
    return jnp.transpose(table[rel], (2, 0, 1))


def _b_bias(q_off, k_off, table):
    bucket = _t5_bucket(k_off[None, :] - q_off[:, None])
    return jnp.transpose(table[bucket], (2, 0, 1))


def _band_tile(bias_fn, tq, n_prev):
    window = tq + n_prev * CHUNK
    band = bias_fn(jnp.arange(CHUNK), jnp.arange((n_prev + 1) * CHUNK) - n_prev * CHUNK)
    blocks = []
    for c in range(tq // CHUNK):
        left = c * CHUNK
        right = window - left - band.shape[-1]
        blocks.append(jnp.pad(band, ((0, 0), (0, 0), (left, right)),
                              constant_values=NEG_INF))
    return jnp.concatenate(blocks, axis=1)


def _sample_bias(bias_fn, t_new, n_cache):
    full = bias_fn(jnp.arange(t_new), jnp.arange(n_cache + t_new) - n_cache)
    full = full.reshape(-1, n_cache + t_new)
    new = jnp.pad(full[:, n_cache:], ((0, 0), (0, LANES - t_new)),
                  constant_values=NEG_INF)
    return full[:, :n_cache], new


def _project(x, g_pre, w_kv, w_qg, *, tm):
    kv32, kv16, h = _norm_proj(x, g_pre, w_kv, tm=tm, tn=KV_COLS // 3)
    qg = _matmul(h, w_qg, tm=tm, tn=1024)
    return kv32, kv16, qg


def kernel(x_prompt, x_sample, cache_a_k, cache_a_v, cache_b_k, cache_b_v, w_in,
           w_a_out, w_b_out, w_out, a_rel_table, t5_table, b_sinks, g_mix_pre,
           g_mix_post, g_ffn_pre, g_ffn_post, w_up, w_down):
    depth = w_in.shape[0]
    batch, seq, d = x_prompt.shape
    dec_batch, t_new, _ = x_sample.shape
    a_len = cache_a_k.shape[2]
    b_len = cache_b_k.shape[2]
    bf16 = jnp.bfloat16

    yp = x_prompt.reshape(batch * seq, d)
    ys = x_sample.reshape(dec_batch * t_new, d)

    o_qa, o_ka, o_qb, o_kb, o_ga = 0, A_WIDTH, 3 * A_WIDTH, 3 * A_WIDTH + B_WIDTH, \
        3 * A_WIDTH + B_WIDTH + 2 * B_KV_WIDTH
    b_bias_fn = functools.partial(_b_bias, table=t5_table)
    bias_b_tile = _band_tile(b_bias_fn, B_TQ, B_LEFT_CHUNKS)
    bias_bc, bias_bn = _sample_bias(b_bias_fn, t_new, b_len)

    outs = {k: [] for k in ("akp", "avp", "bkp", "bvp", "aks", "avs", "bks", "bvs")}
    for l in range(depth):
        w = w_in[l]
        w_kv = jnp.concatenate([w[:, o_ka:o_qb], w[:, o_kb:o_ga]], axis=1).astype(bf16)
        w_qg = jnp.concatenate([w[:, o_qa:o_ka], w[:, o_qb:o_kb], w[:, o_ga:]],
                               axis=1).astype(bf16)
        wa = w_a_out[l].astype(bf16)
        wb = w_b_out[l].astype(bf16)
        wo = w_out[l].astype(bf16)
        wu = w_up[l].astype(bf16)
        wd = w_down[l].astype(bf16)
        g_pre = g_mix_pre[l][None, :]
        g_post = g_mix_post[l][None, :]
        gf_pre = g_ffn_pre[l][None, :]
        gf_post = g_ffn_post[l][None, :]
        a_bias_fn = functools.partial(_a_bias, table=a_rel_table[l])

        kv32, kv16, qg = _project(yp, g_pre, w_kv, w_qg, tm=1024)
        oa = _attn_a(qg, kv16, _band_tile(a_bias_fn, A_TQ, A_LEFT_CHUNKS), seq=seq)
        ob = _attn_b(qg, kv16, bias_b_tile, b_sinks[l], seq=seq)
        yp = _merge(oa, ob, qg, yp, wa, wb, wo, g_post, tm=256)
        yp = _ffn(yp, gf_pre, wu, wd, gf_post, tm=512, tf=1024)
        kv_p = kv32.reshape(batch, seq, KV_COLS)
        na, nb = min(a_len, seq), min(b_len, seq)
        outs["akp"].append(kv_p[:, seq - na:, :A_WIDTH].reshape(batch, na, A_HEADS, HEAD_DIM))
        outs["avp"].append(kv_p[:, seq - na:, A_WIDTH:2 * A_WIDTH]
                           .reshape(batch, na, A_HEADS, HEAD_DIM))
        outs["bkp"].append(kv_p[:, seq - nb:, 2 * A_WIDTH:2 * A_WIDTH + B_KV_WIDTH]
                           .reshape(batch, nb, B_KV_HEADS, HEAD_DIM))
        outs["bvp"].append(kv_p[:, seq - nb:, 2 * A_WIDTH + B_KV_WIDTH:]
                           .reshape(batch, nb, B_KV_HEADS, HEAD_DIM))

        ms = dec_batch * t_new
        kv32, kv16, qg = _project(ys, g_pre, w_kv, w_qg, tm=ms)
        cak = cache_a_k[l].reshape(dec_batch, a_len, A_WIDTH)
        cav = cache_a_v[l].reshape(dec_batch, a_len, A_WIDTH)
        cbk = cache_b_k[l].reshape(dec_batch, b_len, B_KV_WIDTH)
        cbv = cache_b_v[l].reshape(dec_batch, b_len, B_KV_WIDTH)
        bias_ac, bias_an = _sample_bias(a_bias_fn, t_new, a_len)
        sink_col = jnp.repeat(b_sinks[l], t_new)[:, None]
        oa, ob = _attn_sample(qg, kv32, cak, cav, cbk, cbv, bias_ac, bias_an,
                              bias_bc, bias_bn, sink_col, t_new=t_new)
        ys = _merge(oa, ob, qg, ys, wa, wb, wo, g_post, tm=ms)
        ys = _ffn(ys, gf_pre, wu, wd, gf_post, tm=ms, tf=1024)
        kv_s = kv32.reshape(dec_batch, t_new, KV_COLS)
        roll = lambda cache, new: jnp.concatenate([cache, new], axis=1)[:, t_new:]
        outs["aks"].append(roll(cak, kv_s[:, :, :A_WIDTH])
                           .reshape(dec_batch, a_len, A_HEADS, HEAD_DIM))
        outs["avs"].append(roll(cav, kv_s[:, :, A_WIDTH:2 * A_WIDTH])
                           .reshape(dec_batch, a_len, A_HEADS, HEAD_DIM))
        outs["bks"].append(roll(cbk, kv_s[:, :, 2 * A_WIDTH:2 * A_WIDTH + B_KV_WIDTH])
                           .reshape(dec_batch, b_len, B_KV_HEADS, HEAD_DIM))
        outs["bvs"].append(roll(cbv, kv_s[:, :, 2 * A_WIDTH + B_KV_WIDTH:])
                           .reshape(dec_batch, b_len, B_KV_HEADS, HEAD_DIM))

    stack = lambda k: jnp.stack(outs[k])
    return (yp.reshape(batch, seq, d), ys.reshape(dec_batch, t_new, d),
            stack("akp"), stack("avp"), stack("bkp"), stack("bvp"),
            stack("aks"), stack("avs"), stack("bks"), stack("bvs"))
```

```python
import functools
import math

import jax
import jax.numpy as jnp
from jax import lax
from jax.experimental import pallas as pl
from jax.experimental.pallas import tpu as pltpu

D_MODEL = 2048
CHUNK = 64
HEAD_DIM = 64
A_HEADS = 16
A_WIDTH = A_HEADS * HEAD_DIM
A_LEFT_CHUNKS = 8
A_REL_CLIP = 256
B_HEADS = 16
B_KV_HEADS = 2
B_GROUP = B_HEADS // B_KV_HEADS
B_WIDTH = B_HEADS * HEAD_DIM
B_KV_WIDTH = B_KV_HEADS * HEAD_DIM
B_LEFT_CHUNKS = 2
T5_BUCKETS = 32
T5_MAX_DIST = 128
EPS = 1e-6
NEG_INF = -1e30
SCALE = HEAD_DIM ** -0.5

LANES = 128
MXU_COLS = 256

COL_QA = 0
COL_KA = A_WIDTH
COL_QB = 3 * A_WIDTH
COL_KB = 3 * A_WIDTH + B_WIDTH
COL_GA = COL_KB + 2 * B_KV_WIDTH
QG_COLS = A_WIDTH + B_WIDTH + 2 * D_MODEL
QG_TN = 4 * MXU_COLS

A_TQ = 256
B_TQ = 128
A_KBLOCKS = A_LEFT_CHUNKS * CHUNK // A_TQ + 1
B_KBLOCKS = B_LEFT_CHUNKS * CHUNK // B_TQ + 1
A_HEADS_PER_STEP = 8

VMEM_LIMIT = 56 * 1024 * 1024


def _params(sem, vmem=VMEM_LIMIT):
    return pltpu.CompilerParams(dimension_semantics=sem, vmem_limit_bytes=vmem)


def _rms_scale(x, g):
    return x * lax.rsqrt(jnp.mean(x * x, axis=-1, keepdims=True) + EPS) * g


def _layer_vec_spec(l, d):
    return pl.BlockSpec((None, 1, d), lambda *_: (l, 0, 0))


def _norm_proj_kernel(x_ref, g_ref, w_ref, o32_ref, o16_ref, h_ref):
    @pl.when(pl.program_id(1) == 0)
    def _():
        h_ref[...] = _rms_scale(x_ref[...], g_ref[...]).astype(h_ref.dtype)

    acc = jnp.dot(h_ref[...], w_ref[...], preferred_element_type=jnp.float32)
    o32_ref[...] = acc
    o16_ref[...] = acc.astype(o16_ref.dtype)


def _norm_kv_proj(x, g, w_in, l, *, tm):
    m, d = x.shape
    tn = A_WIDTH
    n = 2 * A_WIDTH
    col0 = COL_KA // tn
    return pl.pallas_call(
        _norm_proj_kernel,
        grid=(m // tm, n // tn),
        in_specs=[
            pl.BlockSpec((tm, d), lambda i, j: (i, 0)),
            _layer_vec_spec(l, d),
            pl.BlockSpec((None, d, tn), lambda i, j: (l, 0, col0 + j)),
        ],
        out_specs=[
            pl.BlockSpec((tm, tn), lambda i, j: (i, j)),
            pl.BlockSpec((tm, tn), lambda i, j: (i, j)),
            pl.BlockSpec((tm, d), lambda i, j: (i, 0)),
        ],
        out_shape=[
            jax.ShapeDtypeStruct((m, n), jnp.float32),
            jax.ShapeDtypeStruct((m, n), jnp.bfloat16),
            jax.ShapeDtypeStruct((m, d), jnp.bfloat16),
        ],
        compiler_params=_params(("parallel", "arbitrary")),
        name="norm_kv_proj",
    )(x, g, w_in)


def _kvb_kernel(h_ref, w_ref, o32_ref, o16_ref):
    acc = jnp.dot(h_ref[...], w_ref[...], preferred_element_type=jnp.float32)
    o32_ref[...] = acc
    o16_ref[...] = acc.astype(o16_ref.dtype)


def _kvb_proj(h, w_in, l, *, tm):
    m, d = h.shape
    n = 2 * B_KV_WIDTH
    col = COL_KB // n
    return pl.pallas_call(
        _kvb_kernel,
        grid=(m // tm,),
        in_specs=[
            pl.BlockSpec((tm, d), lambda i: (i, 0)),
            pl.BlockSpec((None, d, n), lambda i: (l, 0, col)),
        ],
        out_specs=[pl.BlockSpec((tm, n), lambda i: (i, 0))] * 2,
        out_shape=[
            jax.ShapeDtypeStruct((m, n), jnp.float32),
            jax.ShapeDtypeStruct((m, n), jnp.bfloat16),
        ],
        compiler_params=_params(("parallel",)),
        name="kvb_proj",
    )(h, w_in)


def _qg_kernel(h_ref, *refs):
    w_refs, o_ref = refs[:-1], refs[-1]
    h = h_ref[...]
    for k, w_ref in enumerate(w_refs):
        o_ref[:, k * MXU_COLS:(k + 1) * MXU_COLS] = jnp.dot(
            h, w_ref[...], preferred_element_type=jnp.float32).astype(o_ref.dtype)


def _qg_proj(h, w_in, l, *, tm):
    m, d = h.shape
    per_tile = QG_TN // MXU_COLS
    qa0, qb0, ga0 = (c // MXU_COLS for c in (COL_QA, COL_QB, COL_GA))
    n_q = A_WIDTH // QG_TN

    def w_map(k):
        def index_map(i, j):
            gate0 = ga0 + (j - 2 * n_q) * per_tile
            src = jnp.where(j < n_q, qa0 + j * per_tile,
                            jnp.where(j < 2 * n_q, qb0 + (j - n_q) * per_tile, gate0))
            return (l, 0, src + k)
        return index_map

    return pl.pallas_call(
        _qg_kernel,
        grid=(m // tm, QG_COLS // QG_TN),
        in_specs=[pl.BlockSpec((tm, d), lambda i, j: (i, 0))]
        + [pl.BlockSpec((None, d, MXU_COLS), w_map(k)) for k in range(per_tile)],
        out_specs=pl.BlockSpec((tm, QG_TN), lambda i, j: (i, j)),
        out_shape=jax.ShapeDtypeStruct((m, QG_COLS), jnp.bfloat16),
        compiler_params=_params(("parallel", "arbitrary")),
        name="qg_proj",
    )(h, *([w_in] * per_tile))


def _low_half():
    return lax.broadcasted_iota(jnp.int32, (1, LANES), 1) < HEAD_DIM


def _start_mask(tile_in_seq, n_kblocks, tq):
    lane = lax.broadcasted_iota(jnp.int32, (1, n_kblocks * tq), 1)
    first_valid = jnp.maximum(n_kblocks - 1 - tile_in_seq, 0) * tq
    return jnp.where(lane >= first_valid, 0.0, NEG_INF)


def _attn_a_kernel(q_ref, *refs, tiles_per_seq):
    k_refs = refs[:A_KBLOCKS]
    v_refs = refs[A_KBLOCKS:2 * A_KBLOCKS]
    bias_ref, o_ref = refs[2 * A_KBLOCKS:]
    start = _start_mask(pl.program_id(1) % tiles_per_seq, A_KBLOCKS, A_TQ)
    low = _low_half()
    for p in range(A_HEADS_PER_STEP // 2):
        sl = slice(p * LANES, (p + 1) * LANES)
        q2 = q_ref[:, sl]
        k2 = jnp.concatenate([r[:, sl] for r in k_refs], axis=0)
        v2 = jnp.concatenate([r[:, sl] for r in v_refs], axis=0)
        acc = None
        for hh in range(2):
            sel = low if hh == 0 else jnp.logical_not(low)
            qm = jnp.where(sel, q2, 0) * SCALE
            s = lax.dot_general(qm, k2, (((1,), (1,)), ((), ())),
                                preferred_element_type=jnp.float32)
            s = s + bias_ref[2 * p + hh] + start
            e = jnp.exp(s - jnp.max(s, axis=-1, keepdims=True))
            denom = jnp.sum(e, axis=-1, keepdims=True)
            vm = jnp.where(sel, v2, 0)
            o = jnp.dot(e.astype(v2.dtype), vm,
                        preferred_element_type=jnp.float32) / denom
            acc = o if acc is None else acc + o
        o_ref[:, sl] = acc.astype(o_ref.dtype)


def _attn_a(qg, kva16, bias, *, seq):
    m = qg.shape[0]
    tiles_per_seq = seq // A_TQ
    hw = A_HEADS_PER_STEP * HEAD_DIM
    n_hg = A_WIDTH // hw
    back = A_KBLOCKS - 1

    def kv_map(j, col0):
        def index_map(g, t):
            i = t % tiles_per_seq
            row = t - i + jnp.maximum(i - back + j, 0)
            return (row, col0 + g)
        return index_map

    k_specs = [pl.BlockSpec((A_TQ, hw), kv_map(j, 0)) for j in range(A_KBLOCKS)]
    v_specs = [pl.BlockSpec((A_TQ, hw), kv_map(j, n_hg)) for j in range(A_KBLOCKS)]
    return pl.pallas_call(
        functools.partial(_attn_a_kernel, tiles_per_seq=tiles_per_seq),
        grid=(n_hg, m // A_TQ),
        in_specs=[pl.BlockSpec((A_TQ, hw), lambda g, t: (t, g))] + k_specs + v_specs + [
            pl.BlockSpec((A_HEADS_PER_STEP, A_TQ, A_KBLOCKS * A_TQ),
                         lambda g, t: (g, 0, 0)),
        ],
        out_specs=pl.BlockSpec((A_TQ, hw), lambda g, t: (t, g)),
        out_shape=jax.ShapeDtypeStruct((m, A_WIDTH), jnp.bfloat16),
        compiler_params=_params(("parallel", "parallel")),
        name="attn_a",
    )(qg, *([kva16] * (2 * A_KBLOCKS)), bias)


def _dup_half(x, g):
    swapped = pltpu.roll(x, HEAD_DIM, 1)
    low = _low_half()
    return jnp.where(low, x, swapped) if g == 0 else jnp.where(low, swapped, x)


def _attn_b_kernel(sink_ref, q_ref, *refs, tiles_per_seq):
    k_refs = refs[:B_KBLOCKS]
    v_refs = refs[B_KBLOCKS:2 * B_KBLOCKS]
    bias_ref, o_ref = refs[2 * B_KBLOCKS:]
    start = _start_mask(pl.program_id(0) % tiles_per_seq, B_KBLOCKS, B_TQ)
    low = _low_half()
    k2 = jnp.concatenate([r[...] for r in k_refs], axis=0).astype(jnp.float32)
    v2 = jnp.concatenate([r[...] for r in v_refs], axis=0).astype(jnp.float32)
    pairs = B_GROUP // 2
    for g in range(B_KV_HEADS):
        kd = _dup_half(k2, g).astype(jnp.bfloat16)
        vd = _dup_half(v2, g).astype(jnp.bfloat16)
        stacked = []
        for p in range(pairs):
            c0 = (g * pairs + p) * LANES
            q2 = q_ref[:, c0:c0 + LANES]
            stacked.append(jnp.where(low, q2, 0) * SCALE)
            stacked.append(jnp.where(low, 0, q2) * SCALE)
        qs = jnp.concatenate(stacked, axis=0)
        s_all = lax.dot_general(qs, kd, (((1,), (1,)), ((), ())),
                                preferred_element_type=jnp.float32)
        probs, denoms = [], []
        for hl in range(B_GROUP):
            h = g * B_GROUP + hl
            s = s_all[hl * B_TQ:(hl + 1) * B_TQ] + bias_ref[h] + start
            sink = sink_ref[h]
            mx = jnp.maximum(jnp.max(s, axis=-1, keepdims=True), sink)
            e = jnp.exp(s - mx)
            denoms.append(jnp.sum(e, axis=-1, keepdims=True) + jnp.exp(sink - mx))
            probs.append(e.astype(jnp.bfloat16))
        r = jnp.dot(jnp.concatenate(probs, axis=0), vd,
                    preferred_element_type=jnp.float32)
        for p in range(pairs):
            c0 = (g * pairs + p) * LANES
            r0 = r[(2 * p) * B_TQ:(2 * p + 1) * B_TQ] / denoms[2 * p]
            r1 = r[(2 * p + 1) * B_TQ:(2 * p + 2) * B_TQ] / denoms[2 * p + 1]
            o_ref[:, c0:c0 + LANES] = jnp.where(low, r0, r1).astype(o_ref.dtype)


def _attn_b(qg, kvb16, bias, sinks, *, seq):
    m = qg.shape[0]
    tiles_per_seq = seq // B_TQ
    back = B_KBLOCKS - 1

    def kv_map(j, col):
        def index_map(t, sink_ref):
            i = t % tiles_per_seq
            return (t - i + jnp.maximum(i - back + j, 0), col)
        return index_map

    k_specs = [pl.BlockSpec((B_TQ, LANES), kv_map(j, 0)) for j in range(B_KBLOCKS)]
    v_specs = [pl.BlockSpec((B_TQ, LANES), kv_map(j, 1)) for j in range(B_KBLOCKS)]
    return pl.pallas_call(
        functools.partial(_attn_b_kernel, tiles_per_seq=tiles_per_seq),
        grid_spec=pltpu.PrefetchScalarGridSpec(
            num_scalar_prefetch=1,
            grid=(m // B_TQ,),
            in_specs=[pl.BlockSpec((B_TQ, B_WIDTH), lambda t, s: (t, 1))]
            + k_specs + v_specs + [
                pl.BlockSpec((B_HEADS, B_TQ, B_KBLOCKS * B_TQ), lambda t, s: (0, 0, 0)),
            ],
            out_specs=pl.BlockSpec((B_TQ, B_WIDTH), lambda t, s: (t, 0)),
        ),
        out_shape=jax.ShapeDtypeStruct((m, B_WIDTH), jnp.bfloat16),
        compiler_params=_params(("parallel",)),
        name="attn_b",
    )(sinks, qg, *([kvb16] * (2 * B_KBLOCKS)), bias)


def _attn_sample_kernel(qa_ref, qb_ref, kan_ref, van_ref, kbn_ref, vbn_ref,
                        cak_ref, cav_ref, cbk_ref, cbv_ref,
                        bias_ac_ref, bias_an_ref, bias_bc_ref, bias_bn_ref, sink_ref,
                        *refs, t_new):
    oa_ref, ob_ref, nak_ref, nav_ref, nbk_ref, nbv_ref = refs[-6:]
    heads = A_HEADS
    rows = heads * t_new
    width = heads * HEAD_DIM
    bf16 = jnp.bfloat16
    row_head = lax.broadcasted_iota(jnp.int32, (rows, width), 0) // t_new
    lane_head = lax.broadcasted_iota(jnp.int32, (rows, width), 1) // HEAD_DIM
    own = row_head == lane_head

    def stack_q(q):
        return jnp.where(own, jnp.concatenate([q] * heads, axis=0), 0) * SCALE

    def pad_top(x):
        return jnp.concatenate(
            [jnp.zeros((LANES - t_new, x.shape[1]), x.dtype), x], axis=0)

    def nt_dot(a, b):
        return lax.dot_general(a, b, (((1,), (1,)), ((), ())),
                               preferred_element_type=jnp.float32)

    def finish(s_c, s_n, vt_c, v_n, sink):
        mx = jnp.maximum(jnp.max(s_c, axis=-1, keepdims=True),
                         jnp.max(s_n, axis=-1, keepdims=True))
        if sink is not None:
            mx = jnp.maximum(mx, sink)
        e_c = jnp.exp(s_c - mx)
        e_n = jnp.exp(s_n - mx)
        denom = jnp.sum(e_c, axis=-1, keepdims=True) + jnp.sum(e_n, axis=-1, keepdims=True)
        if sink is not None:
            denom = denom + jnp.exp(sink - mx)
        o = nt_dot(e_c.astype(bf16), vt_c)
        o = o + jnp.dot(e_n.astype(bf16), v_n, preferred_element_type=jnp.float32)
        o = jnp.where(own, o / denom, 0.0)
        return jnp.sum(o.reshape(heads, t_new, width), axis=0)

    def roll_in(cache_t, new_pad):
        n = cache_t.shape[1]
        shifted = pltpu.roll(cache_t, n - t_new, 1)
        lane = lax.broadcasted_iota(jnp.int32, (1, LANES), 1)
        tail = jnp.where(lane < LANES - t_new, shifted[:, n - LANES:], new_pad.T)
        if n == LANES:
            return tail
        return jnp.concatenate([shifted[:, :n - LANES], tail], axis=1)

    kt_c = cak_ref[...]
    vt_c = cav_ref[...]
    k_n = pad_top(kan_ref[...])
    v_n = pad_top(van_ref[...])
    qs = stack_q(qa_ref[...])
    s_c = jnp.dot(qs, kt_c.astype(bf16), preferred_element_type=jnp.float32) + bias_ac_ref[...]
    s_n = nt_dot(qs, k_n.astype(bf16)) + bias_an_ref[...]
    oa_ref[...] = finish(s_c, s_n, vt_c.astype(bf16), v_n.astype(bf16), None).astype(oa_ref.dtype)
    nak_ref[...] = roll_in(kt_c, k_n)
    nav_ref[...] = roll_in(vt_c, v_n)

    def expansion(shape, src_axis):
        src = lax.broadcasted_iota(jnp.int32, shape, src_axis)
        dst = lax.broadcasted_iota(jnp.int32, shape, 1 - src_axis)
        return ((src % HEAD_DIM == dst % HEAD_DIM)
                & (src // HEAD_DIM == dst // (B_GROUP * HEAD_DIM))).astype(bf16)

    def widen(x):
        return jnp.dot(x.astype(bf16), expansion((B_KV_WIDTH, width), 0),
                       preferred_element_type=jnp.float32).astype(bf16)

    def widen_t(xt):
        return jnp.dot(expansion((width, B_KV_WIDTH), 1), xt.astype(bf16),
                       preferred_element_type=jnp.float32).astype(bf16)

    kt_c = cbk_ref[...]
    vt_c = cbv_ref[...]
    k_n = pad_top(kbn_ref[...])
    v_n = pad_top(vbn_ref[...])
    qs = stack_q(qb_ref[...])
    s_c = jnp.dot(qs, widen_t(kt_c), preferred_element_type=jnp.float32) + bias_bc_ref[...]
    s_n = nt_dot(qs, widen(k_n)) + bias_bn_ref[...]
    ob_ref[...] = finish(s_c, s_n, widen_t(vt_c), widen(v_n), sink_ref[...]).astype(ob_ref.dtype)
    nbk_ref[...] = roll_in(kt_c, k_n)
    nbv_ref[...] = roll_in(vt_c, v_n)


def _attn_sample(qg, kva32, kvb32, caches, l, prev_out, bias_ac, bias_an, bias_bc,
                 bias_bn, sink_col, *, t_new):
    m = qg.shape[0]
    nb = m // t_new
    cak, cav, cbk, cbv = caches
    a_len, b_len = cak.shape[-1], cbk.shape[-1]
    rows = A_HEADS * t_new
    const = lambda shape: pl.BlockSpec(shape, lambda b: (0,) * len(shape))
    cache_spec = lambda c: pl.BlockSpec((None, None) + c.shape[2:], lambda b: (l, b, 0, 0))
    in_specs = [
        pl.BlockSpec((t_new, A_WIDTH), lambda b: (b, 0)),
        pl.BlockSpec((t_new, B_WIDTH), lambda b: (b, 1)),
        pl.BlockSpec((t_new, A_WIDTH), lambda b: (b, 0)),
        pl.BlockSpec((t_new, A_WIDTH), lambda b: (b, 1)),
        pl.BlockSpec((t_new, B_KV_WIDTH), lambda b: (b, 0)),
        pl.BlockSpec((t_new, B_KV_WIDTH), lambda b: (b, 1)),
        cache_spec(cak), cache_spec(cav), cache_spec(cbk), cache_spec(cbv),
        const((rows, a_len)), const((rows, LANES)),
        const((rows, b_len)), const((rows, LANES)),
        const((rows, 1)),
    ]
    args = [qg, qg, kva32, kva32, kvb32, kvb32, cak, cav, cbk, cbv,
            bias_ac, bias_an, bias_bc, bias_bn, sink_col]
    aliases = {}
    if prev_out is not None:
        for k, arr in enumerate(prev_out):
            aliases[len(args)] = 2 + k
            in_specs.append(pl.BlockSpec(memory_space=pl.ANY))
            args.append(arr)
    return pl.pallas_call(
        functools.partial(_attn_sample_kernel, t_new=t_new),
        grid=(nb,),
        in_specs=in_specs,
        out_specs=[
            pl.BlockSpec((t_new, A_WIDTH), lambda b: (b, 0)),
            pl.BlockSpec((t_new, B_WIDTH), lambda b: (b, 0)),
            cache_spec(cak), cache_spec(cav), cache_spec(cbk), cache_spec(cbv),
        ],
        out_shape=[
            jax.ShapeDtypeStruct((m, A_WIDTH), jnp.bfloat16),
            jax.ShapeDtypeStruct((m, B_WIDTH), jnp.bfloat16),
        ] + [jax.ShapeDtypeStruct(c.shape, jnp.float32) for c in caches],
        input_output_aliases=aliases,
        compiler_params=_params(("parallel",)),
        name="attn_sample",
    )(*args)


def _cache_t_kernel(ka_ref, va_ref, kb_ref, vb_ref, *refs):
    ak_ref, av_ref, bk_ref, bv_ref = refs[-4:]
    ak_ref[...] = ka_ref[...].T
    av_ref[...] = va_ref[...].T
    bk_ref[...] = kb_ref[...].T
    bv_ref[...] = vb_ref[...].T


def _prompt_caches(kva32, kvb32, l, prev_out, *, depth, batch, seq, na, nb):
    a_tile = seq // na
    b_tile = seq // nb
    out_shapes = [(depth, batch, A_WIDTH, na)] * 2 + [(depth, batch, B_KV_WIDTH, nb)] * 2
    in_specs = [
        pl.BlockSpec((na, A_WIDTH), lambda b: ((b + 1) * a_tile - 1, 0)),
        pl.BlockSpec((na, A_WIDTH), lambda b: ((b + 1) * a_tile - 1, 1)),
        pl.BlockSpec((nb, B_KV_WIDTH), lambda b: ((b + 1) * b_tile - 1, 0)),
        pl.BlockSpec((nb, B_KV_WIDTH), lambda b: ((b + 1) * b_tile - 1, 1)),
    ]
    args = [kva32, kva32, kvb32, kvb32]
    aliases = {}
    if prev_out is not None:
        for k, arr in enumerate(prev_out):
            aliases[len(args)] = k
            in_specs.append(pl.BlockSpec(memory_space=pl.ANY))
            args.append(arr)
    return pl.pallas_call(
        _cache_t_kernel,
        grid=(batch,),
        in_specs=in_specs,
        out_specs=[pl.BlockSpec((None, None) + s[2:], lambda b: (l, b, 0, 0))
                   for s in out_shapes],
        out_shape=[jax.ShapeDtypeStruct(s, jnp.float32) for s in out_shapes],
        input_output_aliases=aliases,
        compiler_params=_params(("parallel",)),
        name="prompt_caches",
    )(*args)


def _merge_kernel(oa_ref, ob_ref, ga_ref, gb_ref, x_ref, wa_ref, wb_ref, wo_ref,
                  g_ref, y_ref):
    ta = jnp.dot(oa_ref[...], wa_ref[...], preferred_element_type=jnp.float32)
    tb = jnp.dot(ob_ref[...], wb_ref[...], preferred_element_type=jnp.float32)
    mixed = (jax.nn.sigmoid(ga_ref[...].astype(jnp.float32)) * ta
             + jax.nn.sigmoid(gb_ref[...].astype(jnp.float32)) * tb)
    z = jnp.dot(mixed.astype(jnp.bfloat16), wo_ref[...],
                preferred_element_type=jnp.float32)
    y_ref[...] = x_ref[...] + _rms_scale(z, g_ref[...])


def _merge(oa, ob, qg, x, wa, wb, wo, g, l, *, tm):
    m, d = x.shape
    ga_col = (A_WIDTH + B_WIDTH) // d
    resident = lambda w: pl.BlockSpec((None,) + w.shape[1:], lambda i: (l, 0, 0),
                                      pipeline_mode=pl.Buffered(1))
    return pl.pallas_call(
        _merge_kernel,
        grid=(m // tm,),
        in_specs=[
            pl.BlockSpec((tm, A_WIDTH), lambda i: (i, 0)),
            pl.BlockSpec((tm, B_WIDTH), lambda i: (i, 0)),
            pl.BlockSpec((tm, d), lambda i: (i, ga_col)),
            pl.BlockSpec((tm, d), lambda i: (i, ga_col + 1)),
            pl.BlockSpec((tm, d), lambda i: (i, 0)),
            resident(wa), resident(wb), resident(wo),
            _layer_vec_spec(l, d),
        ],
        out_specs=pl.BlockSpec((tm, d), lambda i: (i, 0)),
        out_shape=jax.ShapeDtypeStruct((m, d), jnp.float32),
        compiler_params=_params(("parallel",)),
        name="merge",
    )(oa, ob, qg, qg, x, wa, wb, wo, g)


def _ffn_kernel(x_ref, gpre_ref, wu_ref, wd_ref, gpost_ref, y_ref, h_ref):
    f = pl.program_id(1)

    @pl.when(f == 0)
    def _():
        h_ref[...] = _rms_scale(x_ref[...], gpre_ref[...]).astype(h_ref.dtype)

    u = jnp.dot(h_ref[...], wu_ref[...], preferred_element_type=jnp.float32)
    u = jnp.square(jnp.maximum(u, 0.0)).astype(jnp.bfloat16)
    part = jnp.dot(u, wd_ref[...], preferred_element_type=jnp.float32)

    @pl.when(f == 0)
    def _():
        y_ref[...] = part

    @pl.when(f > 0)
    def _():
        y_ref[...] += part

    @pl.when(f == pl.num_programs(1) - 1)
    def _():
        y_ref[...] = x_ref[...] + _rms_scale(y_ref[...], gpost_ref[...])


def _ffn(x, gpre, wu, wd, gpost, l, *, tm, tf):
    m, d = x.shape
    dff = wu.shape[-1]
    return pl.pallas_call(
        _ffn_kernel,
        grid=(m // tm, dff // tf),
        in_specs=[
            pl.BlockSpec((tm, d), lambda i, f: (i, 0)),
            _layer_vec_spec(l, d),
            pl.BlockSpec((None, d, tf), lambda i, f: (l, 0, f)),
            pl.BlockSpec((None, tf, d), lambda i, f: (l, f, 0)),
            _layer_vec_spec(l, d),
        ],
        out_specs=pl.BlockSpec((tm, d), lambda i, f: (i, 0)),
        out_shape=jax.ShapeDtypeStruct((m, d), jnp.float32),
        scratch_shapes=[pltpu.VMEM((tm, d), jnp.bfloat16)],
        compiler_params=_params(("parallel", "arbitrary")),
        name="ffn",
    )(x, gpre, wu, wd, gpost)


def _t5_bucket(rel):
    half = T5_BUCKETS // 2
    exact = half // 2
    ret = jnp.where(rel > 0, half, 0)
    n = jnp.abs(rel)
    large = exact + (jnp.log(jnp.maximum(n, 1).astype(jnp.float32) / exact)
                     / math.log(T5_MAX_DIST / exact) * (half - exact)).astype(jnp.int32)
    large = jnp.minimum(large, half - 1)
    return ret + jnp.where(n < exact, n, large)


def _a_bias_of_rel(table):
    return lambda rel: table.T[:, jnp.clip(rel, -A_REL_CLIP, A_REL_CLIP) + A_REL_CLIP]


def _b_bias_of_rel(table):
    return lambda rel: table.T[:, _t5_bucket(-rel)]


def _hankel(u, q, n):
    heads, k = u.shape
    period = q + n
    u = jnp.pad(u, ((0, 0), (0, period - k)))
    flat = jnp.tile(u, (1, q + 1))[:, :q * (period + 1)]
    return flat.reshape(heads, q, period + 1)[:, :, :n]


def _rel_bias(bias_of_rel, q_len, n_keys, k0):
    k = jnp.arange(q_len + n_keys - 1)
    u = bias_of_rel(k - (n_keys - 1) - k0)
    return _hankel(u, q_len, n_keys)[:, :, ::-1]


def _band_tile(bias_of_rel, tq, n_prev):
    window = tq + n_prev * CHUNK
    band = _rel_bias(bias_of_rel, CHUNK, (n_prev + 1) * CHUNK, -n_prev * CHUNK)
    blocks = []
    for c in range(tq // CHUNK):
        left = c * CHUNK
        right = window - left - band.shape[-1]
        blocks.append(jnp.pad(band, ((0, 0), (0, 0), (left, right)),
                              constant_values=NEG_INF))
    return jnp.concatenate(blocks, axis=1)


def _sample_bias(bias_of_rel, t_new, n_cache):
    full = _rel_bias(bias_of_rel, t_new, n_cache + t_new, -n_cache)
    full = full.reshape(-1, n_cache + t_new)
    new = jnp.pad(full[:, n_cache:], ((0, 0), (LANES - t_new, 0)),
                  constant_values=NEG_INF)
    return full[:, :n_cache], new


def _cache_t(c):
    d, s, r, h, e = c.shape
    return jnp.transpose(c, (0, 1, 3, 4, 2)).reshape(d, s, h * e, r)


def _cache_untranspose(c, heads):
    d, s, w, r = c.shape
    return jnp.transpose(c.reshape(d, s, heads, w // heads, r), (0, 1, 4, 2, 3))


def kernel(x_prompt, x_sample, cache_a_k, cache_a_v, cache_b_k, cache_b_v, w_in,
           w_a_out, w_b_out, w_out, a_rel_table, t5_table, b_sinks, g_mix_pre,
           g_mix_post, g_ffn_pre, g_ffn_post, w_up, w_down):
    depth = w_in.shape[0]
    batch, seq, d = x_prompt.shape
    dec_batch, t_new, _ = x_sample.shape
    a_len = cache_a_k.shape[2]
    b_len = cache_b_k.shape[2]
    bf16 = jnp.bfloat16
    ms = dec_batch * t_new

    yp = x_prompt.reshape(batch * seq, d)
    ys = x_sample.reshape(ms, d)

    w_in16 = w_in.astype(bf16)
    wa16 = w_a_out.astype(bf16)
    wb16 = w_b_out.astype(bf16)
    wo16 = w_out.astype(bf16)
    wu16 = w_up.astype(bf16)
    wd16 = w_down.astype(bf16)
    g_pre = g_mix_pre[:, None, :]
    g_post = g_mix_post[:, None, :]
    gf_pre = g_ffn_pre[:, None, :]
    gf_post = g_ffn_post[:, None, :]
    caches = tuple(_cache_t(c) for c in (cache_a_k, cache_a_v, cache_b_k, cache_b_v))

    b_of_rel = _b_bias_of_rel(t5_table)
    bias_b_tile = _band_tile(b_of_rel, B_TQ, B_LEFT_CHUNKS)
    bias_bc, bias_bn = _sample_bias(b_of_rel, t_new, b_len)
    na, nb = min(a_len, seq), min(b_len, seq)

    prompt_caches = None
    sample_caches = None
    for l in range(depth):
        a_of_rel = _a_bias_of_rel(a_rel_table[l])

        kva32, kva16, h = _norm_kv_proj(yp, g_pre, w_in16, l, tm=1024)
        kvb32, kvb16 = _kvb_proj(h, w_in16, l, tm=1024)
        qg = _qg_proj(h, w_in16, l, tm=1024)
        oa = _attn_a(qg, kva16, _band_tile(a_of_rel, A_TQ, A_LEFT_CHUNKS), seq=seq)
        ob = _attn_b(qg, kvb16, bias_b_tile, b_sinks[l], seq=seq)
        yp = _merge(oa, ob, qg, yp, wa16, wb16, wo16, g_post, l, tm=256)
        yp = _ffn(yp, gf_pre, wu16, wd16, gf_post, l, tm=512, tf=1024)
        prompt_caches = _prompt_caches(kva32, kvb32, l, prompt_caches, depth=depth,
                                       batch=batch, seq=seq, na=na, nb=nb)

        kva32, _, h = _norm_kv_proj(ys, g_pre, w_in16, l, tm=ms)
        kvb32, _ = _kvb_proj(h, w_in16, l, tm=ms)
        qg = _qg_proj(h, w_in16, l, tm=ms)
        bias_ac, bias_an = _sample_bias(a_of_rel, t_new, a_len)
        sink_col = jnp.repeat(b_sinks[l], t_new)[:, None]
        oa, ob, *sample_caches = _attn_sample(
            qg, kva32, kvb32, caches, l, sample_caches, bias_ac, bias_an, bias_bc,
            bias_bn, sink_col, t_new=t_new)
        ys = _merge(oa, ob, qg, ys, wa16, wb16, wo16, g_post, l, tm=ms)
        ys = _ffn(ys, gf_pre, wu16, wd16, gf_post, l, tm=ms, tf=1024)

    akp, avp, bkp, bvp = prompt_caches
    aks, avs, bks, bvs = sample_caches
    return (yp.reshape(batch, seq, d), ys.reshape(dec_batch, t_new, d),
            _cache_untranspose(akp, A_HEADS), _cache_untranspose(avp, A_HEADS),
            _cache_untranspose(bkp, B_KV_HEADS), _cache_untranspose(bvp, B_KV_HEADS),
            _cache_untranspose(aks, A_HEADS), _cache_untranspose(avs, A_HEADS),
            _cache_untranspose(bks, B_KV_HEADS), _cache_untranspose(bvs, B_KV_HEADS))
```

```python
import functools
import math

import jax
import jax.numpy as jnp
from jax import lax
from jax.experimental import pallas as pl
from jax.experimental.pallas import tpu as pltpu

D_MODEL = 2048
CHUNK = 64
HEAD_DIM = 64
A_HEADS = 16
A_WIDTH = A_HEADS * HEAD_DIM
A_LEFT_CHUNKS = 8
A_REL_CLIP = 256
B_HEADS = 16
B_KV_HEADS = 2
B_GROUP = B_HEADS // B_KV_HEADS
B_WIDTH = B_HEADS * HEAD_DIM
B_KV_WIDTH = B_KV_HEADS * HEAD_DIM
B_LEFT_CHUNKS = 2
T5_BUCKETS = 32
T5_MAX_DIST = 128
EPS = 1e-6
NEG_INF = -1e30
SCALE = HEAD_DIM ** -0.5

LANES = 128
MXU_COLS = 256

COL_QA = 0
COL_KA = A_WIDTH
COL_QB = 3 * A_WIDTH
COL_KB = 3 * A_WIDTH + B_WIDTH
COL_GA = COL_KB + 2 * B_KV_WIDTH
QG_COLS = A_WIDTH + B_WIDTH + 2 * D_MODEL
QG_TN = 4 * MXU_COLS

A_TQ = 256
B_TQ = 128
A_KBLOCKS = A_LEFT_CHUNKS * CHUNK // A_TQ + 1
B_KBLOCKS = B_LEFT_CHUNKS * CHUNK // B_TQ + 1
A_HEADS_PER_STEP = 16

VMEM_LIMIT = 56 * 1024 * 1024


def _params(sem, vmem=VMEM_LIMIT):
    return pltpu.CompilerParams(dimension_semantics=sem, vmem_limit_bytes=vmem)


def _rms_scale(x, g):
    return x * lax.rsqrt(jnp.mean(x * x, axis=-1, keepdims=True) + EPS) * g


def _layer_vec_spec(l, d):
    return pl.BlockSpec((None, 1, d), lambda *_: (l, 0, 0))


def _norm_proj_kernel(x_ref, g_ref, w_ref, o32_ref, o16_ref, h_ref):
    @pl.when(pl.program_id(1) == 0)
    def _():
        h_ref[...] = _rms_scale(x_ref[...], g_ref[...]).astype(h_ref.dtype)

    acc = jnp.dot(h_ref[...], w_ref[...], preferred_element_type=jnp.float32)
    o32_ref[...] = acc
    o16_ref[...] = acc.astype(o16_ref.dtype)


def _norm_kv_proj(x, g, w_in, l, *, tm):
    m, d = x.shape
    tn = A_WIDTH
    n = 2 * A_WIDTH
    col0 = COL_KA // tn
    return pl.pallas_call(
        _norm_proj_kernel,
        grid=(m // tm, n // tn),
        in_specs=[
            pl.BlockSpec((tm, d), lambda i, j: (i, 0)),
            _layer_vec_spec(l, d),
            pl.BlockSpec((None, d, tn), lambda i, j: (l, 0, col0 + j)),
        ],
        out_specs=[
            pl.BlockSpec((tm, tn), lambda i, j: (i, j)),
            pl.BlockSpec((tm, tn), lambda i, j: (i, j)),
            pl.BlockSpec((tm, d), lambda i, j: (i, 0)),
        ],
        out_shape=[
            jax.ShapeDtypeStruct((m, n), jnp.float32),
            jax.ShapeDtypeStruct((m, n), jnp.bfloat16),
            jax.ShapeDtypeStruct((m, d), jnp.bfloat16),
        ],
        compiler_params=_params(("parallel", "arbitrary")),
        name="norm_kv_proj",
    )(x, g, w_in)


def _kvb_kernel(h_ref, w_ref, o32_ref, o16_ref):
    acc = jnp.dot(h_ref[...], w_ref[...], preferred_element_type=jnp.float32)
    o32_ref[...] = acc
    o16_ref[...] = acc.astype(o16_ref.dtype)


def _kvb_proj(h, w_in, l, *, tm):
    m, d = h.shape
    n = 2 * B_KV_WIDTH
    col = COL_KB // n
    return pl.pallas_call(
        _kvb_kernel,
        grid=(m // tm,),
        in_specs=[
            pl.BlockSpec((tm, d), lambda i: (i, 0)),
            pl.BlockSpec((None, d, n), lambda i: (l, 0, col)),
        ],
        out_specs=[pl.BlockSpec((tm, n), lambda i: (i, 0))] * 2,
        out_shape=[
            jax.ShapeDtypeStruct((m, n), jnp.float32),
            jax.ShapeDtypeStruct((m, n), jnp.bfloat16),
        ],
        compiler_params=_params(("parallel",)),
        name="kvb_proj",
    )(h, w_in)


def _qg_kernel(h_ref, *refs):
    w_refs, o_ref = refs[:-1], refs[-1]
    h = h_ref[...]
    for k, w_ref in enumerate(w_refs):
        o_ref[:, k * MXU_COLS:(k + 1) * MXU_COLS] = jnp.dot(
            h, w_ref[...], preferred_element_type=jnp.float32).astype(o_ref.dtype)


def _qg_proj(h, w_in, l, *, tm):
    m, d = h.shape
    per_tile = QG_TN // MXU_COLS
    qa0, qb0, ga0 = (c // MXU_COLS for c in (COL_QA, COL_QB, COL_GA))
    n_q = A_WIDTH // QG_TN

    def w_map(k):
        def index_map(i, j):
            gate0 = ga0 + (j - 2 * n_q) * per_tile
            src = jnp.where(j < n_q, qa0 + j * per_tile,
                            jnp.where(j < 2 * n_q, qb0 + (j - n_q) * per_tile, gate0))
            return (l, 0, src + k)
        return index_map

    return pl.pallas_call(
        _qg_kernel,
        grid=(m // tm, QG_COLS // QG_TN),
        in_specs=[pl.BlockSpec((tm, d), lambda i, j: (i, 0))]
        + [pl.BlockSpec((None, d, MXU_COLS), w_map(k)) for k in range(per_tile)],
        out_specs=pl.BlockSpec((tm, QG_TN), lambda i, j: (i, j)),
        out_shape=jax.ShapeDtypeStruct((m, QG_COLS), jnp.bfloat16),
        compiler_params=_params(("parallel", "arbitrary")),
        name="qg_proj",
    )(h, *([w_in] * per_tile))


def _low_half():
    return lax.broadcasted_iota(jnp.int32, (1, LANES), 1) < HEAD_DIM


def _start_mask(tile_in_seq, n_kblocks, tq):
    lane = lax.broadcasted_iota(jnp.int32, (1, n_kblocks * tq), 1)
    first_valid = jnp.maximum(n_kblocks - 1 - tile_in_seq, 0) * tq
    return jnp.where(lane >= first_valid, 0.0, NEG_INF)


def _attn_a_kernel(q_ref, *refs, tiles_per_seq):
    k_refs = refs[:A_KBLOCKS]
    v_refs = refs[A_KBLOCKS:2 * A_KBLOCKS]
    bias_ref, o_ref = refs[2 * A_KBLOCKS:]
    tile_in_seq = pl.program_id(1) % tiles_per_seq
    low = _low_half()

    def heads(start):
        for p in range(A_HEADS_PER_STEP // 2):
            sl = slice(p * LANES, (p + 1) * LANES)
            q2 = q_ref[:, sl]
            k2 = jnp.concatenate([r[:, sl] for r in k_refs], axis=0)
            v2 = jnp.concatenate([r[:, sl] for r in v_refs], axis=0)
            qs = jnp.concatenate([jnp.where(low, q2, 0), jnp.where(low, 0, q2)],
                                 axis=0) * SCALE
            s_all = lax.dot_general(qs, k2, (((1,), (1,)), ((), ())),
                                    preferred_element_type=jnp.float32)
            probs, denoms = [], []
            for hh in range(2):
                s = s_all[hh * A_TQ:(hh + 1) * A_TQ] + bias_ref[2 * p + hh]
                if start is not None:
                    s = s + start
                e = jnp.exp(s - jnp.max(s, axis=-1, keepdims=True))
                denoms.append(jnp.sum(e, axis=-1, keepdims=True))
                probs.append(e.astype(v2.dtype))
            r = jnp.dot(jnp.concatenate(probs, axis=0), v2,
                        preferred_element_type=jnp.float32)
            o_ref[:, sl] = jnp.where(low, r[:A_TQ] / denoms[0],
                                     r[A_TQ:] / denoms[1]).astype(o_ref.dtype)

    @pl.when(tile_in_seq >= A_KBLOCKS - 1)
    def _():
        heads(None)

    @pl.when(tile_in_seq < A_KBLOCKS - 1)
    def _():
        heads(_start_mask(tile_in_seq, A_KBLOCKS, A_TQ))


def _attn_a(qg, kva16, bias, *, seq):
    m = qg.shape[0]
    tiles_per_seq = seq // A_TQ
    hw = A_HEADS_PER_STEP * HEAD_DIM
    n_hg = A_WIDTH // hw
    back = A_KBLOCKS - 1

    def kv_map(j, col0):
        def index_map(g, t):
            i = t % tiles_per_seq
            row = t - i + jnp.maximum(i - back + j, 0)
            return (row, col0 + g)
        return index_map

    k_specs = [pl.BlockSpec((A_TQ, hw), kv_map(j, 0)) for j in range(A_KBLOCKS)]
    v_specs = [pl.BlockSpec((A_TQ, hw), kv_map(j, n_hg)) for j in range(A_KBLOCKS)]
    return pl.pallas_call(
        functools.partial(_attn_a_kernel, tiles_per_seq=tiles_per_seq),
        grid=(n_hg, m // A_TQ),
        in_specs=[pl.BlockSpec((A_TQ, hw), lambda g, t: (t, g))] + k_specs + v_specs + [
            pl.BlockSpec((A_HEADS_PER_STEP, A_TQ, A_KBLOCKS * A_TQ),
                         lambda g, t: (g, 0, 0)),
        ],
        out_specs=pl.BlockSpec((A_TQ, hw), lambda g, t: (t, g)),
        out_shape=jax.ShapeDtypeStruct((m, A_WIDTH), jnp.bfloat16),
        compiler_params=_params(("parallel", "parallel")),
        name="attn_a",
    )(qg, *([kva16] * (2 * A_KBLOCKS)), bias)


def _dup_half(x, g):
    swapped = pltpu.roll(x, HEAD_DIM, 1)
    low = _low_half()
    return jnp.where(low, x, swapped) if g == 0 else jnp.where(low, swapped, x)


def _attn_b_kernel(sink_ref, q_ref, *refs, tiles_per_seq):
    k_refs = refs[:B_KBLOCKS]
    v_refs = refs[B_KBLOCKS:2 * B_KBLOCKS]
    bias_ref, o_ref = refs[2 * B_KBLOCKS:]
    start = _start_mask(pl.program_id(0) % tiles_per_seq, B_KBLOCKS, B_TQ)
    low = _low_half()
    k2 = jnp.concatenate([r[...] for r in k_refs], axis=0).astype(jnp.float32)
    v2 = jnp.concatenate([r[...] for r in v_refs], axis=0).astype(jnp.float32)
    pairs = B_GROUP // 2
    for g in range(B_KV_HEADS):
        kd = _dup_half(k2, g).astype(jnp.bfloat16)
        vd = _dup_half(v2, g).astype(jnp.bfloat16)
        stacked = []
        for p in range(pairs):
            c0 = (g * pairs + p) * LANES
            q2 = q_ref[:, c0:c0 + LANES]
            stacked.append(jnp.where(low, q2, 0) * SCALE)
            stacked.append(jnp.where(low, 0, q2) * SCALE)
        qs = jnp.concatenate(stacked, axis=0)
        s_all = lax.dot_general(qs, kd, (((1,), (1,)), ((), ())),
                                preferred_element_type=jnp.float32)
        probs, denoms = [], []
        for hl in range(B_GROUP):
            h = g * B_GROUP + hl
            s = s_all[hl * B_TQ:(hl + 1) * B_TQ] + bias_ref[h] + start
            sink = sink_ref[h]
            mx = jnp.maximum(jnp.max(s, axis=-1, keepdims=True), sink)
            e = jnp.exp(s - mx)
            denoms.append(jnp.sum(e, axis=-1, keepdims=True) + jnp.exp(sink - mx))
            probs.append(e.astype(jnp.bfloat16))
        r = jnp.dot(jnp.concatenate(probs, axis=0), vd,
                    preferred_element_type=jnp.float32)
        for p in range(pairs):
            c0 = (g * pairs + p) * LANES
            r0 = r[(2 * p) * B_TQ:(2 * p + 1) * B_TQ] / denoms[2 * p]
            r1 = r[(2 * p + 1) * B_TQ:(2 * p + 2) * B_TQ] / denoms[2 * p + 1]
            o_ref[:, c0:c0 + LANES] = jnp.where(low, r0, r1).astype(o_ref.dtype)


def _attn_b(qg, kvb16, bias, sinks, *, seq):
    m = qg.shape[0]
    tiles_per_seq = seq // B_TQ
    back = B_KBLOCKS - 1

    def kv_map(j, col):
        def index_map(t, sink_ref):
            i = t % tiles_per_seq
            return (t - i + jnp.maximum(i - back + j, 0), col)
        return index_map

    k_specs = [pl.BlockSpec((B_TQ, LANES), kv_map(j, 0)) for j in range(B_KBLOCKS)]
    v_specs = [pl.BlockSpec((B_TQ, LANES), kv_map(j, 1)) for j in range(B_KBLOCKS)]
    return pl.pallas_call(
        functools.partial(_attn_b_kernel, tiles_per_seq=tiles_per_seq),
        grid_spec=pltpu.PrefetchScalarGridSpec(
            num_scalar_prefetch=1,
            grid=(m // B_TQ,),
            in_specs=[pl.BlockSpec((B_TQ, B_WIDTH), lambda t, s: (t, 1))]
            + k_specs + v_specs + [
                pl.BlockSpec((B_HEADS, B_TQ, B_KBLOCKS * B_TQ), lambda t, s: (0, 0, 0)),
            ],
            out_specs=pl.BlockSpec((B_TQ, B_WIDTH), lambda t, s: (t, 0)),
        ),
        out_shape=jax.ShapeDtypeStruct((m, B_WIDTH), jnp.bfloat16),
        compiler_params=_params(("parallel",)),
        name="attn_b",
    )(sinks, qg, *([kvb16] * (2 * B_KBLOCKS)), bias)


def _attn_sample_kernel(qa_ref, qb_ref, kan_ref, van_ref, kbn_ref, vbn_ref,
                        cak_ref, cav_ref, cbk_ref, cbv_ref,
                        bias_ac_ref, bias_an_ref, bias_bc_ref, bias_bn_ref, sink_ref,
                        *refs, t_new):
    oa_ref, ob_ref, nak_ref, nav_ref, nbk_ref, nbv_ref = refs[-6:]
    heads = A_HEADS
    rows = heads * t_new
    width = heads * HEAD_DIM
    bf16 = jnp.bfloat16
    row_head = lax.broadcasted_iota(jnp.int32, (rows, width), 0) // t_new
    lane_head = lax.broadcasted_iota(jnp.int32, (rows, width), 1) // HEAD_DIM
    own = row_head == lane_head

    def stack_q(q):
        return jnp.where(own, jnp.concatenate([q] * heads, axis=0), 0) * SCALE

    def pad_top(x):
        return jnp.concatenate(
            [jnp.zeros((LANES - t_new, x.shape[1]), x.dtype), x], axis=0)

    def nt_dot(a, b):
        return lax.dot_general(a, b, (((1,), (1,)), ((), ())),
                               preferred_element_type=jnp.float32)

    def finish(s_c, s_n, vt_c, v_n, sink):
        mx = jnp.maximum(jnp.max(s_c, axis=-1, keepdims=True),
                         jnp.max(s_n, axis=-1, keepdims=True))
        if sink is not None:
            mx = jnp.maximum(mx, sink)
        e_c = jnp.exp(s_c - mx)
        e_n = jnp.exp(s_n - mx)
        denom = jnp.sum(e_c, axis=-1, keepdims=True) + jnp.sum(e_n, axis=-1, keepdims=True)
        if sink is not None:
            denom = denom + jnp.exp(sink - mx)
        o = nt_dot(e_c.astype(bf16), vt_c)
        o = o + jnp.dot(e_n.astype(bf16), v_n, preferred_element_type=jnp.float32)
        o = jnp.where(own, o / denom, 0.0)
        return jnp.sum(o.reshape(heads, t_new, width), axis=0)

    def roll_in(cache_t, new_pad):
        n = cache_t.shape[1]
        shifted = pltpu.roll(cache_t, n - t_new, 1)
        lane = lax.broadcasted_iota(jnp.int32, (1, LANES), 1)
        tail = jnp.where(lane < LANES - t_new, shifted[:, n - LANES:], new_pad.T)
        if n == LANES:
            return tail
        return jnp.concatenate([shifted[:, :n - LANES], tail], axis=1)

    kt_c = cak_ref[...]
    vt_c = cav_ref[...]
    k_n = pad_top(kan_ref[...])
    v_n = pad_top(van_ref[...])
    qs = stack_q(qa_ref[...])
    s_c = jnp.dot(qs, kt_c.astype(bf16), preferred_element_type=jnp.float32) + bias_ac_ref[...]
    s_n = nt_dot(qs, k_n.astype(bf16)) + bias_an_ref[...]
    oa_ref[...] = finish(s_c, s_n, vt_c.astype(bf16), v_n.astype(bf16), None).astype(oa_ref.dtype)
    nak_ref[...] = roll_in(kt_c, k_n)
    nav_ref[...] = roll_in(vt_c, v_n)

    def expansion(shape, src_axis):
        src = lax.broadcasted_iota(jnp.int32, shape, src_axis)
        dst = lax.broadcasted_iota(jnp.int32, shape, 1 - src_axis)
        return ((src % HEAD_DIM == dst % HEAD_DIM)
                & (src // HEAD_DIM == dst // (B_GROUP * HEAD_DIM))).astype(bf16)

    def widen(x):
        return jnp.dot(x.astype(bf16), expansion((B_KV_WIDTH, width), 0),
                       preferred_element_type=jnp.float32).astype(bf16)

    def widen_t(xt):
        return jnp.dot(expansion((width, B_KV_WIDTH), 1), xt.astype(bf16),
                       preferred_element_type=jnp.float32).astype(bf16)

    kt_c = cbk_ref[...]
    vt_c = cbv_ref[...]
    k_n = pad_top(kbn_ref[...])
    v_n = pad_top(vbn_ref[...])
    qs = stack_q(qb_ref[...])
    s_c = jnp.dot(qs, widen_t(kt_c), preferred_element_type=jnp.float32) + bias_bc_ref[...]
    s_n = nt_dot(qs, widen(k_n)) + bias_bn_ref[...]
    ob_ref[...] = finish(s_c, s_n, widen_t(vt_c), widen(v_n), sink_ref[...]).astype(ob_ref.dtype)
    nbk_ref[...] = roll_in(kt_c, k_n)
    nbv_ref[...] = roll_in(vt_c, v_n)


def _attn_sample(qg, kva32, kvb32, caches, l, prev_out, bias_ac, bias_an, bias_bc,
                 bias_bn, sink_col, *, t_new):
    m = qg.shape[0]
    nb = m // t_new
    cak, cav, cbk, cbv = caches
    a_len, b_len = cak.shape[-1], cbk.shape[-1]
    rows = A_HEADS * t_new
    const = lambda shape: pl.BlockSpec(shape, lambda b: (0,) * len(shape))
    cache_spec = lambda c: pl.BlockSpec((None, None) + c.shape[2:], lambda b: (l, b, 0, 0))
    in_specs = [
        pl.BlockSpec((t_new, A_WIDTH), lambda b: (b, 0)),
        pl.BlockSpec((t_new, B_WIDTH), lambda b: (b, 1)),
        pl.BlockSpec((t_new, A_WIDTH), lambda b: (b, 0)),
        pl.BlockSpec((t_new, A_WIDTH), lambda b: (b, 1)),
        pl.BlockSpec((t_new, B_KV_WIDTH), lambda b: (b, 0)),
        pl.BlockSpec((t_new, B_KV_WIDTH), lambda b: (b, 1)),
        cache_spec(cak), cache_spec(cav), cache_spec(cbk), cache_spec(cbv),
        const((rows, a_len)), const((rows, LANES)),
        const((rows, b_len)), const((rows, LANES)),
        const((rows, 1)),
    ]
    args = [qg, qg, kva32, kva32, kvb32, kvb32, cak, cav, cbk, cbv,
            bias_ac, bias_an, bias_bc, bias_bn, sink_col]
    aliases = {}
    if prev_out is not None:
        for k, arr in enumerate(prev_out):
            aliases[len(args)] = 2 + k
            in_specs.append(pl.BlockSpec(memory_space=pl.ANY))
            args.append(arr)
    return pl.pallas_call(
        functools.partial(_attn_sample_kernel, t_new=t_new),
        grid=(nb,),
        in_specs=in_specs,
        out_specs=[
            pl.BlockSpec((t_new, A_WIDTH), lambda b: (b, 0)),
            pl.BlockSpec((t_new, B_WIDTH), lambda b: (b, 0)),
            cache_spec(cak), cache_spec(cav), cache_spec(cbk), cache_spec(cbv),
        ],
        out_shape=[
            jax.ShapeDtypeStruct((m, A_WIDTH), jnp.bfloat16),
            jax.ShapeDtypeStruct((m, B_WIDTH), jnp.bfloat16),
        ] + [jax.ShapeDtypeStruct(c.shape, jnp.float32) for c in caches],
        input_output_aliases=aliases,
        compiler_params=_params(("parallel",)),
        name="attn_sample",
    )(*args)


def _cache_t_kernel(ka_ref, va_ref, kb_ref, vb_ref, *refs):
    ak_ref, av_ref, bk_ref, bv_ref = refs[-4:]
    ak_ref[...] = ka_ref[...].T
    av_ref[...] = va_ref[...].T
    bk_ref[...] = kb_ref[...].T
    bv_ref[...] = vb_ref[...].T


def _prompt_caches(kva32, kvb32, l, prev_out, *, depth, batch, seq, na, nb):
    a_tile = seq // na
    b_tile = seq // nb
    out_shapes = [(depth, batch, A_WIDTH, na)] * 2 + [(depth, batch, B_KV_WIDTH, nb)] * 2
    in_specs = [
        pl.BlockSpec((na, A_WIDTH), lambda b: ((b + 1) * a_tile - 1, 0)),
        pl.BlockSpec((na, A_WIDTH), lambda b: ((b + 1) * a_tile - 1, 1)),
        pl.BlockSpec((nb, B_KV_WIDTH), lambda b: ((b + 1) * b_tile - 1, 0)),
        pl.BlockSpec((nb, B_KV_WIDTH), lambda b: ((b + 1) * b_tile - 1, 1)),
    ]
    args = [kva32, kva32, kvb32, kvb32]
    aliases = {}
    if prev_out is not None:
        for k, arr in enumerate(prev_out):
            aliases[len(args)] = k
            in_specs.append(pl.BlockSpec(memory_space=pl.ANY))
            args.append(arr)
    return pl.pallas_call(
        _cache_t_kernel,
        grid=(batch,),
        in_specs=in_specs,
        out_specs=[pl.BlockSpec((None, None) + s[2:], lambda b: (l, b, 0, 0))
                   for s in out_shapes],
        out_shape=[jax.ShapeDtypeStruct(s, jnp.float32) for s in out_shapes],
        input_output_aliases=aliases,
        compiler_params=_params(("parallel",)),
        name="prompt_caches",
    )(*args)


def _merge_kernel(oa_ref, ob_ref, ga_ref, gb_ref, x_ref, wa_ref, wb_ref, wo_ref,
                  g_ref, y_ref):
    ta = jnp.dot(oa_ref[...], wa_ref[...], preferred_element_type=jnp.float32)
    tb = jnp.dot(ob_ref[...], wb_ref[...], preferred_element_type=jnp.float32)
    mixed = (jax.nn.sigmoid(ga_ref[...].astype(jnp.float32)) * ta
             + jax.nn.sigmoid(gb_ref[...].astype(jnp.float32)) * tb)
    z = jnp.dot(mixed.astype(jnp.bfloat16), wo_ref[...],
                preferred_element_type=jnp.float32)
    y_ref[...] = x_ref[...] + _rms_scale(z, g_ref[...])


def _merge(oa, ob, qg, x, wa, wb, wo, g, l, *, tm):
    m, d = x.shape
    ga_col = (A_WIDTH + B_WIDTH) // d
    resident = lambda w: pl.BlockSpec((None,) + w.shape[1:], lambda i: (l, 0, 0),
                                      pipeline_mode=pl.Buffered(1))
    return pl.pallas_call(
        _merge_kernel,
        grid=(m // tm,),
        in_specs=[
            pl.BlockSpec((tm, A_WIDTH), lambda i: (i, 0)),
            pl.BlockSpec((tm, B_WIDTH), lambda i: (i, 0)),
            pl.BlockSpec((tm, d), lambda i: (i, ga_col)),
            pl.BlockSpec((tm, d), lambda i: (i, ga_col + 1)),
            pl.BlockSpec((tm, d), lambda i: (i, 0)),
            resident(wa), resident(wb), resident(wo),
            _layer_vec_spec(l, d),
        ],
        out_specs=pl.BlockSpec((tm, d), lambda i: (i, 0)),
        out_shape=jax.ShapeDtypeStruct((m, d), jnp.float32),
        compiler_params=_params(("parallel",)),
        name="merge",
    )(oa, ob, qg, qg, x, wa, wb, wo, g)


def _ffn_kernel(x_ref, gpre_ref, wu_ref, wd_ref, gpost_ref, y_ref, h_ref):
    f = pl.program_id(1)

    @pl.when(f == 0)
    def _():
        h_ref[...] = _rms_scale(x_ref[...], gpre_ref[...]).astype(h_ref.dtype)
        y_ref[...] = jnp.zeros_like(y_ref)

    u = jnp.dot(h_ref[...], wu_ref[...], preferred_element_type=jnp.float32)
    u = jnp.square(jnp.maximum(u, 0.0)).astype(jnp.bfloat16)
    y_ref[...] += jnp.dot(u, wd_ref[...], preferred_element_type=jnp.float32)

    @pl.when(f == pl.num_programs(1) - 1)
    def _():
        y_ref[...] = x_ref[...] + _rms_scale(y_ref[...], gpost_ref[...])


def _ffn(x, gpre, wu, wd, gpost, l, *, tm, tf):
    m, d = x.shape
    dff = wu.shape[-1]
    return pl.pallas_call(
        _ffn_kernel,
        grid=(m // tm, dff // tf),
        in_specs=[
            pl.BlockSpec((tm, d), lambda i, f: (i, 0)),
            _layer_vec_spec(l, d),
            pl.BlockSpec((None, d, tf), lambda i, f: (l, 0, f)),
            pl.BlockSpec((None, tf, d), lambda i, f: (l, f, 0)),
            _layer_vec_spec(l, d),
        ],
        out_specs=pl.BlockSpec((tm, d), lambda i, f: (i, 0)),
        out_shape=jax.ShapeDtypeStruct((m, d), jnp.float32),
        scratch_shapes=[pltpu.VMEM((tm, d), jnp.bfloat16)],
        compiler_params=_params(("parallel", "arbitrary")),
        name="ffn",
    )(x, gpre, wu, wd, gpost)


def _t5_bucket(rel):
    half = T5_BUCKETS // 2
    exact = half // 2
    ret = jnp.where(rel > 0, half, 0)
    n = jnp.abs(rel)
    large = exact + (jnp.log(jnp.maximum(n, 1).astype(jnp.float32) / exact)
                     / math.log(T5_MAX_DIST / exact) * (half - exact)).astype(jnp.int32)
    large = jnp.minimum(large, half - 1)
    return ret + jnp.where(n < exact, n, large)


def _a_bias_of_rel(table):
    return lambda rel: table.T[:, jnp.clip(rel, -A_REL_CLIP, A_REL_CLIP) + A_REL_CLIP]


def _b_bias_of_rel(table):
    return lambda rel: table.T[:, _t5_bucket(-rel)]


def _hankel(u, q, n):
    heads, k = u.shape
    period = q + n
    u = jnp.pad(u, ((0, 0), (0, period - k)))
    flat = jnp.tile(u, (1, q + 1))[:, :q * (period + 1)]
    return flat.reshape(heads, q, period + 1)[:, :, :n]


def _rel_bias(bias_of_rel, q_len, n_keys, k0):
    k = jnp.arange(q_len + n_keys - 1)
    u = bias_of_rel(k - (n_keys - 1) - k0)
    return _hankel(u, q_len, n_keys)[:, :, ::-1]


def _band_tile(bias_of_rel, tq, n_prev):
    window = tq + n_prev * CHUNK
    band = _rel_bias(bias_of_rel, CHUNK, (n_prev + 1) * CHUNK, -n_prev * CHUNK)
    blocks = []
    for c in range(tq // CHUNK):
        left = c * CHUNK
        right = window - left - band.shape[-1]
        blocks.append(jnp.pad(band, ((0, 0), (0, 0), (left, right)),
                              constant_values=NEG_INF))
    return jnp.concatenate(blocks, axis=1)


def _sample_bias(bias_of_rel, t_new, n_cache):
    full = _rel_bias(bias_of_rel, t_new, n_cache + t_new, -n_cache)
    full = full.reshape(-1, n_cache + t_new)
    new = jnp.pad(full[:, n_cache:], ((0, 0), (LANES - t_new, 0)),
                  constant_values=NEG_INF)
    return full[:, :n_cache], new


def _cache_t(c):
    d, s, r, h, e = c.shape
    return jnp.transpose(c, (0, 1, 3, 4, 2)).reshape(d, s, h * e, r)


def _cache_untranspose(c, heads):
    d, s, w, r = c.shape
    return jnp.transpose(c.reshape(d, s, heads, w // heads, r), (0, 1, 4, 2, 3))


def kernel(x_prompt, x_sample, cache_a_k, cache_a_v, cache_b_k, cache_b_v, w_in,
           w_a_out, w_b_out, w_out, a_rel_table, t5_table, b_sinks, g_mix_pre,
           g_mix_post, g_ffn_pre, g_ffn_post, w_up, w_down):
    depth = w_in.shape[0]
    batch, seq, d = x_prompt.shape
    dec_batch, t_new, _ = x_sample.shape
    a_len = cache_a_k.shape[2]
    b_len = cache_b_k.shape[2]
    bf16 = jnp.bfloat16
    ms = dec_batch * t_new

    yp = x_prompt.reshape(batch * seq, d)
    ys = x_sample.reshape(ms, d)

    w_in16 = w_in.astype(bf16)
    wa16 = w_a_out.astype(bf16)
    wb16 = w_b_out.astype(bf16)
    wo16 = w_out.astype(bf16)
    wu16 = w_up.astype(bf16)
    wd16 = w_down.astype(bf16)
    g_pre = g_mix_pre[:, None, :]
    g_post = g_mix_post[:, None, :]
    gf_pre = g_ffn_pre[:, None, :]
    gf_post = g_ffn_post[:, None, :]
    caches = tuple(_cache_t(c) for c in (cache_a_k, cache_a_v, cache_b_k, cache_b_v))

    b_of_rel = _b_bias_of_rel(t5_table)
    bias_b_tile = _band_tile(b_of_rel, B_TQ, B_LEFT_CHUNKS)
    bias_bc, bias_bn = _sample_bias(b_of_rel, t_new, b_len)
    na, nb = min(a_len, seq), min(b_len, seq)

    prompt_caches = None
    sample_caches = None
    for l in range(depth):
        a_of_rel = _a_bias_of_rel(a_rel_table[l])

        kva32, kva16, h = _norm_kv_proj(yp, g_pre, w_in16, l, tm=1024)
        kvb32, kvb16 = _kvb_proj(h, w_in16, l, tm=1024)
        qg = _qg_proj(h, w_in16, l, tm=1024)
        oa = _attn_a(qg, kva16, _band_tile(a_of_rel, A_TQ, A_LEFT_CHUNKS), seq=seq)
        ob = _attn_b(qg, kvb16, bias_b_tile, b_sinks[l], seq=seq)
        yp = _merge(oa, ob, qg, yp, wa16, wb16, wo16, g_post, l, tm=256)
        yp = _ffn(yp, gf_pre, wu16, wd16, gf_post, l, tm=512, tf=1024)
        prompt_caches = _prompt_caches(kva32, kvb32, l, prompt_caches, depth=depth,
                                       batch=batch, seq=seq, na=na, nb=nb)

        kva32, _, h = _norm_kv_proj(ys, g_pre, w_in16, l, tm=ms)
        kvb32, _ = _kvb_proj(h, w_in16, l, tm=ms)
        qg = _qg_proj(h, w_in16, l, tm=ms)
        bias_ac, bias_an = _sample_bias(a_of_rel, t_new, a_len)
        sink_col = jnp.repeat(b_sinks[l], t_new)[:, None]
        oa, ob, *sample_caches = _attn_sample(
            qg, kva32, kvb32, caches, l, sample_caches, bias_ac, bias_an, bias_bc,
            bias_bn, sink_col, t_new=t_new)
        ys = _merge(oa, ob, qg, ys, wa16, wb16, wo16, g_post, l, tm=ms)
        ys = _ffn(ys, gf_pre, wu16, wd16, gf_post, l, tm=ms, tf=1024)

    akp, avp, bkp, bvp = prompt_caches
    aks, avs, bks, bvs = sample_caches
    return (yp.reshape(batch, seq, d), ys.reshape(dec_batch, t_new, d),
            _cache_untranspose(akp, A_HEADS), _cache_untranspose(avp, A_HEADS),
            _cache_untranspose(bkp, B_KV_HEADS), _cache_untranspose(bvp, B_KV_HEADS),
            _cache_untranspose(aks, A_HEADS), _cache_untranspose(avs, A_HEADS),
            _cache_untranspose(bks, B_KV_HEADS), _cache_untranspose(bvs, B_KV_HEADS))
```

```python
import functools
import math

import jax
import jax.numpy as jnp
from jax import lax
from jax.experimental import pallas as pl
from jax.experimental.pallas import tpu as pltpu

D_MODEL = 2048
CHUNK = 64
HEAD_DIM = 64
A_HEADS = 16
A_WIDTH = A_HEADS * HEAD_DIM
A_LEFT_CHUNKS = 8
A_REL_CLIP = 256
B_HEADS = 16
B_KV_HEADS = 2
B_GROUP = B_HEADS // B_KV_HEADS
B_WIDTH = B_HEADS * HEAD_DIM
B_KV_WIDTH = B_KV_HEADS * HEAD_DIM
B_LEFT_CHUNKS = 2
T5_BUCKETS = 32
T5_MAX_DIST = 128
EPS = 1e-6
NEG_INF = -1e30
SCALE = HEAD_DIM ** -0.5

LANES = 128
MXU_COLS = 256

COL_QA = 0
COL_KA = A_WIDTH
COL_VA = 2 * A_WIDTH
COL_QB = 3 * A_WIDTH
COL_KB = 3 * A_WIDTH + B_WIDTH
COL_GA = COL_KB + 2 * B_KV_WIDTH
PROJ_TN = 4 * MXU_COLS
QKV_KA, QKV_VA, QKV_QA, QKV_QB = 0, 1, 2, 3

A_TQ = 256
B_TQ = 128
A_KBLOCKS = A_LEFT_CHUNKS * CHUNK // A_TQ + 1
B_KBLOCKS = B_LEFT_CHUNKS * CHUNK // B_TQ + 1

VMEM_LIMIT = 56 * 1024 * 1024


def _params(sem, vmem=VMEM_LIMIT):
    return pltpu.CompilerParams(dimension_semantics=sem, vmem_limit_bytes=vmem)


def _rms_scale(x, g):
    return x * lax.rsqrt(jnp.mean(x * x, axis=-1, keepdims=True) + EPS) * g


def _layer_vec_spec(l, d):
    return pl.BlockSpec((None, 1, d), lambda *_: (l, 0, 0))


PROJ_KV_STEPS = 2
PROJ_Q_STEPS = 4
PROJ_KVB_STEP = 4
PROJ_GATE_STEP0 = 5


def _in_proj_kernel(x_ref, g_ref, *refs):
    n_w = PROJ_TN // MXU_COLS
    w_refs = refs[:n_w]
    qkv_ref, kvb16_ref, gates_ref, kva32_ref, kvb32_ref, h_ref = refs[n_w:]
    j = pl.program_id(1)

    @pl.when(j == 0)
    def _():
        h_ref[...] = _rms_scale(x_ref[...], g_ref[...]).astype(h_ref.dtype)

    def tile(out16_ref, out32_ref):
        h = h_ref[...]
        for k, w_ref in enumerate(w_refs):
            cols = slice(k * MXU_COLS, (k + 1) * MXU_COLS)
            acc = jnp.dot(h, w_ref[...], preferred_element_type=jnp.float32)
            out16_ref[:, cols] = acc.astype(out16_ref.dtype)
            if out32_ref is not None:
                out32_ref[:, cols] = acc

    @pl.when(j < PROJ_KV_STEPS)
    def _():
        tile(qkv_ref, kva32_ref)

    @pl.when((j >= PROJ_KV_STEPS) & (j < PROJ_Q_STEPS))
    def _():
        tile(qkv_ref, None)

    @pl.when(j == PROJ_KVB_STEP)
    def _():
        acc = jnp.dot(h_ref[...], w_refs[0][...], preferred_element_type=jnp.float32)
        kvb16_ref[...] = acc.astype(kvb16_ref.dtype)
        kvb32_ref[...] = acc

    @pl.when(j >= PROJ_GATE_STEP0)
    def _():
        tile(gates_ref, None)


def _in_proj(x, g, w_in, l, *, tm):
    m, d = x.shape
    n_w = PROJ_TN // MXU_COLS
    n_gate = 2 * D_MODEL // PROJ_TN
    src = [c // MXU_COLS for c in (COL_KA, COL_VA, COL_QA, COL_QB, COL_KB)]
    ga0 = COL_GA // MXU_COLS

    def w_map(k):
        def index_map(i, j):
            base = src[PROJ_KVB_STEP] if k == 0 else src[PROJ_KVB_STEP - 1] + k
            for step in range(PROJ_KVB_STEP - 1, -1, -1):
                base = jnp.where(j == step, src[step] + k, base)
            gate = ga0 + (j - PROJ_GATE_STEP0) * n_w + k
            return (l, 0, jnp.where(j >= PROJ_GATE_STEP0, gate, base))
        return index_map

    kvb_n = 2 * B_KV_WIDTH
    return pl.pallas_call(
        _in_proj_kernel,
        grid=(m // tm, PROJ_GATE_STEP0 + n_gate),
        in_specs=[
            pl.BlockSpec((tm, d), lambda i, j: (i, 0)),
            _layer_vec_spec(l, d),
        ] + [pl.BlockSpec((None, d, MXU_COLS), w_map(k)) for k in range(n_w)],
        out_specs=[
            pl.BlockSpec((tm, PROJ_TN), lambda i, j: (i, jnp.minimum(j, PROJ_Q_STEPS - 1))),
            pl.BlockSpec((tm, kvb_n), lambda i, j: (i, 0)),
            pl.BlockSpec((tm, PROJ_TN),
                         lambda i, j: (i, jnp.clip(j - PROJ_GATE_STEP0, 0, n_gate - 1))),
            pl.BlockSpec((tm, PROJ_TN), lambda i, j: (i, jnp.minimum(j, PROJ_KV_STEPS - 1))),
            pl.BlockSpec((tm, kvb_n), lambda i, j: (i, 0)),
        ],
        out_shape=[
            jax.ShapeDtypeStruct((m, PROJ_Q_STEPS * PROJ_TN), jnp.bfloat16),
            jax.ShapeDtypeStruct((m, kvb_n), jnp.bfloat16),
            jax.ShapeDtypeStruct((m, 2 * D_MODEL), jnp.bfloat16),
            jax.ShapeDtypeStruct((m, PROJ_KV_STEPS * PROJ_TN), jnp.float32),
            jax.ShapeDtypeStruct((m, kvb_n), jnp.float32),
        ],
        scratch_shapes=[pltpu.VMEM((tm, d), jnp.bfloat16)],
        compiler_params=_params(("parallel", "arbitrary")),
        name="in_proj",
    )(x, g, *([w_in] * n_w))


def _low_half():
    return lax.broadcasted_iota(jnp.int32, (1, LANES), 1) < HEAD_DIM


def _start_mask(tile_in_seq, n_kblocks, tq):
    lane = lax.broadcasted_iota(jnp.int32, (1, n_kblocks * tq), 1)
    first_valid = jnp.maximum(n_kblocks - 1 - tile_in_seq, 0) * tq
    return jnp.where(lane >= first_valid, 0.0, NEG_INF)


def _attn_a_kernel(q_ref, *refs, tiles_per_seq):
    k_refs = refs[:A_KBLOCKS]
    v_refs = refs[A_KBLOCKS:2 * A_KBLOCKS]
    bias_ref, o_ref = refs[2 * A_KBLOCKS:]
    tile_in_seq = pl.program_id(0) % tiles_per_seq
    low = _low_half()

    def heads(start):
        for p in range(A_HEADS // 2):
            sl = slice(p * LANES, (p + 1) * LANES)
            q2 = q_ref[:, sl]
            k2 = jnp.concatenate([r[:, sl] for r in k_refs], axis=0)
            v2 = jnp.concatenate([r[:, sl] for r in v_refs], axis=0)
            qs = jnp.concatenate([jnp.where(low, q2, 0), jnp.where(low, 0, q2)],
                                 axis=0) * SCALE
            s_all = lax.dot_general(qs, k2, (((1,), (1,)), ((), ())),
                                    preferred_element_type=jnp.float32)
            probs, denoms = [], []
            for hh in range(2):
                s = s_all[hh * A_TQ:(hh + 1) * A_TQ] + bias_ref[2 * p + hh]
                if start is not None:
                    s = s + start
                e = jnp.exp(s - jnp.max(s, axis=-1, keepdims=True))
                denoms.append(jnp.sum(e, axis=-1, keepdims=True))
                probs.append(e.astype(v2.dtype))
            r = jnp.dot(jnp.concatenate(probs, axis=0), v2,
                        preferred_element_type=jnp.float32)
            o_ref[:, sl] = jnp.where(low, r[:A_TQ] / denoms[0],
                                     r[A_TQ:] / denoms[1]).astype(o_ref.dtype)

    @pl.when(tile_in_seq >= A_KBLOCKS - 1)
    def _():
        heads(None)

    @pl.when(tile_in_seq < A_KBLOCKS - 1)
    def _():
        heads(_start_mask(tile_in_seq, A_KBLOCKS, A_TQ))


def _band_kv_map(j, col, n_kblocks, tiles_per_seq):
    def index_map(t, *_):
        i = t % tiles_per_seq
        return (t - i + jnp.maximum(i - (n_kblocks - 1) + j, 0), col)
    return index_map


def _attn_a(qkv, bias, *, m, seq):
    tiles_per_seq = seq // A_TQ
    kv_spec = lambda j, col: pl.BlockSpec(
        (A_TQ, A_WIDTH), _band_kv_map(j, col, A_KBLOCKS, tiles_per_seq))
    return pl.pallas_call(
        functools.partial(_attn_a_kernel, tiles_per_seq=tiles_per_seq),
        grid=(m // A_TQ,),
        in_specs=[pl.BlockSpec((A_TQ, A_WIDTH), lambda t: (t, QKV_QA))]
        + [kv_spec(j, QKV_KA) for j in range(A_KBLOCKS)]
        + [kv_spec(j, QKV_VA) for j in range(A_KBLOCKS)]
        + [pl.BlockSpec((A_HEADS, A_TQ, A_KBLOCKS * A_TQ), lambda t: (0, 0, 0),
                        pipeline_mode=pl.Buffered(1))],
        out_specs=pl.BlockSpec((A_TQ, A_WIDTH), lambda t: (t, 0)),
        out_shape=jax.ShapeDtypeStruct((m, A_WIDTH), jnp.bfloat16),
        compiler_params=_params(("parallel",)),
        name="attn_a",
    )(qkv, *([qkv] * (2 * A_KBLOCKS)), bias)


def _attn_b_kernel(sink_ref, q_ref, *refs, tiles_per_seq):
    k_refs = refs[:B_KBLOCKS]
    v_refs = refs[B_KBLOCKS:2 * B_KBLOCKS]
    bias_ref, o_ref = refs[2 * B_KBLOCKS:]
    tile_in_seq = pl.program_id(0) % tiles_per_seq
    low = _low_half()
    pairs = B_GROUP // 2

    def heads(start):
        k2 = jnp.concatenate([r[...] for r in k_refs], axis=0).astype(jnp.float32)
        v2 = jnp.concatenate([r[...] for r in v_refs], axis=0).astype(jnp.float32)
        for g in range(B_KV_HEADS):
            kg = k2 if g == 0 else pltpu.roll(k2, HEAD_DIM, 1)
            vg = v2 if g == 0 else pltpu.roll(v2, HEAD_DIM, 1)
            kd = jnp.where(low, kg, pltpu.roll(kg, HEAD_DIM, 1)).astype(jnp.bfloat16)
            vd = jnp.where(low, vg, pltpu.roll(vg, HEAD_DIM, 1)).astype(jnp.bfloat16)
            stacked = []
            for p in range(pairs):
                c0 = (g * pairs + p) * LANES
                q2 = q_ref[:, c0:c0 + LANES]
                stacked.append(jnp.where(low, q2, 0) * SCALE)
                stacked.append(jnp.where(low, 0, q2) * SCALE)
            qs = jnp.concatenate(stacked, axis=0)
            s_all = lax.dot_general(qs, kd, (((1,), (1,)), ((), ())),
                                    preferred_element_type=jnp.float32)
            probs, denoms = [], []
            for hl in range(B_GROUP):
                h = g * B_GROUP + hl
                s = s_all[hl * B_TQ:(hl + 1) * B_TQ] + bias_ref[h]
                if start is not None:
                    s = s + start
                sink = sink_ref[h]
                mx = jnp.maximum(jnp.max(s, axis=-1, keepdims=True), sink)
                e = jnp.exp(s - mx)
                denoms.append(jnp.sum(e, axis=-1, keepdims=True) + jnp.exp(sink - mx))
                probs.append(e.astype(jnp.bfloat16))
            r = jnp.dot(jnp.concatenate(probs, axis=0), vd,
                        preferred_element_type=jnp.float32)
            for p in range(pairs):
                c0 = (g * pairs + p) * LANES
                r0 = r[(2 * p) * B_TQ:(2 * p + 1) * B_TQ] / denoms[2 * p]
                r1 = r[(2 * p + 1) * B_TQ:(2 * p + 2) * B_TQ] / denoms[2 * p + 1]
                o_ref[:, c0:c0 + LANES] = jnp.where(low, r0, r1).astype(o_ref.dtype)

    @pl.when(tile_in_seq >= B_KBLOCKS - 1)
    def _():
        heads(None)

    @pl.when(tile_in_seq < B_KBLOCKS - 1)
    def _():
        heads(_start_mask(tile_in_seq, B_KBLOCKS, B_TQ))


def _attn_b(qkv, kvb16, bias, sinks, *, m, seq):
    tiles_per_seq = seq // B_TQ
    kv_spec = lambda j, col: pl.BlockSpec(
        (B_TQ, LANES), _band_kv_map(j, col, B_KBLOCKS, tiles_per_seq))
    return pl.pallas_call(
        functools.partial(_attn_b_kernel, tiles_per_seq=tiles_per_seq),
        grid_spec=pltpu.PrefetchScalarGridSpec(
            num_scalar_prefetch=1,
            grid=(m // B_TQ,),
            in_specs=[pl.BlockSpec((B_TQ, B_WIDTH), lambda t, s: (t, QKV_QB))]
            + [kv_spec(j, 0) for j in range(B_KBLOCKS)]
            + [kv_spec(j, 1) for j in range(B_KBLOCKS)]
            + [pl.BlockSpec((B_HEADS, B_TQ, B_KBLOCKS * B_TQ), lambda t, s: (0, 0, 0))],
            out_specs=pl.BlockSpec((B_TQ, B_WIDTH), lambda t, s: (t, 0)),
        ),
        out_shape=jax.ShapeDtypeStruct((m, B_WIDTH), jnp.bfloat16),
        compiler_params=_params(("parallel",)),
        name="attn_b",
    )(sinks, qkv, *([kvb16] * (2 * B_KBLOCKS)), bias)


def _pad_top(x, t_new):
    return jnp.concatenate([jnp.zeros((LANES - t_new, x.shape[1]), x.dtype), x], axis=0)


def _attn_sample_kernel(qa_ref, qb_ref, kan_ref, van_ref, kbn_ref, vbn_ref,
                        cak_ref, cav_ref, cbk_ref, cbv_ref,
                        bias_ac_ref, bias_an_ref, bias_bc_ref, bias_bn_ref, sink_ref,
                        oa_ref, ob_ref, *, t_new):
    heads = A_HEADS
    rows = heads * t_new
    width = heads * HEAD_DIM
    bf16 = jnp.bfloat16
    row_head = lax.broadcasted_iota(jnp.int32, (rows, width), 0) // t_new
    lane_head = lax.broadcasted_iota(jnp.int32, (rows, width), 1) // HEAD_DIM
    own = row_head == lane_head

    def stack_q(q):
        return jnp.where(own, jnp.concatenate([q] * heads, axis=0), 0) * SCALE

    def nt_dot(a, b):
        return lax.dot_general(a, b, (((1,), (1,)), ((), ())),
                               preferred_element_type=jnp.float32)

    def finish(s_c, s_n, vt_c, v_n, sink):
        mx = jnp.maximum(jnp.max(s_c, axis=-1, keepdims=True),
                         jnp.max(s_n, axis=-1, keepdims=True))
        if sink is not None:
            mx = jnp.maximum(mx, sink)
        e_c = jnp.exp(s_c - mx)
        e_n = jnp.exp(s_n - mx)
        denom = jnp.sum(e_c, axis=-1, keepdims=True) + jnp.sum(e_n, axis=-1, keepdims=True)
        if sink is not None:
            denom = denom + jnp.exp(sink - mx)
        o = nt_dot(e_c.astype(bf16), vt_c)
        o = o + jnp.dot(e_n.astype(bf16), v_n, preferred_element_type=jnp.float32)
        o = jnp.where(own, o / denom, 0.0)
        return jnp.sum(o.reshape(heads, t_new, width), axis=0)

    k_n = _pad_top(kan_ref[...], t_new).astype(bf16)
    v_n = _pad_top(van_ref[...], t_new).astype(bf16)
    qs = stack_q(qa_ref[...])
    s_c = jnp.dot(qs, cak_ref[...].astype(bf16),
                  preferred_element_type=jnp.float32) + bias_ac_ref[...]
    s_n = nt_dot(qs, k_n) + bias_an_ref[...]
    oa_ref[...] = finish(s_c, s_n, cav_ref[...].astype(bf16), v_n, None).astype(oa_ref.dtype)

    def expansion(shape, src_axis):
        src = lax.broadcasted_iota(jnp.int32, shape, src_axis)
        dst = lax.broadcasted_iota(jnp.int32, shape, 1 - src_axis)
        return ((src % HEAD_DIM == dst % HEAD_DIM)
                & (src // HEAD_DIM == dst // (B_GROUP * HEAD_DIM))).astype(bf16)

    def widen(x):
        return jnp.dot(x.astype(bf16), expansion((B_KV_WIDTH, width), 0),
                       preferred_element_type=jnp.float32).astype(bf16)

    def widen_t(xt):
        return jnp.dot(expansion((width, B_KV_WIDTH), 1), xt.astype(bf16),
                       preferred_element_type=jnp.float32).astype(bf16)

    k_n = widen(_pad_top(kbn_ref[...], t_new))
    v_n = widen(_pad_top(vbn_ref[...], t_new))
    qs = stack_q(qb_ref[...])
    s_c = jnp.dot(qs, widen_t(cbk_ref[...]),
                  preferred_element_type=jnp.float32) + bias_bc_ref[...]
    s_n = nt_dot(qs, k_n) + bias_bn_ref[...]
    ob_ref[...] = finish(s_c, s_n, widen_t(cbv_ref[...]), v_n, sink_ref[...]).astype(ob_ref.dtype)


def _attn_sample(qkv, kva32, kvb32, caches, l, bias_ac, bias_an, bias_bc, bias_bn,
                 sink_col, *, t_new):
    m = qkv.shape[0]
    cak, cav, cbk, cbv = caches
    a_len, b_len = cak.shape[-1], cbk.shape[-1]
    rows = A_HEADS * t_new
    const = lambda shape: pl.BlockSpec(shape, lambda b: (0,) * len(shape))
    cache_spec = lambda c: pl.BlockSpec((None, None) + c.shape[2:], lambda b: (l, b, 0, 0))
    return pl.pallas_call(
        functools.partial(_attn_sample_kernel, t_new=t_new),
        grid=(m // t_new,),
        in_specs=[
            pl.BlockSpec((t_new, A_WIDTH), lambda b: (b, QKV_QA)),
            pl.BlockSpec((t_new, B_WIDTH), lambda b: (b, QKV_QB)),
            pl.BlockSpec((t_new, A_WIDTH), lambda b: (b, 0)),
            pl.BlockSpec((t_new, A_WIDTH), lambda b: (b, 1)),
            pl.BlockSpec((t_new, B_KV_WIDTH), lambda b: (b, 0)),
            pl.BlockSpec((t_new, B_KV_WIDTH), lambda b: (b, 1)),
            cache_spec(cak), cache_spec(cav), cache_spec(cbk), cache_spec(cbv),
            const((rows, a_len)), const((rows, LANES)),
            const((rows, b_len)), const((rows, LANES)),
            const((rows, 1)),
        ],
        out_specs=[
            pl.BlockSpec((t_new, A_WIDTH), lambda b: (b, 0)),
            pl.BlockSpec((t_new, B_WIDTH), lambda b: (b, 0)),
        ],
        out_shape=[
            jax.ShapeDtypeStruct((m, A_WIDTH), jnp.bfloat16),
            jax.ShapeDtypeStruct((m, B_WIDTH), jnp.bfloat16),
        ],
        compiler_params=_params(("parallel",)),
        name="attn_sample",
    )(qkv, qkv, kva32, kva32, kvb32, kvb32, cak, cav, cbk, cbv,
      bias_ac, bias_an, bias_bc, bias_bn, sink_col)


def _roll_caches_kernel(kan_ref, van_ref, kbn_ref, vbn_ref,
                        cak_ref, cav_ref, cbk_ref, cbv_ref,
                        nak_ref, nav_ref, nbk_ref, nbv_ref, *, t_new):
    def roll_in(cache_ref, new_ref, out_ref):
        cache_t = cache_ref[...]
        n = cache_t.shape[1]
        shifted = pltpu.roll(cache_t, n - t_new, 1)
        lane = lax.broadcasted_iota(jnp.int32, (1, LANES), 1)
        tail = jnp.where(lane < LANES - t_new, shifted[:, n - LANES:],
                         _pad_top(new_ref[...], t_new).T)
        if n > LANES:
            out_ref[:, :n - LANES] = shifted[:, :n - LANES]
        out_ref[:, n - LANES:] = tail

    roll_in(cak_ref, kan_ref, nak_ref)
    roll_in(cav_ref, van_ref, nav_ref)
    roll_in(cbk_ref, kbn_ref, nbk_ref)
    roll_in(cbv_ref, vbn_ref, nbv_ref)


def _roll_caches(new_a, new_b, caches, *, t_new):
    depth, streams = caches[0].shape[:2]
    cache_spec = lambda c: pl.BlockSpec((None, None) + c.shape[2:], lambda l, b: (l, b, 0, 0))
    new_spec = lambda width, col: pl.BlockSpec((None, t_new, width), lambda l, b: (l, b, col))
    return pl.pallas_call(
        functools.partial(_roll_caches_kernel, t_new=t_new),
        grid=(depth, streams),
        in_specs=[new_spec(A_WIDTH, 0), new_spec(A_WIDTH, 1),
                  new_spec(B_KV_WIDTH, 0), new_spec(B_KV_WIDTH, 1)]
        + [cache_spec(c) for c in caches],
        out_specs=[cache_spec(c) for c in caches],
        out_shape=[jax.ShapeDtypeStruct(c.shape, jnp.float32) for c in caches],
        compiler_params=_params(("parallel", "parallel")),
        name="roll_caches",
    )(new_a, new_a, new_b, new_b, *caches)


def _prompt_caches_kernel(*refs, depth):
    in_refs, (ak_ref, av_ref, bk_ref, bv_ref) = refs[:4 * depth], refs[4 * depth:]
    for k in range(depth):
        @pl.when(pl.program_id(0) == k)
        def _(k=k):
            ka_ref, va_ref, kb_ref, vb_ref = in_refs[4 * k:4 * k + 4]
            ak_ref[...] = ka_ref[...].T
            av_ref[...] = va_ref[...].T
            bk_ref[...] = kb_ref[...].T
            bv_ref[...] = vb_ref[...].T


def _prompt_caches(kva32s, kvb32s, *, batch, seq, na, nb):
    depth = len(kva32s)

    def rows_map(k, tile, col):
        def index_map(l, b):
            bb = jnp.where(l < k, 0, jnp.where(l > k, batch - 1, b))
            return ((bb + 1) * tile - 1, col)
        return index_map

    in_specs, args = [], []
    for k in range(depth):
        in_specs += [
            pl.BlockSpec((na, A_WIDTH), rows_map(k, seq // na, 0)),
            pl.BlockSpec((na, A_WIDTH), rows_map(k, seq // na, 1)),
            pl.BlockSpec((nb, B_KV_WIDTH), rows_map(k, seq // nb, 0)),
            pl.BlockSpec((nb, B_KV_WIDTH), rows_map(k, seq // nb, 1)),
        ]
        args += [kva32s[k], kva32s[k], kvb32s[k], kvb32s[k]]
    out_shapes = [(depth, batch, A_WIDTH, na)] * 2 + [(depth, batch, B_KV_WIDTH, nb)] * 2
    return pl.pallas_call(
        functools.partial(_prompt_caches_kernel, depth=depth),
        grid=(depth, batch),
        in_specs=in_specs,
        out_specs=[pl.BlockSpec((None, None) + s[2:], lambda l, b: (l, b, 0, 0))
                   for s in out_shapes],
        out_shape=[jax.ShapeDtypeStruct(s, jnp.float32) for s in out_shapes],
        compiler_params=_params(("arbitrary", "arbitrary")),
        name="prompt_caches",
    )(*args)


def _merge_kernel(oa_ref, ob_ref, ga_ref, gb_ref, x_ref, wa_ref, wb_ref, wo_ref,
                  g_ref, y_ref):
    ta = jnp.dot(oa_ref[...], wa_ref[...], preferred_element_type=jnp.float32)
    tb = jnp.dot(ob_ref[...], wb_ref[...], preferred_element_type=jnp.float32)
    mixed = (jax.nn.sigmoid(ga_ref[...].astype(jnp.float32)) * ta
             + jax.nn.sigmoid(gb_ref[...].astype(jnp.float32)) * tb)
    z = jnp.dot(mixed.astype(jnp.bfloat16), wo_ref[...],
                preferred_element_type=jnp.float32)
    y_ref[...] = x_ref[...] + _rms_scale(z, g_ref[...])


def _merge(oa, ob, gates, x, wa, wb, wo, g, l, *, tm):
    m, d = x.shape
    resident = lambda w: pl.BlockSpec((None,) + w.shape[1:], lambda i: (l, 0, 0),
                                      pipeline_mode=pl.Buffered(1))
    return pl.pallas_call(
        _merge_kernel,
        grid=(m // tm,),
        in_specs=[
            pl.BlockSpec((tm, A_WIDTH), lambda i: (i, 0)),
            pl.BlockSpec((tm, B_WIDTH), lambda i: (i, 0)),
            pl.BlockSpec((tm, d), lambda i: (i, 0)),
            pl.BlockSpec((tm, d), lambda i: (i, 1)),
            pl.BlockSpec((tm, d), lambda i: (i, 0)),
            resident(wa), resident(wb), resident(wo),
            _layer_vec_spec(l, d),
        ],
        out_specs=pl.BlockSpec((tm, d), lambda i: (i, 0)),
        out_shape=jax.ShapeDtypeStruct((m, d), jnp.float32),
        compiler_params=_params(("parallel",)),
        name="merge",
    )(oa, ob, gates, gates, x, wa, wb, wo, g)


def _ffn_kernel(x_ref, gpre_ref, wu_ref, wd_ref, gpost_ref, y_ref, h_ref):
    f = pl.program_id(1)

    @pl.when(f == 0)
    def _():
        h_ref[...] = _rms_scale(x_ref[...], gpre_ref[...]).astype(h_ref.dtype)
        y_ref[...] = jnp.zeros_like(y_ref)

    u = jnp.dot(h_ref[...], wu_ref[...], preferred_element_type=jnp.float32)
    u = jnp.square(jnp.maximum(u, 0.0)).astype(jnp.bfloat16)
    y_ref[...] += jnp.dot(u, wd_ref[...], preferred_element_type=jnp.float32)

    @pl.when(f == pl.num_programs(1) - 1)
    def _():
        y_ref[...] = x_ref[...] + _rms_scale(y_ref[...], gpost_ref[...])


def _ffn(x, gpre, wu, wd, gpost, l, *, tm, tf):
    m, d = x.shape
    dff = wu.shape[-1]
    return pl.pallas_call(
        _ffn_kernel,
        grid=(m // tm, dff // tf),
        in_specs=[
            pl.BlockSpec((tm, d), lambda i, f: (i, 0)),
            _layer_vec_spec(l, d),
            pl.BlockSpec((None, d, tf), lambda i, f: (l, 0, f)),
            pl.BlockSpec((None, tf, d), lambda i, f: (l, f, 0)),
            _layer_vec_spec(l, d),
        ],
        out_specs=pl.BlockSpec((tm, d), lambda i, f: (i, 0)),
        out_shape=jax.ShapeDtypeStruct((m, d), jnp.float32),
        scratch_shapes=[pltpu.VMEM((tm, d), jnp.bfloat16)],
        compiler_params=_params(("parallel", "arbitrary")),
        name="ffn",
    )(x, gpre, wu, wd, gpost)


def _t5_bucket(rel):
    half = T5_BUCKETS // 2
    exact = half // 2
    ret = jnp.where(rel > 0, half, 0)
    n = jnp.abs(rel)
    large = exact + (jnp.log(jnp.maximum(n, 1).astype(jnp.float32) / exact)
                     / math.log(T5_MAX_DIST / exact) * (half - exact)).astype(jnp.int32)
    large = jnp.minimum(large, half - 1)
    return ret + jnp.where(n < exact, n, large)


def _a_bias_of_rel(table):
    return lambda rel: table.T[:, jnp.clip(rel, -A_REL_CLIP, A_REL_CLIP) + A_REL_CLIP]


def _b_bias_of_rel(table):
    return lambda rel: table.T[:, _t5_bucket(-rel)]


def _hankel(u, q, n):
    heads, k = u.shape
    period = q + n
    u = jnp.pad(u, ((0, 0), (0, period - k)))
    flat = jnp.tile(u, (1, q + 1))[:, :q * (period + 1)]
    return flat.reshape(heads, q, period + 1)[:, :, :n]


def _rel_bias(bias_of_rel, q_len, n_keys, k0):
    k = jnp.arange(q_len + n_keys - 1)
    u = bias_of_rel(k - (n_keys - 1) - k0)
    return _hankel(u, q_len, n_keys)[:, :, ::-1]


def _band_tile(bias_of_rel, tq, n_prev):
    window = tq + n_prev * CHUNK
    band = _rel_bias(bias_of_rel, CHUNK, (n_prev + 1) * CHUNK, -n_prev * CHUNK)
    blocks = []
    for c in range(tq // CHUNK):
        left = c * CHUNK
        right = window - left - band.shape[-1]
        blocks.append(jnp.pad(band, ((0, 0), (0, 0), (left, right)),
                              constant_values=NEG_INF))
    return jnp.concatenate(blocks, axis=1)


def _sample_bias(bias_of_rel, t_new, n_cache):
    full = _rel_bias(bias_of_rel, t_new, n_cache + t_new, -n_cache)
    full = full.reshape(-1, n_cache + t_new)
    new = jnp.pad(full[:, n_cache:], ((0, 0), (LANES - t_new, 0)),
                  constant_values=NEG_INF)
    return full[:, :n_cache], new


def _cache_t(c):
    d, s, r, h, e = c.shape
    return jnp.transpose(c, (0, 1, 3, 4, 2)).reshape(d, s, h * e, r)


def _cache_untranspose(c, heads):
    d, s, w, r = c.shape
    return jnp.transpose(c.reshape(d, s, heads, w // heads, r), (0, 1, 4, 2, 3))


def kernel(x_prompt, x_sample, cache_a_k, cache_a_v, cache_b_k, cache_b_v, w_in,
           w_a_out, w_b_out, w_out, a_rel_table, t5_table, b_sinks, g_mix_pre,
           g_mix_post, g_ffn_pre, g_ffn_post, w_up, w_down):
    depth = w_in.shape[0]
    batch, seq, d = x_prompt.shape
    dec_batch, t_new, _ = x_sample.shape
    a_len = cache_a_k.shape[2]
    b_len = cache_b_k.shape[2]
    bf16 = jnp.bfloat16
    mp = batch * seq
    ms = dec_batch * t_new

    yp = x_prompt.reshape(mp, d)
    ys = x_sample.reshape(ms, d)

    w_in16 = w_in.astype(bf16)
    wa16 = w_a_out.astype(bf16)
    wb16 = w_b_out.astype(bf16)
    wo16 = w_out.astype(bf16)
    wu16 = w_up.astype(bf16)
    wd16 = w_down.astype(bf16)
    g_pre = g_mix_pre[:, None, :]
    g_post = g_mix_post[:, None, :]
    gf_pre = g_ffn_pre[:, None, :]
    gf_post = g_ffn_post[:, None, :]
    caches = tuple(_cache_t(c) for c in (cache_a_k, cache_a_v, cache_b_k, cache_b_v))

    b_of_rel = _b_bias_of_rel(t5_table)
    bias_b_tile = _band_tile(b_of_rel, B_TQ, B_LEFT_CHUNKS)
    bias_bc, bias_bn = _sample_bias(b_of_rel, t_new, b_len)

    prompt_kva, prompt_kvb, sample_kva, sample_kvb = [], [], [], []
    for l in range(depth):
        a_of_rel = _a_bias_of_rel(a_rel_table[l])

        qkv, kvb16, gates, kva32, kvb32 = _in_proj(yp, g_pre, w_in16, l, tm=1024)
        oa = _attn_a(qkv, _band_tile(a_of_rel, A_TQ, A_LEFT_CHUNKS), m=mp, seq=seq)
        ob = _attn_b(qkv, kvb16, bias_b_tile, b_sinks[l], m=mp, seq=seq)
        yp = _merge(oa, ob, gates, yp, wa16, wb16, wo16, g_post, l, tm=256)
        yp = _ffn(yp, gf_pre, wu16, wd16, gf_post, l, tm=512, tf=1024)
        prompt_kva.append(kva32)
        prompt_kvb.append(kvb32)

        qkv, _, gates, kva32, kvb32 = _in_proj(ys, g_pre, w_in16, l, tm=ms)
        bias_ac, bias_an = _sample_bias(a_of_rel, t_new, a_len)
        sink_col = jnp.repeat(b_sinks[l], t_new)[:, None]
        oa, ob = _attn_sample(qkv, kva32, kvb32, caches, l, bias_ac, bias_an, bias_bc,
                              bias_bn, sink_col, t_new=t_new)
        ys = _merge(oa, ob, gates, ys, wa16, wb16, wo16, g_post, l, tm=ms)
        ys = _ffn(ys, gf_pre, wu16, wd16, gf_post, l, tm=ms, tf=1024)
        sample_kva.append(kva32)
        sample_kvb.append(kvb32)

    akp, avp, bkp, bvp = _prompt_caches(prompt_kva, prompt_kvb, batch=batch, seq=seq,
                                        na=min(a_len, seq), nb=min(b_len, seq))
    aks, avs, bks, bvs = _roll_caches(jnp.stack(sample_kva), jnp.stack(sample_kvb),
                                      caches, t_new=t_new)
    return (yp.reshape(batch, seq, d), ys.reshape(dec_batch, t_new, d),
            _cache_untranspose(akp, A_HEADS), _cache_untranspose(avp, A_HEADS),
            _cache_untranspose(bkp, B_KV_HEADS), _cache_untranspose(bvp, B_KV_HEADS),
            _cache_untranspose(aks, A_HEADS), _cache_untranspose(avs, A_HEADS),
            _cache_untranspose(bks, B_KV_HEADS), _cache_untranspose(bvs, B_KV_HEADS))
```

```python
import functools
import math

import jax
import jax.numpy as jnp
from jax import lax
from jax.experimental import pallas as pl
from jax.experimental.pallas import tpu as pltpu

D_MODEL = 2048
CHUNK = 64
HEAD_DIM = 64
A_HEADS = 16
A_WIDTH = A_HEADS * HEAD_DIM
A_LEFT_CHUNKS = 8
A_REL_CLIP = 256
B_HEADS = 16
B_KV_HEADS = 2
B_GROUP = B_HEADS // B_KV_HEADS
B_WIDTH = B_HEADS * HEAD_DIM
B_KV_WIDTH = B_KV_HEADS * HEAD_DIM
B_LEFT_CHUNKS = 2
T5_BUCKETS = 32
T5_MAX_DIST = 128
EPS = 1e-6
NEG_INF = -1e30
SCALE = HEAD_DIM ** -0.5

LANES = 128
MXU_COLS = 256

COL_QA = 0
COL_KA = A_WIDTH
COL_VA = 2 * A_WIDTH
COL_QB = 3 * A_WIDTH
COL_KB = 3 * A_WIDTH + B_WIDTH
COL_GA = COL_KB + 2 * B_KV_WIDTH
PROJ_TN = 4 * MXU_COLS
QKV_KA, QKV_VA, QKV_QA, QKV_QB = 0, 1, 2, 3

A_TQ = 256
B_TQ = 128
A_KBLOCKS = A_LEFT_CHUNKS * CHUNK // A_TQ + 1
B_KBLOCKS = B_LEFT_CHUNKS * CHUNK // B_TQ + 1

VMEM_LIMIT = 56 * 1024 * 1024


def _params(sem, vmem=VMEM_LIMIT):
    return pltpu.CompilerParams(dimension_semantics=sem, vmem_limit_bytes=vmem)


def _rms_scale(x, g):
    return x * lax.rsqrt(jnp.mean(x * x, axis=-1, keepdims=True) + EPS) * g


def _layer_vec_spec(l, d):
    return pl.BlockSpec((None, 1, d), lambda *_: (l, 0, 0))


PROJ_KV_STEPS = 2
PROJ_Q_STEPS = 4
PROJ_KVB_STEP = 4
PROJ_GATE_STEP0 = 5


def _in_proj_kernel(x_ref, g_ref, *refs):
    n_w = PROJ_TN // MXU_COLS
    w_refs = refs[:n_w]
    qkv_ref, kvb16_ref, gates_ref, kva32_ref, kvb32_ref, h_ref = refs[n_w:]
    j = pl.program_id(1)

    @pl.when(j == 0)
    def _():
        h_ref[...] = _rms_scale(x_ref[...], g_ref[...]).astype(h_ref.dtype)

    def tile(out16_ref, out32_ref):
        h = h_ref[...]
        for k, w_ref in enumerate(w_refs):
            cols = slice(k * MXU_COLS, (k + 1) * MXU_COLS)
            acc = jnp.dot(h, w_ref[...], preferred_element_type=jnp.float32)
            out16_ref[:, cols] = acc.astype(out16_ref.dtype)
            if out32_ref is not None:
                out32_ref[:, cols] = acc

    @pl.when(j < PROJ_KV_STEPS)
    def _():
        tile(qkv_ref, kva32_ref)

    @pl.when((j >= PROJ_KV_STEPS) & (j < PROJ_Q_STEPS))
    def _():
        tile(qkv_ref, None)

    @pl.when(j == PROJ_KVB_STEP)
    def _():
        acc = jnp.dot(h_ref[...], w_refs[0][...], preferred_element_type=jnp.float32)
        kvb16_ref[...] = acc.astype(kvb16_ref.dtype)
        kvb32_ref[...] = acc

    @pl.when(j >= PROJ_GATE_STEP0)
    def _():
        tile(gates_ref, None)


def _in_proj(x, g, w_in, l, *, tm):
    m, d = x.shape
    n_w = PROJ_TN // MXU_COLS
    n_gate = 2 * D_MODEL // PROJ_TN
    src = [c // MXU_COLS for c in (COL_KA, COL_VA, COL_QA, COL_QB, COL_KB)]
    ga0 = COL_GA // MXU_COLS

    def w_map(k):
        def index_map(i, j):
            base = src[PROJ_KVB_STEP] if k == 0 else src[PROJ_KVB_STEP - 1] + k
            for step in range(PROJ_KVB_STEP - 1, -1, -1):
                base = jnp.where(j == step, src[step] + k, base)
            gate = ga0 + (j - PROJ_GATE_STEP0) * n_w + k
            return (0, jnp.where(j >= PROJ_GATE_STEP0, gate, base))
        return index_map

    kvb_n = 2 * B_KV_WIDTH
    return pl.pallas_call(
        _in_proj_kernel,
        grid=(m // tm, PROJ_GATE_STEP0 + n_gate),
        in_specs=[
            pl.BlockSpec((tm, d), lambda i, j: (i, 0)),
            _layer_vec_spec(l, d),
        ] + [pl.BlockSpec((d, MXU_COLS), w_map(k)) for k in range(n_w)],
        out_specs=[
            pl.BlockSpec((tm, PROJ_TN), lambda i, j: (i, jnp.minimum(j, PROJ_Q_STEPS - 1))),
            pl.BlockSpec((tm, kvb_n), lambda i, j: (i, 0)),
            pl.BlockSpec((tm, PROJ_TN),
                         lambda i, j: (i, jnp.clip(j - PROJ_GATE_STEP0, 0, n_gate - 1))),
            pl.BlockSpec((tm, PROJ_TN), lambda i, j: (i, jnp.minimum(j, PROJ_KV_STEPS - 1))),
            pl.BlockSpec((tm, kvb_n), lambda i, j: (i, 0)),
        ],
        out_shape=[
            jax.ShapeDtypeStruct((m, PROJ_Q_STEPS * PROJ_TN), jnp.bfloat16),
            jax.ShapeDtypeStruct((m, kvb_n), jnp.bfloat16),
            jax.ShapeDtypeStruct((m, 2 * D_MODEL), jnp.bfloat16),
            jax.ShapeDtypeStruct((m, PROJ_KV_STEPS * PROJ_TN), jnp.float32),
            jax.ShapeDtypeStruct((m, kvb_n), jnp.float32),
        ],
        scratch_shapes=[pltpu.VMEM((tm, d), jnp.bfloat16)],
        compiler_params=_params(("parallel", "arbitrary")),
        name="in_proj",
    )(x, g, *([w_in] * n_w))


def _low_half():
    return lax.broadcasted_iota(jnp.int32, (1, LANES), 1) < HEAD_DIM


def _start_mask(tile_in_seq, n_kblocks, tq):
    lane = lax.broadcasted_iota(jnp.int32, (1, n_kblocks * tq), 1)
    first_valid = jnp.maximum(n_kblocks - 1 - tile_in_seq, 0) * tq
    return jnp.where(lane >= first_valid, 0.0, NEG_INF)


def _attn_a_kernel(q_ref, *refs, tiles_per_seq):
    k_refs = refs[:A_KBLOCKS]
    v_refs = refs[A_KBLOCKS:2 * A_KBLOCKS]
    bias_ref = refs[2 * A_KBLOCKS]
    n_cast = (len(refs) - 2 * A_KBLOCKS - 2) // 2
    w32_refs = refs[2 * A_KBLOCKS + 1:2 * A_KBLOCKS + 1 + n_cast]
    o_ref = refs[2 * A_KBLOCKS + 1 + n_cast]
    w16_refs = refs[2 * A_KBLOCKS + 2 + n_cast:]
    tile_in_seq = pl.program_id(0) % tiles_per_seq
    low = _low_half()

    def heads(start):
        for w32_ref, w16_ref in zip(w32_refs, w16_refs):
            w16_ref[...] = w32_ref[...].astype(w16_ref.dtype)
        for p in range(A_HEADS // 2):
            sl = slice(p * LANES, (p + 1) * LANES)
            q2 = q_ref[:, sl]
            k2 = jnp.concatenate([r[:, sl] for r in k_refs], axis=0)
            v2 = jnp.concatenate([r[:, sl] for r in v_refs], axis=0)
            qs = jnp.concatenate([jnp.where(low, q2, 0), jnp.where(low, 0, q2)],
                                 axis=0) * SCALE
            s_all = lax.dot_general(qs, k2, (((1,), (1,)), ((), ())),
                                    preferred_element_type=jnp.float32)
            probs, denoms = [], []
            for hh in range(2):
                s = s_all[hh * A_TQ:(hh + 1) * A_TQ] + bias_ref[2 * p + hh]
                if start is not None:
                    s = s + start
                e = jnp.exp(s - jnp.max(s, axis=-1, keepdims=True))
                denoms.append(jnp.sum(e, axis=-1, keepdims=True))
                probs.append(e.astype(v2.dtype))
            r = jnp.dot(jnp.concatenate(probs, axis=0), v2,
                        preferred_element_type=jnp.float32)
            o_ref[:, sl] = jnp.where(low, r[:A_TQ] / denoms[0],
                                     r[A_TQ:] / denoms[1]).astype(o_ref.dtype)

    @pl.when(tile_in_seq >= A_KBLOCKS - 1)
    def _():
        heads(None)

    @pl.when(tile_in_seq < A_KBLOCKS - 1)
    def _():
        heads(_start_mask(tile_in_seq, A_KBLOCKS, A_TQ))


def _band_kv_map(j, col, n_kblocks, tiles_per_seq):
    def index_map(t, *_):
        i = t % tiles_per_seq
        return (t - i + jnp.maximum(i - (n_kblocks - 1) + j, 0), col)
    return index_map


def _attn_a(qkv, bias, weights, l, *, m, seq):
    tiles_per_seq = seq // A_TQ
    steps = m // A_TQ
    kv_spec = lambda j, col: pl.BlockSpec(
        (A_TQ, A_WIDTH), _band_kv_map(j, col, A_KBLOCKS, tiles_per_seq))
    slab32 = lambda w: pl.BlockSpec((None, w.shape[1] // steps, w.shape[2]),
                                    lambda t: (l, t, 0))
    slab16 = lambda w: pl.BlockSpec((w.shape[1] // steps, w.shape[2]), lambda t: (t, 0))
    return pl.pallas_call(
        functools.partial(_attn_a_kernel, tiles_per_seq=tiles_per_seq),
        grid=(steps,),
        in_specs=[pl.BlockSpec((A_TQ, A_WIDTH), lambda t: (t, QKV_QA))]
        + [kv_spec(j, QKV_KA) for j in range(A_KBLOCKS)]
        + [kv_spec(j, QKV_VA) for j in range(A_KBLOCKS)]
        + [pl.BlockSpec((A_HEADS, A_TQ, A_KBLOCKS * A_TQ), lambda t: (0, 0, 0),
                        pipeline_mode=pl.Buffered(1))]
        + [slab32(w) for w in weights],
        out_specs=[pl.BlockSpec((A_TQ, A_WIDTH), lambda t: (t, 0))]
        + [slab16(w) for w in weights],
        out_shape=[jax.ShapeDtypeStruct((m, A_WIDTH), jnp.bfloat16)]
        + [jax.ShapeDtypeStruct(w.shape[1:], jnp.bfloat16) for w in weights],
        compiler_params=_params(("parallel",)),
        name="attn_a",
    )(qkv, *([qkv] * (2 * A_KBLOCKS)), bias, *weights)


def _dup_half(x, g):
    swapped = pltpu.roll(x, HEAD_DIM, 1)
    low = _low_half()
    return jnp.where(low, x, swapped) if g == 0 else jnp.where(low, swapped, x)


def _attn_b_kernel(sink_ref, q_ref, *refs, tiles_per_seq):
    k_refs = refs[:B_KBLOCKS]
    v_refs = refs[B_KBLOCKS:2 * B_KBLOCKS]
    bias_ref, o_ref = refs[2 * B_KBLOCKS:]
    start = _start_mask(pl.program_id(0) % tiles_per_seq, B_KBLOCKS, B_TQ)
    low = _low_half()
    k2 = jnp.concatenate([r[...] for r in k_refs], axis=0).astype(jnp.float32)
    v2 = jnp.concatenate([r[...] for r in v_refs], axis=0).astype(jnp.float32)
    pairs = B_GROUP // 2
    for g in range(B_KV_HEADS):
        kd = _dup_half(k2, g).astype(jnp.bfloat16)
        vd = _dup_half(v2, g).astype(jnp.bfloat16)
        stacked = []
        for p in range(pairs):
            c0 = (g * pairs + p) * LANES
            q2 = q_ref[:, c0:c0 + LANES]
            stacked.append(jnp.where(low, q2, 0) * SCALE)
            stacked.append(jnp.where(low, 0, q2) * SCALE)
        qs = jnp.concatenate(stacked, axis=0)
        s_all = lax.dot_general(qs, kd, (((1,), (1,)), ((), ())),
                                preferred_element_type=jnp.float32)
        probs, denoms = [], []
        for hl in range(B_GROUP):
            h = g * B_GROUP + hl
            s = s_all[hl * B_TQ:(hl + 1) * B_TQ] + bias_ref[h] + start
            sink = sink_ref[h]
            mx = jnp.maximum(jnp.max(s, axis=-1, keepdims=True), sink)
            e = jnp.exp(s - mx)
            denoms.append(jnp.sum(e, axis=-1, keepdims=True) + jnp.exp(sink - mx))
            probs.append(e.astype(jnp.bfloat16))
        r = jnp.dot(jnp.concatenate(probs, axis=0), vd,
                    preferred_element_type=jnp.float32)
        for p in range(pairs):
            c0 = (g * pairs + p) * LANES
            r0 = r[(2 * p) * B_TQ:(2 * p + 1) * B_TQ] / denoms[2 * p]
            r1 = r[(2 * p + 1) * B_TQ:(2 * p + 2) * B_TQ] / denoms[2 * p + 1]
            o_ref[:, c0:c0 + LANES] = jnp.where(low, r0, r1).astype(o_ref.dtype)


def _attn_b(qkv, kvb16, bias, sinks, *, m, seq):
    tiles_per_seq = seq // B_TQ
    kv_spec = lambda j, col: pl.BlockSpec(
        (B_TQ, LANES), _band_kv_map(j, col, B_KBLOCKS, tiles_per_seq))
    return pl.pallas_call(
        functools.partial(_attn_b_kernel, tiles_per_seq=tiles_per_seq),
        grid_spec=pltpu.PrefetchScalarGridSpec(
            num_scalar_prefetch=1,
            grid=(m // B_TQ,),
            in_specs=[pl.BlockSpec((B_TQ, B_WIDTH), lambda t, s: (t, QKV_QB))]
            + [kv_spec(j, 0) for j in range(B_KBLOCKS)]
            + [kv_spec(j, 1) for j in range(B_KBLOCKS)]
            + [pl.BlockSpec((B_HEADS, B_TQ, B_KBLOCKS * B_TQ), lambda t, s: (0, 0, 0))],
            out_specs=pl.BlockSpec((B_TQ, B_WIDTH), lambda t, s: (t, 0)),
        ),
        out_shape=jax.ShapeDtypeStruct((m, B_WIDTH), jnp.bfloat16),
        compiler_params=_params(("parallel",)),
        name="attn_b",
    )(sinks, qkv, *([kvb16] * (2 * B_KBLOCKS)), bias)


def _pad_top(x, t_new):
    return jnp.concatenate([jnp.zeros((LANES - t_new, x.shape[1]), x.dtype), x], axis=0)


def _attn_sample_kernel(qa_ref, qb_ref, kan_ref, van_ref, kbn_ref, vbn_ref,
                        cak_ref, cav_ref, cbk_ref, cbv_ref,
                        bias_ac_ref, bias_an_ref, bias_bc_ref, bias_bn_ref, sink_ref,
                        oa_ref, ob_ref, *, t_new):
    heads = A_HEADS
    rows = heads * t_new
    width = heads * HEAD_DIM
    bf16 = jnp.bfloat16
    row_head = lax.broadcasted_iota(jnp.int32, (rows, width), 0) // t_new
    lane_head = lax.broadcasted_iota(jnp.int32, (rows, width), 1) // HEAD_DIM
    own = row_head == lane_head

    def stack_q(q):
        return jnp.where(own, jnp.concatenate([q] * heads, axis=0), 0) * SCALE

    def nt_dot(a, b):
        return lax.dot_general(a, b, (((1,), (1,)), ((), ())),
                               preferred_element_type=jnp.float32)

    def finish(s_c, s_n, vt_c, v_n, sink):
        mx = jnp.maximum(jnp.max(s_c, axis=-1, keepdims=True),
                         jnp.max(s_n, axis=-1, keepdims=True))
        if sink is not None:
            mx = jnp.maximum(mx, sink)
        e_c = jnp.exp(s_c - mx)
        e_n = jnp.exp(s_n - mx)
        denom = jnp.sum(e_c, axis=-1, keepdims=True) + jnp.sum(e_n, axis=-1, keepdims=True)
        if sink is not None:
            denom = denom + jnp.exp(sink - mx)
        o = nt_dot(e_c.astype(bf16), vt_c)
        o = o + jnp.dot(e_n.astype(bf16), v_n, preferred_element_type=jnp.float32)
        o = jnp.where(own, o / denom, 0.0)
        return jnp.sum(o.reshape(heads, t_new, width), axis=0)

    k_n = _pad_top(kan_ref[...], t_new).astype(bf16)
    v_n = _pad_top(van_ref[...], t_new).astype(bf16)
    qs = stack_q(qa_ref[...])
    s_c = jnp.dot(qs, cak_ref[...].astype(bf16),
                  preferred_element_type=jnp.float32) + bias_ac_ref[...]
    s_n = nt_dot(qs, k_n) + bias_an_ref[...]
    oa_ref[...] = finish(s_c, s_n, cav_ref[...].astype(bf16), v_n, None).astype(oa_ref.dtype)

    def expansion(shape, src_axis):
        src = lax.broadcasted_iota(jnp.int32, shape, src_axis)
        dst = lax.broadcasted_iota(jnp.int32, shape, 1 - src_axis)
        return ((src % HEAD_DIM == dst % HEAD_DIM)
                & (src // HEAD_DIM == dst // (B_GROUP * HEAD_DIM))).astype(bf16)

    def widen(x):
        return jnp.dot(x.astype(bf16), expansion((B_KV_WIDTH, width), 0),
                       preferred_element_type=jnp.float32).astype(bf16)

    def widen_t(xt):
        return jnp.dot(expansion((width, B_KV_WIDTH), 1), xt.astype(bf16),
                       preferred_element_type=jnp.float32).astype(bf16)

    k_n = widen(_pad_top(kbn_ref[...], t_new))
    v_n = widen(_pad_top(vbn_ref[...], t_new))
    qs = stack_q(qb_ref[...])
    s_c = jnp.dot(qs, widen_t(cbk_ref[...]),
                  preferred_element_type=jnp.float32) + bias_bc_ref[...]
    s_n = nt_dot(qs, k_n) + bias_bn_ref[...]
    ob_ref[...] = finish(s_c, s_n, widen_t(cbv_ref[...]), v_n, sink_ref[...]).astype(ob_ref.dtype)


def _attn_sample(qkv, kva32, kvb32, caches, l, bias_ac, bias_an, bias_bc, bias_bn,
                 sink_col, *, t_new):
    m = qkv.shape[0]
    cak, cav, cbk, cbv = caches
    a_len, b_len = cak.shape[-1], cbk.shape[-1]
    rows = A_HEADS * t_new
    const = lambda shape: pl.BlockSpec(shape, lambda b: (0,) * len(shape))
    cache_spec = lambda c: pl.BlockSpec((None, None) + c.shape[2:], lambda b: (l, b, 0, 0))
    return pl.pallas_call(
        functools.partial(_attn_sample_kernel, t_new=t_new),
        grid=(m // t_new,),
        in_specs=[
            pl.BlockSpec((t_new, A_WIDTH), lambda b: (b, QKV_QA)),
            pl.BlockSpec((t_new, B_WIDTH), lambda b: (b, QKV_QB)),
            pl.BlockSpec((t_new, A_WIDTH), lambda b: (b, 0)),
            pl.BlockSpec((t_new, A_WIDTH), lambda b: (b, 1)),
            pl.BlockSpec((t_new, B_KV_WIDTH), lambda b: (b, 0)),
            pl.BlockSpec((t_new, B_KV_WIDTH), lambda b: (b, 1)),
            cache_spec(cak), cache_spec(cav), cache_spec(cbk), cache_spec(cbv),
            const((rows, a_len)), const((rows, LANES)),
            const((rows, b_len)), const((rows, LANES)),
            const((rows, 1)),
        ],
        out_specs=[
            pl.BlockSpec((t_new, A_WIDTH), lambda b: (b, 0)),
            pl.BlockSpec((t_new, B_WIDTH), lambda b: (b, 0)),
        ],
        out_shape=[
            jax.ShapeDtypeStruct((m, A_WIDTH), jnp.bfloat16),
            jax.ShapeDtypeStruct((m, B_WIDTH), jnp.bfloat16),
        ],
        compiler_params=_params(("parallel",)),
        name="attn_sample",
    )(qkv, qkv, kva32, kva32, kvb32, kvb32, cak, cav, cbk, cbv,
      bias_ac, bias_an, bias_bc, bias_bn, sink_col)


def _roll_caches_kernel(kan_ref, van_ref, kbn_ref, vbn_ref,
                        cak_ref, cav_ref, cbk_ref, cbv_ref,
                        nak_ref, nav_ref, nbk_ref, nbv_ref, *, t_new):
    def roll_in(cache_ref, new_ref, out_ref):
        cache_t = cache_ref[...]
        n = cache_t.shape[1]
        shifted = pltpu.roll(cache_t, n - t_new, 1)
        lane = lax.broadcasted_iota(jnp.int32, (1, LANES), 1)
        tail = jnp.where(lane < LANES - t_new, shifted[:, n - LANES:],
                         _pad_top(new_ref[...], t_new).T)
        if n > LANES:
            out_ref[:, :n - LANES] = shifted[:, :n - LANES]
        out_ref[:, n - LANES:] = tail

    roll_in(cak_ref, kan_ref, nak_ref)
    roll_in(cav_ref, van_ref, nav_ref)
    roll_in(cbk_ref, kbn_ref, nbk_ref)
    roll_in(cbv_ref, vbn_ref, nbv_ref)


def _roll_caches(new_a, new_b, caches, *, t_new):
    depth, streams = caches[0].shape[:2]
    cache_spec = lambda c: pl.BlockSpec((None, None) + c.shape[2:], lambda l, b: (l, b, 0, 0))
    new_spec = lambda width, col: pl.BlockSpec((None, t_new, width), lambda l, b: (l, b, col))
    return pl.pallas_call(
        functools.partial(_roll_caches_kernel, t_new=t_new),
        grid=(depth, streams),
        in_specs=[new_spec(A_WIDTH, 0), new_spec(A_WIDTH, 1),
                  new_spec(B_KV_WIDTH, 0), new_spec(B_KV_WIDTH, 1)]
        + [cache_spec(c) for c in caches],
        out_specs=[cache_spec(c) for c in caches],
        out_shape=[jax.ShapeDtypeStruct(c.shape, jnp.float32) for c in caches],
        compiler_params=_params(("parallel", "parallel")),
        name="roll_caches",
    )(new_a, new_a, new_b, new_b, *caches)


def _prompt_caches_kernel(*refs, depth):
    in_refs, (ak_ref, av_ref, bk_ref, bv_ref) = refs[:4 * depth], refs[4 * depth:]
    for k in range(depth):
        @pl.when(pl.program_id(0) == k)
        def _(k=k):
            ka_ref, va_ref, kb_ref, vb_ref = in_refs[4 * k:4 * k + 4]
            ak_ref[...] = ka_ref[...].T
            av_ref[...] = va_ref[...].T
            bk_ref[...] = kb_ref[...].T
            bv_ref[...] = vb_ref[...].T


def _prompt_caches(kva32s, kvb32s, *, batch, seq, na, nb):
    depth = len(kva32s)

    def rows_map(k, tile, col):
        def index_map(l, b):
            bb = jnp.where(l < k, 0, jnp.where(l > k, batch - 1, b))
            return ((bb + 1) * tile - 1, col)
        return index_map

    in_specs, args = [], []
    for k in range(depth):
        in_specs += [
            pl.BlockSpec((na, A_WIDTH), rows_map(k, seq // na, 0)),
            pl.BlockSpec((na, A_WIDTH), rows_map(k, seq // na, 1)),
            pl.BlockSpec((nb, B_KV_WIDTH), rows_map(k, seq // nb, 0)),
            pl.BlockSpec((nb, B_KV_WIDTH), rows_map(k, seq // nb, 1)),
        ]
        args += [kva32s[k], kva32s[k], kvb32s[k], kvb32s[k]]
    out_shapes = [(depth, batch, A_WIDTH, na)] * 2 + [(depth, batch, B_KV_WIDTH, nb)] * 2
    return pl.pallas_call(
        functools.partial(_prompt_caches_kernel, depth=depth),
        grid=(depth, batch),
        in_specs=in_specs,
        out_specs=[pl.BlockSpec((None, None) + s[2:], lambda l, b: (l, b, 0, 0))
                   for s in out_shapes],
        out_shape=[jax.ShapeDtypeStruct(s, jnp.float32) for s in out_shapes],
        compiler_params=_params(("arbitrary", "arbitrary")),
        name="prompt_caches",
    )(*args)


def _merge_kernel(oa_ref, ob_ref, ga_ref, gb_ref, x_ref, wa_ref, wb_ref, wo_ref,
                  g_ref, y_ref):
    ta = jnp.dot(oa_ref[...], wa_ref[...], preferred_element_type=jnp.float32)
    tb = jnp.dot(ob_ref[...], wb_ref[...], preferred_element_type=jnp.float32)
    mixed = (jax.nn.sigmoid(ga_ref[...].astype(jnp.float32)) * ta
             + jax.nn.sigmoid(gb_ref[...].astype(jnp.float32)) * tb)
    z = jnp.dot(mixed.astype(jnp.bfloat16), wo_ref[...],
                preferred_element_type=jnp.float32)
    y_ref[...] = x_ref[...] + _rms_scale(z, g_ref[...])


def _merge(oa, ob, gates, x, wa, wb, wo, g, l, *, tm):
    m, d = x.shape
    resident = lambda w: pl.BlockSpec(w.shape, lambda i: (0, 0),
                                      pipeline_mode=pl.Buffered(1))
    return pl.pallas_call(
        _merge_kernel,
        grid=(m // tm,),
        in_specs=[
            pl.BlockSpec((tm, A_WIDTH), lambda i: (i, 0)),
            pl.BlockSpec((tm, B_WIDTH), lambda i: (i, 0)),
            pl.BlockSpec((tm, d), lambda i: (i, 0)),
            pl.BlockSpec((tm, d), lambda i: (i, 1)),
            pl.BlockSpec((tm, d), lambda i: (i, 0)),
            resident(wa), resident(wb), resident(wo),
            _layer_vec_spec(l, d),
        ],
        out_specs=pl.BlockSpec((tm, d), lambda i: (i, 0)),
        out_shape=jax.ShapeDtypeStruct((m, d), jnp.float32),
        compiler_params=_params(("parallel",)),
        name="merge",
    )(oa, ob, gates, gates, x, wa, wb, wo, g)


def _ffn_kernel(x_ref, gpre_ref, wu_ref, wd_ref, gpost_ref, *refs):
    f = pl.program_id(1)
    if len(refs) == 4:
        w32_ref, y_ref, w16_ref, h_ref = refs
        w16_ref[...] = w32_ref[...].astype(w16_ref.dtype)
    else:
        y_ref, h_ref = refs

    @pl.when(f == 0)
    def _():
        h_ref[...] = _rms_scale(x_ref[...], gpre_ref[...]).astype(h_ref.dtype)
        y_ref[...] = jnp.zeros_like(y_ref)

    u = jnp.dot(h_ref[...], wu_ref[...], preferred_element_type=jnp.float32)
    u = jnp.square(jnp.maximum(u, 0.0)).astype(jnp.bfloat16)
    y_ref[...] += jnp.dot(u, wd_ref[...], preferred_element_type=jnp.float32)

    @pl.when(f == pl.num_programs(1) - 1)
    def _():
        y_ref[...] = x_ref[...] + _rms_scale(y_ref[...], gpost_ref[...])


def _ffn(x, gpre, wu, wd, gpost, l, cast=None, *, tm, tf):
    m, d = x.shape
    dff = wu.shape[-1]
    n_f = dff // tf
    in_specs = [
        pl.BlockSpec((tm, d), lambda i, f: (i, 0)),
        _layer_vec_spec(l, d),
        pl.BlockSpec((d, tf), lambda i, f: (0, f)),
        pl.BlockSpec((tf, d), lambda i, f: (f, 0)),
        _layer_vec_spec(l, d),
    ]
    out_specs = [pl.BlockSpec((tm, d), lambda i, f: (i, 0))]
    out_shape = [jax.ShapeDtypeStruct((m, d), jnp.float32)]
    args = [x, gpre, wu, wd, gpost]
    if cast is not None:
        w, lc = cast
        rows = w.shape[1] // ((m // tm) * n_f)
        in_specs.append(pl.BlockSpec((None, rows, w.shape[2]),
                                     lambda i, f: (lc, i * n_f + f, 0)))
        out_specs.append(pl.BlockSpec((rows, w.shape[2]), lambda i, f: (i * n_f + f, 0)))
        out_shape.append(jax.ShapeDtypeStruct(w.shape[1:], jnp.bfloat16))
        args.append(w)
    return pl.pallas_call(
        _ffn_kernel,
        grid=(m // tm, n_f),
        in_specs=in_specs,
        out_specs=out_specs,
        out_shape=out_shape,
        scratch_shapes=[pltpu.VMEM((tm, d), jnp.bfloat16)],
        compiler_params=_params(("arbitrary", "arbitrary")),
        name="ffn",
    )(*args)


def _t5_bucket(rel):
    half = T5_BUCKETS // 2
    exact = half // 2
    ret = jnp.where(rel > 0, half, 0)
    n = jnp.abs(rel)
    large = exact + (jnp.log(jnp.maximum(n, 1).astype(jnp.float32) / exact)
                     / math.log(T5_MAX_DIST / exact) * (half - exact)).astype(jnp.int32)
    large = jnp.minimum(large, half - 1)
    return ret + jnp.where(n < exact, n, large)


def _a_bias_of_rel(table):
    return lambda rel: table.T[:, jnp.clip(rel, -A_REL_CLIP, A_REL_CLIP) + A_REL_CLIP]


def _b_bias_of_rel(table):
    return lambda rel: table.T[:, _t5_bucket(-rel)]


def _hankel(u, q, n):
    heads, k = u.shape
    period = q + n
    u = jnp.pad(u, ((0, 0), (0, period - k)))
    flat = jnp.tile(u, (1, q + 1))[:, :q * (period + 1)]
    return flat.reshape(heads, q, period + 1)[:, :, :n]


def _rel_bias(bias_of_rel, q_len, n_keys, k0):
    k = jnp.arange(q_len + n_keys - 1)
    u = bias_of_rel(k - (n_keys - 1) - k0)
    return _hankel(u, q_len, n_keys)[:, :, ::-1]


def _band_tile(bias_of_rel, tq, n_prev):
    window = tq + n_prev * CHUNK
    band = _rel_bias(bias_of_rel, CHUNK, (n_prev + 1) * CHUNK, -n_prev * CHUNK)
    blocks = []
    for c in range(tq // CHUNK):
        left = c * CHUNK
        right = window - left - band.shape[-1]
        blocks.append(jnp.pad(band, ((0, 0), (0, 0), (left, right)),
                              constant_values=NEG_INF))
    return jnp.concatenate(blocks, axis=1)


def _sample_bias(bias_of_rel, t_new, n_cache):
    full = _rel_bias(bias_of_rel, t_new, n_cache + t_new, -n_cache)
    full = full.reshape(-1, n_cache + t_new)
    new = jnp.pad(full[:, n_cache:], ((0, 0), (LANES - t_new, 0)),
                  constant_values=NEG_INF)
    return full[:, :n_cache], new


def _cache_t(c):
    d, s, r, h, e = c.shape
    return jnp.transpose(c, (0, 1, 3, 4, 2)).reshape(d, s, h * e, r)


def _cache_untranspose(c, heads):
    d, s, w, r = c.shape
    return jnp.transpose(c.reshape(d, s, heads, w // heads, r), (0, 1, 4, 2, 3))


def kernel(x_prompt, x_sample, cache_a_k, cache_a_v, cache_b_k, cache_b_v, w_in,
           w_a_out, w_b_out, w_out, a_rel_table, t5_table, b_sinks, g_mix_pre,
           g_mix_post, g_ffn_pre, g_ffn_post, w_up, w_down):
    depth = w_in.shape[0]
    batch, seq, d = x_prompt.shape
    dec_batch, t_new, _ = x_sample.shape
    a_len = cache_a_k.shape[2]
    b_len = cache_b_k.shape[2]
    bf16 = jnp.bfloat16
    mp = batch * seq
    ms = dec_batch * t_new

    yp = x_prompt.reshape(mp, d)
    ys = x_sample.reshape(ms, d)

    w_in16 = w_in[0].astype(bf16)
    later_weights = (w_a_out, w_b_out, w_out, w_up, w_down)
    g_pre = g_mix_pre[:, None, :]
    g_post = g_mix_post[:, None, :]
    gf_pre = g_ffn_pre[:, None, :]
    gf_post = g_ffn_post[:, None, :]
    caches = tuple(_cache_t(c) for c in (cache_a_k, cache_a_v, cache_b_k, cache_b_v))

    b_of_rel = _b_bias_of_rel(t5_table)
    bias_b_tile = _band_tile(b_of_rel, B_TQ, B_LEFT_CHUNKS)
    bias_bc, bias_bn = _sample_bias(b_of_rel, t_new, b_len)

    prompt_kva, prompt_kvb, sample_kva, sample_kvb = [], [], [], []
    for l in range(depth):
        a_of_rel = _a_bias_of_rel(a_rel_table[l])

        qkv, kvb16, gates, kva32, kvb32 = _in_proj(yp, g_pre, w_in16, l, tm=1024)
        oa, wa16, wb16, wo16, wu16, wd16 = _attn_a(
            qkv, _band_tile(a_of_rel, A_TQ, A_LEFT_CHUNKS), later_weights, l, m=mp, seq=seq)
        ob = _attn_b(qkv, kvb16, bias_b_tile, b_sinks[l], m=mp, seq=seq)
        yp = _merge(oa, ob, gates, yp, wa16, wb16, wo16, g_post, l, tm=256)
        w_in16_this = w_in16
        if l + 1 < depth:
            yp, w_in16 = _ffn(yp, gf_pre, wu16, wd16, gf_post, l, (w_in, l + 1),
                              tm=512, tf=1024)
        else:
            yp, = _ffn(yp, gf_pre, wu16, wd16, gf_post, l, tm=512, tf=1024)
        prompt_kva.append(kva32)
        prompt_kvb.append(kvb32)

        qkv, _, gates, kva32, kvb32 = _in_proj(ys, g_pre, w_in16_this, l, tm=ms)
        bias_ac, bias_an = _sample_bias(a_of_rel, t_new, a_len)
        sink_col = jnp.repeat(b_sinks[l], t_new)[:, None]
        oa, ob = _attn_sample(qkv, kva32, kvb32, caches, l, bias_ac, bias_an, bias_bc,
                              bias_bn, sink_col, t_new=t_new)
        ys = _merge(oa, ob, gates, ys, wa16, wb16, wo16, g_post, l, tm=ms)
        ys, = _ffn(ys, gf_pre, wu16, wd16, gf_post, l, tm=ms, tf=1024)
        sample_kva.append(kva32)
        sample_kvb.append(kvb32)

    akp, avp, bkp, bvp = _prompt_caches(prompt_kva, prompt_kvb, batch=batch, seq=seq,
                                        na=min(a_len, seq), nb=min(b_len, seq))
    aks, avs, bks, bvs = _roll_caches(jnp.stack(sample_kva), jnp.stack(sample_kvb),
                                      caches, t_new=t_new)
    return (yp.reshape(batch, seq, d), ys.reshape(dec_batch, t_new, d),
            _cache_untranspose(akp, A_HEADS), _cache_untranspose(avp, A_HEADS),
            _cache_untranspose(bkp, B_KV_HEADS), _cache_untranspose(bvp, B_KV_HEADS),
            _cache_untranspose(aks, A_HEADS), _cache_untranspose(avs, A_HEADS),
            _cache_untranspose(bks, B_KV_HEADS), _cache_untranspose(bvs, B_KV_HEADS))
```

```python
import functools
import math

import jax
import jax.numpy as jnp
from jax import lax
from jax.experimental import pallas as pl
from jax.experimental.pallas import tpu as pltpu

D_MODEL = 2048
CHUNK = 64
HEAD_DIM = 64
A_HEADS = 16
A_WIDTH = A_HEADS * HEAD_DIM
A_LEFT_CHUNKS = 8
A_REL_CLIP = 256
B_HEADS = 16
B_KV_HEADS = 2
B_GROUP = B_HEADS // B_KV_HEADS
B_WIDTH = B_HEADS * HEAD_DIM
B_KV_WIDTH = B_KV_HEADS * HEAD_DIM
B_LEFT_CHUNKS = 2
T5_BUCKETS = 32
T5_MAX_DIST = 128
EPS = 1e-6
NEG_INF = -1e30
LOG2E = math.log2(math.e)
Q_SCALE = HEAD_DIM ** -0.5 * LOG2E

LANES = 128
MXU_COLS = 256

COL_QA = 0
COL_KA = A_WIDTH
COL_VA = 2 * A_WIDTH
COL_QB = 3 * A_WIDTH
COL_KB = 3 * A_WIDTH + B_WIDTH
COL_GA = COL_KB + 2 * B_KV_WIDTH
PROJ_TN = 4 * MXU_COLS
QKV_KA, QKV_VA, QKV_QA, QKV_QB = 0, 1, 2, 3

A_TQ = 256
B_TQ = 128
A_KBLOCKS = A_LEFT_CHUNKS * CHUNK // A_TQ + 1
B_KBLOCKS = B_LEFT_CHUNKS * CHUNK // B_TQ + 1

VMEM_LIMIT = 56 * 1024 * 1024


def _params(sem, vmem=VMEM_LIMIT):
    return pltpu.CompilerParams(dimension_semantics=sem, vmem_limit_bytes=vmem)


def _rms_scale(x, g):
    return x * lax.rsqrt(jnp.mean(x * x, axis=-1, keepdims=True) + EPS) * g


def _layer_vec_spec(l, d):
    return pl.BlockSpec((None, 1, d), lambda *_: (l, 0, 0))


PROJ_KV_STEPS = 2
PROJ_Q_STEPS = 4
PROJ_KVB_STEP = 4
PROJ_GATE_STEP0 = 5


def _in_proj_kernel(x_ref, g_ref, *refs):
    n_w = PROJ_TN // MXU_COLS
    w_refs = refs[:n_w]
    qkv_ref, kvb16_ref, gates_ref, kva32_ref, kvb32_ref, h_ref = refs[n_w:]
    j = pl.program_id(1)

    @pl.when(j == 0)
    def _():
        h_ref[...] = _rms_scale(x_ref[...], g_ref[...]).astype(h_ref.dtype)

    def tile(out16_ref, out32_ref, scale=None):
        h = h_ref[...]
        for k, w_ref in enumerate(w_refs):
            cols = slice(k * MXU_COLS, (k + 1) * MXU_COLS)
            acc = jnp.dot(h, w_ref[...], preferred_element_type=jnp.float32)
            if out32_ref is not None:
                out32_ref[:, cols] = acc
            if scale is not None:
                acc = acc * scale
            out16_ref[:, cols] = acc.astype(out16_ref.dtype)

    @pl.when(j < PROJ_KV_STEPS)
    def _():
        tile(qkv_ref, kva32_ref)

    @pl.when((j >= PROJ_KV_STEPS) & (j < PROJ_Q_STEPS))
    def _():
        tile(qkv_ref, None, Q_SCALE)

    @pl.when(j == PROJ_KVB_STEP)
    def _():
        acc = jnp.dot(h_ref[...], w_refs[0][...], preferred_element_type=jnp.float32)
        kvb16_ref[...] = acc.astype(kvb16_ref.dtype)
        kvb32_ref[...] = acc

    @pl.when(j >= PROJ_GATE_STEP0)
    def _():
        tile(gates_ref, None)


def _in_proj(x, g, w_in, l, *, tm):
    m, d = x.shape
    n_w = PROJ_TN // MXU_COLS
    n_gate = 2 * D_MODEL // PROJ_TN
    src = [c // MXU_COLS for c in (COL_KA, COL_VA, COL_QA, COL_QB, COL_KB)]
    ga0 = COL_GA // MXU_COLS

    def w_map(k):
        def index_map(i, j):
            base = src[PROJ_KVB_STEP] if k == 0 else src[PROJ_KVB_STEP - 1] + k
            for step in range(PROJ_KVB_STEP - 1, -1, -1):
                base = jnp.where(j == step, src[step] + k, base)
            gate = ga0 + (j - PROJ_GATE_STEP0) * n_w + k
            return (0, jnp.where(j >= PROJ_GATE_STEP0, gate, base))
        return index_map

    kvb_n = 2 * B_KV_WIDTH
    return pl.pallas_call(
        _in_proj_kernel,
        grid=(m // tm, PROJ_GATE_STEP0 + n_gate),
        in_specs=[
            pl.BlockSpec((tm, d), lambda i, j: (i, 0)),
            _layer_vec_spec(l, d),
        ] + [pl.BlockSpec((d, MXU_COLS), w_map(k)) for k in range(n_w)],
        out_specs=[
            pl.BlockSpec((tm, PROJ_TN), lambda i, j: (i, jnp.minimum(j, PROJ_Q_STEPS - 1))),
            pl.BlockSpec((tm, kvb_n), lambda i, j: (i, 0)),
            pl.BlockSpec((tm, PROJ_TN),
                         lambda i, j: (i, jnp.clip(j - PROJ_GATE_STEP0, 0, n_gate - 1))),
            pl.BlockSpec((tm, PROJ_TN), lambda i, j: (i, jnp.minimum(j, PROJ_KV_STEPS - 1))),
            pl.BlockSpec((tm, kvb_n), lambda i, j: (i, 0)),
        ],
        out_shape=[
            jax.ShapeDtypeStruct((m, PROJ_Q_STEPS * PROJ_TN), jnp.bfloat16),
            jax.ShapeDtypeStruct((m, kvb_n), jnp.bfloat16),
            jax.ShapeDtypeStruct((m, 2 * D_MODEL), jnp.bfloat16),
            jax.ShapeDtypeStruct((m, PROJ_KV_STEPS * PROJ_TN), jnp.float32),
            jax.ShapeDtypeStruct((m, kvb_n), jnp.float32),
        ],
        scratch_shapes=[pltpu.VMEM((tm, d), jnp.bfloat16)],
        compiler_params=_params(("parallel", "arbitrary")),
        name="in_proj",
    )(x, g, *([w_in] * n_w))


def _low_half():
    return lax.broadcasted_iota(jnp.int32, (1, LANES), 1) < HEAD_DIM


def _start_mask(tile_in_seq, n_kblocks, tq):
    lane = lax.broadcasted_iota(jnp.int32, (1, n_kblocks * tq), 1)
    first_valid = jnp.maximum(n_kblocks - 1 - tile_in_seq, 0) * tq
    return jnp.where(lane >= first_valid, 0.0, NEG_INF)


def _attn_a_kernel(q_ref, *refs, tiles_per_seq):
    k_refs = refs[:A_KBLOCKS]
    v_refs = refs[A_KBLOCKS:2 * A_KBLOCKS]
    bias_ref = refs[2 * A_KBLOCKS]
    n_cast = (len(refs) - 2 * A_KBLOCKS - 2) // 2
    w32_refs = refs[2 * A_KBLOCKS + 1:2 * A_KBLOCKS + 1 + n_cast]
    o_ref = refs[2 * A_KBLOCKS + 1 + n_cast]
    w16_refs = refs[2 * A_KBLOCKS + 2 + n_cast:]
    tile_in_seq = pl.program_id(0) % tiles_per_seq
    low = _low_half()

    def heads(start):
        for w32_ref, w16_ref in zip(w32_refs, w16_refs):
            w16_ref[...] = w32_ref[...].astype(w16_ref.dtype)
        for p in range(A_HEADS // 2):
            sl = slice(p * LANES, (p + 1) * LANES)
            q2 = q_ref[:, sl]
            k2 = jnp.concatenate([r[:, sl] for r in k_refs], axis=0)
            v2 = jnp.concatenate([r[:, sl] for r in v_refs], axis=0)
            qs = jnp.concatenate([jnp.where(low, q2, 0), jnp.where(low, 0, q2)], axis=0)
            s_all = lax.dot_general(qs, k2, (((1,), (1,)), ((), ())),
                                    preferred_element_type=jnp.float32)
            probs = []
            for hh in range(2):
                s = s_all[hh * A_TQ:(hh + 1) * A_TQ] + bias_ref[2 * p + hh]
                if start is not None:
                    s = s + start
                e = jnp.exp2(s - jnp.max(s, axis=-1, keepdims=True))
                probs.append(e.astype(v2.dtype))
            v_ones = jnp.concatenate([v2, jnp.ones_like(v2)], axis=1)
            r = jnp.dot(jnp.concatenate(probs, axis=0), v_ones,
                        preferred_element_type=jnp.float32)
            out = r[:, :LANES] / r[:, LANES:]
            o_ref[:, sl] = jnp.where(low, out[:A_TQ], out[A_TQ:]).astype(o_ref.dtype)

    @pl.when(tile_in_seq >= A_KBLOCKS - 1)
    def _():
        heads(None)

    @pl.when(tile_in_seq < A_KBLOCKS - 1)
    def _():
        heads(_start_mask(tile_in_seq, A_KBLOCKS, A_TQ))


def _band_kv_map(j, col, n_kblocks, tiles_per_seq):
    def index_map(t, *_):
        i = t % tiles_per_seq
        return (t - i + jnp.maximum(i - (n_kblocks - 1) + j, 0), col)
    return index_map


def _attn_a(qkv, bias, weights, l, *, m, seq):
    tiles_per_seq = seq // A_TQ
    steps = m // A_TQ
    kv_spec = lambda j, col: pl.BlockSpec(
        (A_TQ, A_WIDTH), _band_kv_map(j, col, A_KBLOCKS, tiles_per_seq))
    slab32 = lambda w: pl.BlockSpec((None, w.shape[1] // steps, w.shape[2]),
                                    lambda t: (l, t, 0))
    slab16 = lambda w: pl.BlockSpec((w.shape[1] // steps, w.shape[2]), lambda t: (t, 0))
    return pl.pallas_call(
        functools.partial(_attn_a_kernel, tiles_per_seq=tiles_per_seq),
        grid=(steps,),
        in_specs=[pl.BlockSpec((A_TQ, A_WIDTH), lambda t: (t, QKV_QA))]
        + [kv_spec(j, QKV_KA) for j in range(A_KBLOCKS)]
        + [kv_spec(j, QKV_VA) for j in range(A_KBLOCKS)]
        + [pl.BlockSpec((A_HEADS, A_TQ, A_KBLOCKS * A_TQ), lambda t: (0, 0, 0),
                        pipeline_mode=pl.Buffered(1))]
        + [slab32(w) for w in weights],
        out_specs=[pl.BlockSpec((A_TQ, A_WIDTH), lambda t: (t, 0))]
        + [slab16(w) for w in weights],
        out_shape=[jax.ShapeDtypeStruct((m, A_WIDTH), jnp.bfloat16)]
        + [jax.ShapeDtypeStruct(w.shape[1:], jnp.bfloat16) for w in weights],
        compiler_params=_params(("parallel",)),
        name="attn_a",
    )(qkv, *([qkv] * (2 * A_KBLOCKS)), bias, *weights)


def _dup_half(x, g):
    swapped = pltpu.roll(x, HEAD_DIM, 1)
    low = _low_half()
    return jnp.where(low, x, swapped) if g == 0 else jnp.where(low, swapped, x)


def _attn_b_kernel(sink_ref, q_ref, *refs, tiles_per_seq):
    k_refs = refs[:B_KBLOCKS]
    v_refs = refs[B_KBLOCKS:2 * B_KBLOCKS]
    bias_ref, o_ref = refs[2 * B_KBLOCKS:]
    start = _start_mask(pl.program_id(0) % tiles_per_seq, B_KBLOCKS, B_TQ)
    low = _low_half()
    k2 = jnp.concatenate([r[...] for r in k_refs], axis=0).astype(jnp.float32)
    v2 = jnp.concatenate([r[...] for r in v_refs], axis=0).astype(jnp.float32)
    pairs = B_GROUP // 2
    for g in range(B_KV_HEADS):
        kd = _dup_half(k2, g).astype(jnp.bfloat16)
        vd = _dup_half(v2, g).astype(jnp.bfloat16)
        stacked = []
        for p in range(pairs):
            c0 = (g * pairs + p) * LANES
            q2 = q_ref[:, c0:c0 + LANES]
            stacked.append(jnp.where(low, q2, 0))
            stacked.append(jnp.where(low, 0, q2))
        qs = jnp.concatenate(stacked, axis=0)
        s_all = lax.dot_general(qs, kd, (((1,), (1,)), ((), ())),
                                preferred_element_type=jnp.float32)
        probs, sink_terms = [], []
        for hl in range(B_GROUP):
            h = g * B_GROUP + hl
            s = s_all[hl * B_TQ:(hl + 1) * B_TQ] + bias_ref[h] + start
            sink = sink_ref[h]
            mx = jnp.maximum(jnp.max(s, axis=-1, keepdims=True), sink)
            probs.append(jnp.exp2(s - mx).astype(jnp.bfloat16))
            sink_terms.append(jnp.exp2(sink - mx))
        v_ones = jnp.concatenate([vd, jnp.ones_like(vd)], axis=1)
        r = jnp.dot(jnp.concatenate(probs, axis=0), v_ones,
                    preferred_element_type=jnp.float32)
        outs = []
        for hl in range(B_GROUP):
            rh = r[hl * B_TQ:(hl + 1) * B_TQ]
            outs.append(rh[:, :LANES] / (rh[:, LANES:] + sink_terms[hl]))
        for p in range(pairs):
            c0 = (g * pairs + p) * LANES
            o_ref[:, c0:c0 + LANES] = jnp.where(
                low, outs[2 * p], outs[2 * p + 1]).astype(o_ref.dtype)


def _attn_b(qkv, kvb16, bias, sinks, *, m, seq):
    tiles_per_seq = seq // B_TQ
    kv_spec = lambda j, col: pl.BlockSpec(
        (B_TQ, LANES), _band_kv_map(j, col, B_KBLOCKS, tiles_per_seq))
    return pl.pallas_call(
        functools.partial(_attn_b_kernel, tiles_per_seq=tiles_per_seq),
        grid_spec=pltpu.PrefetchScalarGridSpec(
            num_scalar_prefetch=1,
            grid=(m // B_TQ,),
            in_specs=[pl.BlockSpec((B_TQ, B_WIDTH), lambda t, s: (t, QKV_QB))]
            + [kv_spec(j, 0) for j in range(B_KBLOCKS)]
            + [kv_spec(j, 1) for j in range(B_KBLOCKS)]
            + [pl.BlockSpec((B_HEADS, B_TQ, B_KBLOCKS * B_TQ), lambda t, s: (0, 0, 0))],
            out_specs=pl.BlockSpec((B_TQ, B_WIDTH), lambda t, s: (t, 0)),
        ),
        out_shape=jax.ShapeDtypeStruct((m, B_WIDTH), jnp.bfloat16),
        compiler_params=_params(("parallel",)),
        name="attn_b",
    )(sinks, qkv, *([kvb16] * (2 * B_KBLOCKS)), bias)


def _pad_top(x, t_new):
    return jnp.concatenate([jnp.zeros((LANES - t_new, x.shape[1]), x.dtype), x], axis=0)


def _attn_sample_kernel(qa_ref, qb_ref, kan_ref, van_ref, kbn_ref, vbn_ref,
                        cak_ref, cav_ref, cbk_ref, cbv_ref,
                        bias_ac_ref, bias_an_ref, bias_bc_ref, bias_bn_ref, sink_ref,
                        oa_ref, ob_ref, *, t_new):
    heads = A_HEADS
    rows = heads * t_new
    width = heads * HEAD_DIM
    bf16 = jnp.bfloat16
    row_head = lax.broadcasted_iota(jnp.int32, (rows, width), 0) // t_new
    lane_head = lax.broadcasted_iota(jnp.int32, (rows, width), 1) // HEAD_DIM
    own = row_head == lane_head

    def stack_q(q):
        return jnp.where(own, jnp.concatenate([q] * heads, axis=0), 0)

    def nt_dot(a, b):
        return lax.dot_general(a, b, (((1,), (1,)), ((), ())),
                               preferred_element_type=jnp.float32)

    def finish(s_c, s_n, vt_c, v_n, sink):
        mx = jnp.maximum(jnp.max(s_c, axis=-1, keepdims=True),
                         jnp.max(s_n, axis=-1, keepdims=True))
        if sink is not None:
            mx = jnp.maximum(mx, sink)
        e_c = jnp.exp2(s_c - mx)
        e_n = jnp.exp2(s_n - mx)
        denom = jnp.sum(e_c, axis=-1, keepdims=True) + jnp.sum(e_n, axis=-1, keepdims=True)
        if sink is not None:
            denom = denom + jnp.exp2(sink - mx)
        o = nt_dot(e_c.astype(bf16), vt_c)
        o = o + jnp.dot(e_n.astype(bf16), v_n, preferred_element_type=jnp.float32)
        o = jnp.where(own, o / denom, 0.0)
        return jnp.sum(o.reshape(heads, t_new, width), axis=0)

    k_n = _pad_top(kan_ref[...], t_new).astype(bf16)
    v_n = _pad_top(van_ref[...], t_new).astype(bf16)
    qs = stack_q(qa_ref[...])
    s_c = jnp.dot(qs, cak_ref[...].astype(bf16),
                  preferred_element_type=jnp.float32) + bias_ac_ref[...]
    s_n = nt_dot(qs, k_n) + bias_an_ref[...]
    oa_ref[...] = finish(s_c, s_n, cav_ref[...].astype(bf16), v_n, None).astype(oa_ref.dtype)

    def expansion(shape, src_axis):
        src = lax.broadcasted_iota(jnp.int32, shape, src_axis)
        dst = lax.broadcasted_iota(jnp.int32, shape, 1 - src_axis)
        return ((src % HEAD_DIM == dst % HEAD_DIM)
                & (src // HEAD_DIM == dst // (B_GROUP * HEAD_DIM))).astype(bf16)

    def widen(x):
        return jnp.dot(x.astype(bf16), expansion((B_KV_WIDTH, width), 0),
                       preferred_element_type=jnp.float32).astype(bf16)

    def widen_t(xt):
        return jnp.dot(expansion((width, B_KV_WIDTH), 1), xt.astype(bf16),
                       preferred_element_type=jnp.float32).astype(bf16)

    k_n = widen(_pad_top(kbn_ref[...], t_new))
    v_n = widen(_pad_top(vbn_ref[...], t_new))
    qs = stack_q(qb_ref[...])
    s_c = jnp.dot(qs, widen_t(cbk_ref[...]),
                  preferred_element_type=jnp.float32) + bias_bc_ref[...]
    s_n = nt_dot(qs, k_n) + bias_bn_ref[...]
    ob_ref[...] = finish(s_c, s_n, widen_t(cbv_ref[...]), v_n, sink_ref[...]).astype(ob_ref.dtype)


def _attn_sample(qkv, kva32, kvb32, caches, l, bias_ac, bias_an, bias_bc, bias_bn,
                 sink_col, *, t_new):
    m = qkv.shape[0]
    cak, cav, cbk, cbv = caches
    a_len, b_len = cak.shape[-1], cbk.shape[-1]
    rows = A_HEADS * t_new
    const = lambda shape: pl.BlockSpec(shape, lambda b: (0,) * len(shape))
    cache_spec = lambda c: pl.BlockSpec((None, None) + c.shape[2:], lambda b: (l, b, 0, 0))
    return pl.pallas_call(
        functools.partial(_attn_sample_kernel, t_new=t_new),
        grid=(m // t_new,),
        in_specs=[
            pl.BlockSpec((t_new, A_WIDTH), lambda b: (b, QKV_QA)),
            pl.BlockSpec((t_new, B_WIDTH), lambda b: (b, QKV_QB)),
            pl.BlockSpec((t_new, A_WIDTH), lambda b: (b, 0)),
            pl.BlockSpec((t_new, A_WIDTH), lambda b: (b, 1)),
            pl.BlockSpec((t_new, B_KV_WIDTH), lambda b: (b, 0)),
            pl.BlockSpec((t_new, B_KV_WIDTH), lambda b: (b, 1)),
            cache_spec(cak), cache_spec(cav), cache_spec(cbk), cache_spec(cbv),
            const((rows, a_len)), const((rows, LANES)),
            const((rows, b_len)), const((rows, LANES)),
            const((rows, 1)),
        ],
        out_specs=[
            pl.BlockSpec((t_new, A_WIDTH), lambda b: (b, 0)),
            pl.BlockSpec((t_new, B_WIDTH), lambda b: (b, 0)),
        ],
        out_shape=[
            jax.ShapeDtypeStruct((m, A_WIDTH), jnp.bfloat16),
            jax.ShapeDtypeStruct((m, B_WIDTH), jnp.bfloat16),
        ],
        compiler_params=_params(("parallel",)),
        name="attn_sample",
    )(qkv, qkv, kva32, kva32, kvb32, kvb32, cak, cav, cbk, cbv,
      bias_ac, bias_an, bias_bc, bias_bn, sink_col)


def _roll_caches_kernel(kan_ref, van_ref, kbn_ref, vbn_ref,
                        cak_ref, cav_ref, cbk_ref, cbv_ref,
                        nak_ref, nav_ref, nbk_ref, nbv_ref, *, t_new):
    def roll_in(cache_ref, new_ref, out_ref):
        cache_t = cache_ref[...]
        n = cache_t.shape[1]
        shifted = pltpu.roll(cache_t, n - t_new, 1)
        lane = lax.broadcasted_iota(jnp.int32, (1, LANES), 1)
        tail = jnp.where(lane < LANES - t_new, shifted[:, n - LANES:],
                         _pad_top(new_ref[...], t_new).T)
        if n > LANES:
            out_ref[:, :n - LANES] = shifted[:, :n - LANES]
        out_ref[:, n - LANES:] = tail

    roll_in(cak_ref, kan_ref, nak_ref)
    roll_in(cav_ref, van_ref, nav_ref)
    roll_in(cbk_ref, kbn_ref, nbk_ref)
    roll_in(cbv_ref, vbn_ref, nbv_ref)


def _roll_caches(new_a, new_b, caches, *, t_new):
    depth, streams = caches[0].shape[:2]
    cache_spec = lambda c: pl.BlockSpec((None, None) + c.shape[2:], lambda l, b: (l, b, 0, 0))
    new_spec = lambda width, col: pl.BlockSpec((None, t_new, width), lambda l, b: (l, b, col))
    return pl.pallas_call(
        functools.partial(_roll_caches_kernel, t_new=t_new),
        grid=(depth, streams),
        in_specs=[new_spec(A_WIDTH, 0), new_spec(A_WIDTH, 1),
                  new_spec(B_KV_WIDTH, 0), new_spec(B_KV_WIDTH, 1)]
        + [cache_spec(c) for c in caches],
        out_specs=[cache_spec(c) for c in caches],
        out_shape=[jax.ShapeDtypeStruct(c.shape, jnp.float32) for c in caches],
        compiler_params=_params(("parallel", "parallel")),
        name="roll_caches",
    )(new_a, new_a, new_b, new_b, *caches)


def _prompt_caches_kernel(*refs, depth):
    in_refs, (ak_ref, av_ref, bk_ref, bv_ref) = refs[:4 * depth], refs[4 * depth:]
    for k in range(depth):
        @pl.when(pl.program_id(0) == k)
        def _(k=k):
            ka_ref, va_ref, kb_ref, vb_ref = in_refs[4 * k:4 * k + 4]
            ak_ref[...] = ka_ref[...].T
            av_ref[...] = va_ref[...].T
            bk_ref[...] = kb_ref[...].T
            bv_ref[...] = vb_ref[...].T


def _prompt_caches(kva32s, kvb32s, *, batch, seq, na, nb):
    depth = len(kva32s)

    def rows_map(k, tile, col):
        def index_map(l, b):
            bb = jnp.where(l < k, 0, jnp.where(l > k, batch - 1, b))
            return ((bb + 1) * tile - 1, col)
        return index_map

    in_specs, args = [], []
    for k in range(depth):
        in_specs += [
            pl.BlockSpec((na, A_WIDTH), rows_map(k, seq // na, 0)),
            pl.BlockSpec((na, A_WIDTH), rows_map(k, seq // na, 1)),
            pl.BlockSpec((nb, B_KV_WIDTH), rows_map(k, seq // nb, 0)),
            pl.BlockSpec((nb, B_KV_WIDTH), rows_map(k, seq // nb, 1)),
        ]
        args += [kva32s[k], kva32s[k], kvb32s[k], kvb32s[k]]
    out_shapes = [(depth, batch, A_WIDTH, na)] * 2 + [(depth, batch, B_KV_WIDTH, nb)] * 2
    return pl.pallas_call(
        functools.partial(_prompt_caches_kernel, depth=depth),
        grid=(depth, batch),
        in_specs=in_specs,
        out_specs=[pl.BlockSpec((None, None) + s[2:], lambda l, b: (l, b, 0, 0))
                   for s in out_shapes],
        out_shape=[jax.ShapeDtypeStruct(s, jnp.float32) for s in out_shapes],
        compiler_params=_params(("arbitrary", "arbitrary")),
        name="prompt_caches",
    )(*args)


def _merge_kernel(oa_ref, ob_ref, ga_ref, gb_ref, x_ref, wa_ref, wb_ref, wo_ref,
                  g_ref, y_ref):
    ta = jnp.dot(oa_ref[...], wa_ref[...], preferred_element_type=jnp.float32)
    tb = jnp.dot(ob_ref[...], wb_ref[...], preferred_element_type=jnp.float32)
    mixed = (jax.nn.sigmoid(ga_ref[...].astype(jnp.float32)) * ta
             + jax.nn.sigmoid(gb_ref[...].astype(jnp.float32)) * tb)
    z = jnp.dot(mixed.astype(jnp.bfloat16), wo_ref[...],
                preferred_element_type=jnp.float32)
    y_ref[...] = x_ref[...] + _rms_scale(z, g_ref[...])


def _merge(oa, ob, gates, x, wa, wb, wo, g, l, *, tm):
    m, d = x.shape
    resident = lambda w: pl.BlockSpec(w.shape, lambda i: (0, 0),
                                      pipeline_mode=pl.Buffered(1))
    return pl.pallas_call(
        _merge_kernel,
        grid=(m // tm,),
        in_specs=[
            pl.BlockSpec((tm, A_WIDTH), lambda i: (i, 0)),
            pl.BlockSpec((tm, B_WIDTH), lambda i: (i, 0)),
            pl.BlockSpec((tm, d), lambda i: (i, 0)),
            pl.BlockSpec((tm, d), lambda i: (i, 1)),
            pl.BlockSpec((tm, d), lambda i: (i, 0)),
            resident(wa), resident(wb), resident(wo),
            _layer_vec_spec(l, d),
        ],
        out_specs=pl.BlockSpec((tm, d), lambda i: (i, 0)),
        out_shape=jax.ShapeDtypeStruct((m, d), jnp.float32),
        compiler_params=_params(("parallel",)),
        name="merge",
    )(oa, ob, gates, gates, x, wa, wb, wo, g)


def _ffn_kernel(x_ref, gpre_ref, wu_ref, wd_ref, gpost_ref, *refs):
    f = pl.program_id(1)
    if len(refs) == 4:
        w32_ref, y_ref, w16_ref, h_ref = refs
        w16_ref[...] = w32_ref[...].astype(w16_ref.dtype)
    else:
        y_ref, h_ref = refs

    @pl.when(f == 0)
    def _():
        h_ref[...] = _rms_scale(x_ref[...], gpre_ref[...]).astype(h_ref.dtype)
        y_ref[...] = jnp.zeros_like(y_ref)

    u = jnp.dot(h_ref[...], wu_ref[...], preferred_element_type=jnp.float32)
    u = jnp.square(jnp.maximum(u, 0.0)).astype(jnp.bfloat16)
    y_ref[...] += jnp.dot(u, wd_ref[...], preferred_element_type=jnp.float32)

    @pl.when(f == pl.num_programs(1) - 1)
    def _():
        y_ref[...] = x_ref[...] + _rms_scale(y_ref[...], gpost_ref[...])


def _ffn(x, gpre, wu, wd, gpost, l, cast=None, *, tm, tf):
    m, d = x.shape
    dff = wu.shape[-1]
    n_f = dff // tf
    in_specs = [
        pl.BlockSpec((tm, d), lambda i, f: (i, 0)),
        _layer_vec_spec(l, d),
        pl.BlockSpec((d, tf), lambda i, f: (0, f)),
        pl.BlockSpec((tf, d), lambda i, f: (f, 0)),
        _layer_vec_spec(l, d),
    ]
    out_specs = [pl.BlockSpec((tm, d), lambda i, f: (i, 0))]
    out_shape = [jax.ShapeDtypeStruct((m, d), jnp.float32)]
    args = [x, gpre, wu, wd, gpost]
    if cast is not None:
        w, lc = cast
        rows = w.shape[1] // ((m // tm) * n_f)
        in_specs.append(pl.BlockSpec((None, rows, w.shape[2]),
                                     lambda i, f: (lc, i * n_f + f, 0)))
        out_specs.append(pl.BlockSpec((rows, w.shape[2]), lambda i, f: (i * n_f + f, 0)))
        out_shape.append(jax.ShapeDtypeStruct(w.shape[1:], jnp.bfloat16))
        args.append(w)
    return pl.pallas_call(
        _ffn_kernel,
        grid=(m // tm, n_f),
        in_specs=in_specs,
        out_specs=out_specs,
        out_shape=out_shape,
        scratch_shapes=[pltpu.VMEM((tm, d), jnp.bfloat16)],
        compiler_params=_params(("arbitrary", "arbitrary")),
        name="ffn",
    )(*args)


def _t5_bucket(rel):
    half = T5_BUCKETS // 2
    exact = half // 2
    ret = jnp.where(rel > 0, half, 0)
    n = jnp.abs(rel)
    large = exact + (jnp.log(jnp.maximum(n, 1).astype(jnp.float32) / exact)
                     / math.log(T5_MAX_DIST / exact) * (half - exact)).astype(jnp.int32)
    large = jnp.minimum(large, half - 1)
    return ret + jnp.where(n < exact, n, large)


def _a_bias_of_rel(table):
    scaled = table.T * LOG2E
    return lambda rel: scaled[:, jnp.clip(rel, -A_REL_CLIP, A_REL_CLIP) + A_REL_CLIP]


def _b_bias_of_rel(table):
    scaled = table.T * LOG2E
    return lambda rel: scaled[:, _t5_bucket(-rel)]


def _hankel(u, q, n):
    heads, k = u.shape
    period = q + n
    u = jnp.pad(u, ((0, 0), (0, period - k)))
    flat = jnp.tile(u, (1, q + 1))[:, :q * (period + 1)]
    return flat.reshape(heads, q, period + 1)[:, :, :n]


def _rel_bias(bias_of_rel, q_len, n_keys, k0):
    k = jnp.arange(q_len + n_keys - 1)
    u = bias_of_rel(k - (n_keys - 1) - k0)
    return _hankel(u, q_len, n_keys)[:, :, ::-1]


def _band_tile(bias_of_rel, tq, n_prev):
    window = tq + n_prev * CHUNK
    band = _rel_bias(bias_of_rel, CHUNK, (n_prev + 1) * CHUNK, -n_prev * CHUNK)
    blocks = []
    for c in range(tq // CHUNK):
        left = c * CHUNK
        right = window - left - band.shape[-1]
        blocks.append(jnp.pad(band, ((0, 0), (0, 0), (left, right)),
                              constant_values=NEG_INF))
    return jnp.concatenate(blocks, axis=1)


def _sample_bias(bias_of_rel, t_new, n_cache):
    full = _rel_bias(bias_of_rel, t_new, n_cache + t_new, -n_cache)
    full = full.reshape(-1, n_cache + t_new)
    new = jnp.pad(full[:, n_cache:], ((0, 0), (LANES - t_new, 0)),
                  constant_values=NEG_INF)
    return full[:, :n_cache], new


def _cache_t(c):
    d, s, r, h, e = c.shape
    return jnp.transpose(c, (0, 1, 3, 4, 2)).reshape(d, s, h * e, r)


def _cache_untranspose(c, heads):
    d, s, w, r = c.shape
    return jnp.transpose(c.reshape(d, s, heads, w // heads, r), (0, 1, 4, 2, 3))


def kernel(x_prompt, x_sample, cache_a_k, cache_a_v, cache_b_k, cache_b_v, w_in,
           w_a_out, w_b_out, w_out, a_rel_table, t5_table, b_sinks, g_mix_pre,
           g_mix_post, g_ffn_pre, g_ffn_post, w_up, w_down):
    depth = w_in.shape[0]
    batch, seq, d = x_prompt.shape
    dec_batch, t_new, _ = x_sample.shape
    a_len = cache_a_k.shape[2]
    b_len = cache_b_k.shape[2]
    bf16 = jnp.bfloat16
    mp = batch * seq
    ms = dec_batch * t_new

    yp = x_prompt.reshape(mp, d)
    ys = x_sample.reshape(ms, d)

    w_in16 = w_in[0].astype(bf16)
    later_weights = (w_a_out, w_b_out, w_out, w_up, w_down)
    g_pre = g_mix_pre[:, None, :]
    g_post = g_mix_post[:, None, :]
    gf_pre = g_ffn_pre[:, None, :]
    gf_post = g_ffn_post[:, None, :]
    caches = tuple(_cache_t(c) for c in (cache_a_k, cache_a_v, cache_b_k, cache_b_v))

    b_of_rel = _b_bias_of_rel(t5_table)
    bias_b_tile = _band_tile(b_of_rel, B_TQ, B_LEFT_CHUNKS)
    bias_bc, bias_bn = _sample_bias(b_of_rel, t_new, b_len)

    prompt_kva, prompt_kvb, sample_kva, sample_kvb = [], [], [], []
    for l in range(depth):
        a_of_rel = _a_bias_of_rel(a_rel_table[l])
        sinks = b_sinks[l] * LOG2E

        qkv, kvb16, gates, kva32, kvb32 = _in_proj(yp, g_pre, w_in16, l, tm=1024)
        oa, wa16, wb16, wo16, wu16, wd16 = _attn_a(
            qkv, _band_tile(a_of_rel, A_TQ, A_LEFT_CHUNKS), later_weights, l, m=mp, seq=seq)
        ob = _attn_b(qkv, kvb16, bias_b_tile, sinks, m=mp, seq=seq)
        yp = _merge(oa, ob, gates, yp, wa16, wb16, wo16, g_post, l, tm=256)
        w_in16_this = w_in16
        if l + 1 < depth:
            yp, w_in16 = _ffn(yp, gf_pre, wu16, wd16, gf_post, l, (w_in, l + 1),
                              tm=512, tf=1024)
        else:
            yp, = _ffn(yp, gf_pre, wu16, wd16, gf_post, l, tm=512, tf=1024)
        prompt_kva.append(kva32)
        prompt_kvb.append(kvb32)

        qkv, _, gates, kva32, kvb32 = _in_proj(ys, g_pre, w_in16_this, l, tm=ms)
        bias_ac, bias_an = _sample_bias(a_of_rel, t_new, a_len)
        sink_col = jnp.repeat(sinks, t_new)[:, None]
        oa, ob = _attn_sample(qkv, kva32, kvb32, caches, l, bias_ac, bias_an, bias_bc,
                              bias_bn, sink_col, t_new=t_new)
        ys = _merge(oa, ob, gates, ys, wa16, wb16, wo16, g_post, l, tm=ms)
        ys, = _ffn(ys, gf_pre, wu16, wd16, gf_post, l, tm=ms, tf=1024)
        sample_kva.append(kva32)
        sample_kvb.append(kvb32)

    akp, avp, bkp, bvp = _prompt_caches(prompt_kva, prompt_kvb, batch=batch, seq=seq,
                                        na=min(a_len, seq), nb=min(b_len, seq))
    aks, avs, bks, bvs = _roll_caches(jnp.stack(sample_kva), jnp.stack(sample_kvb),
                                      caches, t_new=t_new)
    return (yp.reshape(batch, seq, d), ys.reshape(dec_batch, t_new, d),
            _cache_untranspose(akp, A_HEADS), _cache_untranspose(avp, A_HEADS),
            _cache_untranspose(bkp, B_KV_HEADS), _cache_untranspose(bvp, B_KV_HEADS),
            _cache_untranspose(aks, A_HEADS), _cache_untranspose(avs, A_HEADS),
            _cache_untranspose(bks, B_KV_HEADS), _cache_untranspose(bvs, B_KV_HEADS))
```

```python
import functools
import math

import jax
import jax.numpy as jnp
from jax import lax
from jax.experimental import pallas as pl
from jax.experimental.pallas import tpu as pltpu

D_MODEL = 2048
CHUNK = 64
HEAD_DIM = 64
A_HEADS = 16
A_WIDTH = A_HEADS * HEAD_DIM
A_LEFT_CHUNKS = 8
A_REL_CLIP = 256
B_HEADS = 16
B_KV_HEADS = 2
B_GROUP = B_HEADS // B_KV_HEADS
B_WIDTH = B_HEADS * HEAD_DIM
B_KV_WIDTH = B_KV_HEADS * HEAD_DIM
B_LEFT_CHUNKS = 2
T5_BUCKETS = 32
T5_MAX_DIST = 128
EPS = 1e-6
NEG_INF = -1e30
LOG2E = math.log2(math.e)
Q_SCALE = HEAD_DIM ** -0.5 * LOG2E

LANES = 128
MXU_COLS = 256

COL_QA = 0
COL_KA = A_WIDTH
COL_VA = 2 * A_WIDTH
COL_QB = 3 * A_WIDTH
COL_KB = 3 * A_WIDTH + B_WIDTH
COL_GA = COL_KB + 2 * B_KV_WIDTH
PROJ_TN = 4 * MXU_COLS
QKV_KA, QKV_VA, QKV_QA, QKV_QB = 0, 1, 2, 3

A_TQ = 256
B_TQ = 128
A_KBLOCKS = A_LEFT_CHUNKS * CHUNK // A_TQ + 1
B_KBLOCKS = B_LEFT_CHUNKS * CHUNK // B_TQ + 1

VMEM_LIMIT = 56 * 1024 * 1024


def _params(sem, vmem=VMEM_LIMIT):
    return pltpu.CompilerParams(dimension_semantics=sem, vmem_limit_bytes=vmem)


def _rms_scale(x, g):
    return x * lax.rsqrt(jnp.mean(x * x, axis=-1, keepdims=True) + EPS) * g


def _layer_vec_spec(l, d):
    return pl.BlockSpec((None, 1, d), lambda *_: (l, 0, 0))


PROJ_KV_STEPS = 2
PROJ_Q_STEPS = 4
PROJ_KVB_STEP = 4
PROJ_GATE_STEP0 = 5


def _in_proj_kernel(x_ref, g_ref, *refs):
    n_w = PROJ_TN // MXU_COLS
    w_refs = refs[:n_w]
    qkv_ref, kvb16_ref, gates_ref, kva32_ref, kvb32_ref, h_ref = refs[n_w:]
    j = pl.program_id(1)

    @pl.when(j == 0)
    def _():
        h_ref[...] = _rms_scale(x_ref[...], g_ref[...]).astype(h_ref.dtype)

    def tile(out16_ref, out32_ref, scale=None):
        h = h_ref[...]
        for k, w_ref in enumerate(w_refs):
            cols = slice(k * MXU_COLS, (k + 1) * MXU_COLS)
            acc = jnp.dot(h, w_ref[...], preferred_element_type=jnp.float32)
            if out32_ref is not None:
                out32_ref[:, cols] = acc
            if scale is not None:
                acc = acc * scale
            out16_ref[:, cols] = acc.astype(out16_ref.dtype)

    @pl.when(j < PROJ_KV_STEPS)
    def _():
        tile(qkv_ref, kva32_ref)

    @pl.when((j >= PROJ_KV_STEPS) & (j < PROJ_Q_STEPS))
    def _():
        tile(qkv_ref, None, Q_SCALE)

    @pl.when(j == PROJ_KVB_STEP)
    def _():
        acc = jnp.dot(h_ref[...], w_refs[0][...], preferred_element_type=jnp.float32)
        kvb16_ref[...] = acc.astype(kvb16_ref.dtype)
        kvb32_ref[...] = acc

    @pl.when(j >= PROJ_GATE_STEP0)
    def _():
        tile(gates_ref, None)


def _in_proj(x, g, w_in, l, *, tm):
    m, d = x.shape
    n_w = PROJ_TN // MXU_COLS
    n_gate = 2 * D_MODEL // PROJ_TN
    src = [c // MXU_COLS for c in (COL_KA, COL_VA, COL_QA, COL_QB, COL_KB)]
    ga0 = COL_GA // MXU_COLS

    def w_map(k):
        def index_map(i, j):
            base = src[PROJ_KVB_STEP] if k == 0 else src[PROJ_KVB_STEP - 1] + k
            for step in range(PROJ_KVB_STEP - 1, -1, -1):
                base = jnp.where(j == step, src[step] + k, base)
            gate = ga0 + (j - PROJ_GATE_STEP0) * n_w + k
            return (jnp.where(j >= PROJ_GATE_STEP0, gate, base), 0, 0)
        return index_map

    kvb_n = 2 * B_KV_WIDTH
    return pl.pallas_call(
        _in_proj_kernel,
        grid=(m // tm, PROJ_GATE_STEP0 + n_gate),
        in_specs=[
            pl.BlockSpec((tm, d), lambda i, j: (i, 0)),
            _layer_vec_spec(l, d),
        ] + [pl.BlockSpec((None, d, MXU_COLS), w_map(k)) for k in range(n_w)],
        out_specs=[
            pl.BlockSpec((tm, PROJ_TN), lambda i, j: (i, jnp.minimum(j, PROJ_Q_STEPS - 1))),
            pl.BlockSpec((tm, kvb_n), lambda i, j: (i, 0)),
            pl.BlockSpec((tm, PROJ_TN),
                         lambda i, j: (i, jnp.clip(j - PROJ_GATE_STEP0, 0, n_gate - 1))),
            pl.BlockSpec((tm, PROJ_TN), lambda i, j: (i, jnp.minimum(j, PROJ_KV_STEPS - 1))),
            pl.BlockSpec((tm, kvb_n), lambda i, j: (i, 0)),
        ],
        out_shape=[
            jax.ShapeDtypeStruct((m, PROJ_Q_STEPS * PROJ_TN), jnp.bfloat16),
            jax.ShapeDtypeStruct((m, kvb_n), jnp.bfloat16),
            jax.ShapeDtypeStruct((m, 2 * D_MODEL), jnp.bfloat16),
            jax.ShapeDtypeStruct((m, PROJ_KV_STEPS * PROJ_TN), jnp.float32),
            jax.ShapeDtypeStruct((m, kvb_n), jnp.float32),
        ],
        scratch_shapes=[pltpu.VMEM((tm, d), jnp.bfloat16)],
        compiler_params=_params(("parallel", "arbitrary")),
        name="in_proj",
    )(x, g, *([w_in] * n_w))


def _low_half():
    return lax.broadcasted_iota(jnp.int32, (1, LANES), 1) < HEAD_DIM


def _start_mask(tile_in_seq, n_kblocks, tq):
    lane = lax.broadcasted_iota(jnp.int32, (1, n_kblocks * tq), 1)
    first_valid = jnp.maximum(n_kblocks - 1 - tile_in_seq, 0) * tq
    return jnp.where(lane >= first_valid, 0.0, NEG_INF)


def _attn_a_kernel(q_ref, *refs, tiles_per_seq):
    k_refs = refs[:A_KBLOCKS]
    v_refs = refs[A_KBLOCKS:2 * A_KBLOCKS]
    bias_ref = refs[2 * A_KBLOCKS]
    n_cast = (len(refs) - 2 * A_KBLOCKS - 2) // 2
    w32_refs = refs[2 * A_KBLOCKS + 1:2 * A_KBLOCKS + 1 + n_cast]
    o_ref = refs[2 * A_KBLOCKS + 1 + n_cast]
    w16_refs = refs[2 * A_KBLOCKS + 2 + n_cast:]
    tile_in_seq = pl.program_id(0) % tiles_per_seq
    low = _low_half()

    def heads(start):
        for w32_ref, w16_ref in zip(w32_refs, w16_refs):
            w16_ref[...] = w32_ref[...].astype(w16_ref.dtype)
        for p in range(A_HEADS // 2):
            sl = slice(p * LANES, (p + 1) * LANES)
            q2 = q_ref[:, sl]
            k2 = jnp.concatenate([r[:, sl] for r in k_refs], axis=0)
            v2 = jnp.concatenate([r[:, sl] for r in v_refs], axis=0)
            qs = jnp.concatenate([jnp.where(low, q2, 0), jnp.where(low, 0, q2)], axis=0)
            s_all = lax.dot_general(qs, k2, (((1,), (1,)), ((), ())),
                                    preferred_element_type=jnp.float32)
            probs = []
            for hh in range(2):
                s = s_all[hh * A_TQ:(hh + 1) * A_TQ] + bias_ref[2 * p + hh]
                if start is not None:
                    s = s + start
                s = s.astype(v2.dtype)
                probs.append(jnp.exp2(s - jnp.max(s, axis=-1, keepdims=True)))
            v_ones = jnp.concatenate([v2, jnp.ones_like(v2)], axis=1)
            r = jnp.dot(jnp.concatenate(probs, axis=0), v_ones,
                        preferred_element_type=jnp.float32)
            out = r[:, :LANES] / r[:, LANES:]
            o_ref[:, sl] = jnp.where(low, out[:A_TQ], out[A_TQ:]).astype(o_ref.dtype)

    @pl.when(tile_in_seq >= A_KBLOCKS - 1)
    def _():
        heads(None)

    @pl.when(tile_in_seq < A_KBLOCKS - 1)
    def _():
        heads(_start_mask(tile_in_seq, A_KBLOCKS, A_TQ))


def _band_kv_map(j, col, n_kblocks, tiles_per_seq):
    def index_map(t, *_):
        i = t % tiles_per_seq
        return (t - i + jnp.maximum(i - (n_kblocks - 1) + j, 0), col)
    return index_map


def _attn_a(qkv, bias, weights, l, *, m, seq):
    tiles_per_seq = seq // A_TQ
    steps = m // A_TQ
    kv_spec = lambda j, col: pl.BlockSpec(
        (A_TQ, A_WIDTH), _band_kv_map(j, col, A_KBLOCKS, tiles_per_seq))
    slab32 = lambda w: pl.BlockSpec((None, w.shape[1] // steps, w.shape[2]),
                                    lambda t: (l, t, 0))
    slab16 = lambda w: pl.BlockSpec((w.shape[1] // steps, w.shape[2]), lambda t: (t, 0))
    return pl.pallas_call(
        functools.partial(_attn_a_kernel, tiles_per_seq=tiles_per_seq),
        grid=(steps,),
        in_specs=[pl.BlockSpec((A_TQ, A_WIDTH), lambda t: (t, QKV_QA))]
        + [kv_spec(j, QKV_KA) for j in range(A_KBLOCKS)]
        + [kv_spec(j, QKV_VA) for j in range(A_KBLOCKS)]
        + [pl.BlockSpec((A_HEADS, A_TQ, A_KBLOCKS * A_TQ), lambda t: (0, 0, 0),
                        pipeline_mode=pl.Buffered(1))]
        + [slab32(w) for w in weights],
        out_specs=[pl.BlockSpec((A_TQ, A_WIDTH), lambda t: (t, 0))]
        + [slab16(w) for w in weights],
        out_shape=[jax.ShapeDtypeStruct((m, A_WIDTH), jnp.bfloat16)]
        + [jax.ShapeDtypeStruct(w.shape[1:], jnp.bfloat16) for w in weights],
        compiler_params=_params(("parallel",)),
        name="attn_a",
    )(qkv, *([qkv] * (2 * A_KBLOCKS)), bias, *weights)


def _dup_half(x, g):
    swapped = pltpu.roll(x, HEAD_DIM, 1)
    low = _low_half()
    return jnp.where(low, x, swapped) if g == 0 else jnp.where(low, swapped, x)


def _attn_b_kernel(sink_ref, q_ref, *refs, tiles_per_seq):
    k_refs = refs[:B_KBLOCKS]
    v_refs = refs[B_KBLOCKS:2 * B_KBLOCKS]
    bias_ref, o_ref = refs[2 * B_KBLOCKS:]
    start = _start_mask(pl.program_id(0) % tiles_per_seq, B_KBLOCKS, B_TQ)
    low = _low_half()
    k2 = jnp.concatenate([r[...] for r in k_refs], axis=0).astype(jnp.float32)
    v2 = jnp.concatenate([r[...] for r in v_refs], axis=0).astype(jnp.float32)
    pairs = B_GROUP // 2
    for g in range(B_KV_HEADS):
        kd = _dup_half(k2, g).astype(jnp.bfloat16)
        vd = _dup_half(v2, g).astype(jnp.bfloat16)
        stacked = []
        for p in range(pairs):
            c0 = (g * pairs + p) * LANES
            q2 = q_ref[:, c0:c0 + LANES]
            stacked.append(jnp.where(low, q2, 0))
            stacked.append(jnp.where(low, 0, q2))
        qs = jnp.concatenate(stacked, axis=0)
        s_all = lax.dot_general(qs, kd, (((1,), (1,)), ((), ())),
                                preferred_element_type=jnp.float32)
        probs, sink_terms = [], []
        for hl in range(B_GROUP):
            h = g * B_GROUP + hl
            s = s_all[hl * B_TQ:(hl + 1) * B_TQ] + bias_ref[h] + start
            sink = sink_ref[h]
            mx = jnp.maximum(jnp.max(s, axis=-1, keepdims=True), sink)
            probs.append(jnp.exp2(s - mx).astype(jnp.bfloat16))
            sink_terms.append(jnp.exp2(sink - mx))
        v_ones = jnp.concatenate([vd, jnp.ones_like(vd)], axis=1)
        r = jnp.dot(jnp.concatenate(probs, axis=0), v_ones,
                    preferred_element_type=jnp.float32)
        outs = []
        for hl in range(B_GROUP):
            rh = r[hl * B_TQ:(hl + 1) * B_TQ]
            outs.append(rh[:, :LANES] / (rh[:, LANES:] + sink_terms[hl]))
        for p in range(pairs):
            c0 = (g * pairs + p) * LANES
            o_ref[:, c0:c0 + LANES] = jnp.where(
                low, outs[2 * p], outs[2 * p + 1]).astype(o_ref.dtype)


def _attn_b(qkv, kvb16, bias, sinks, *, m, seq):
    tiles_per_seq = seq // B_TQ
    kv_spec = lambda j, col: pl.BlockSpec(
        (B_TQ, LANES), _band_kv_map(j, col, B_KBLOCKS, tiles_per_seq))
    return pl.pallas_call(
        functools.partial(_attn_b_kernel, tiles_per_seq=tiles_per_seq),
        grid_spec=pltpu.PrefetchScalarGridSpec(
            num_scalar_prefetch=1,
            grid=(m // B_TQ,),
            in_specs=[pl.BlockSpec((B_TQ, B_WIDTH), lambda t, s: (t, QKV_QB))]
            + [kv_spec(j, 0) for j in range(B_KBLOCKS)]
            + [kv_spec(j, 1) for j in range(B_KBLOCKS)]
            + [pl.BlockSpec((B_HEADS, B_TQ, B_KBLOCKS * B_TQ), lambda t, s: (0, 0, 0))],
            out_specs=pl.BlockSpec((B_TQ, B_WIDTH), lambda t, s: (t, 0)),
        ),
        out_shape=jax.ShapeDtypeStruct((m, B_WIDTH), jnp.bfloat16),
        compiler_params=_params(("parallel",)),
        name="attn_b",
    )(sinks, qkv, *([kvb16] * (2 * B_KBLOCKS)), bias)


def _pad_top(x, t_new):
    return jnp.concatenate([jnp.zeros((LANES - t_new, x.shape[1]), x.dtype), x], axis=0)


def _attn_sample_kernel(qa_ref, qb_ref, kan_ref, van_ref, kbn_ref, vbn_ref,
                        cak_ref, cav_ref, cbk_ref, cbv_ref,
                        bias_ac_ref, bias_an_ref, bias_bc_ref, bias_bn_ref, sink_ref,
                        oa_ref, ob_ref, *, t_new):
    heads = A_HEADS
    rows = heads * t_new
    width = heads * HEAD_DIM
    bf16 = jnp.bfloat16
    row_head = lax.broadcasted_iota(jnp.int32, (rows, width), 0) // t_new
    lane_head = lax.broadcasted_iota(jnp.int32, (rows, width), 1) // HEAD_DIM
    own = row_head == lane_head

    def stack_q(q):
        return jnp.where(own, jnp.concatenate([q] * heads, axis=0), 0)

    def nt_dot(a, b):
        return lax.dot_general(a, b, (((1,), (1,)), ((), ())),
                               preferred_element_type=jnp.float32)

    def finish(s_c, s_n, vt_c, v_n, sink):
        mx = jnp.maximum(jnp.max(s_c, axis=-1, keepdims=True),
                         jnp.max(s_n, axis=-1, keepdims=True))
        if sink is not None:
            mx = jnp.maximum(mx, sink)
        e_c = jnp.exp2(s_c - mx)
        e_n = jnp.exp2(s_n - mx)
        denom = jnp.sum(e_c, axis=-1, keepdims=True) + jnp.sum(e_n, axis=-1, keepdims=True)
        if sink is not None:
            denom = denom + jnp.exp2(sink - mx)
        o = nt_dot(e_c.astype(bf16), vt_c)
        o = o + jnp.dot(e_n.astype(bf16), v_n, preferred_element_type=jnp.float32)
        o = jnp.where(own, o / denom, 0.0)
        return jnp.sum(o.reshape(heads, t_new, width), axis=0)

    k_n = _pad_top(kan_ref[...], t_new).astype(bf16)
    v_n = _pad_top(van_ref[...], t_new).astype(bf16)
    qs = stack_q(qa_ref[...])
    s_c = jnp.dot(qs, cak_ref[...].astype(bf16),
                  preferred_element_type=jnp.float32) + bias_ac_ref[...]
    s_n = nt_dot(qs, k_n) + bias_an_ref[...]
    oa_ref[...] = finish(s_c, s_n, cav_ref[...].astype(bf16), v_n, None).astype(oa_ref.dtype)

    def expansion(shape, src_axis):
        src = lax.broadcasted_iota(jnp.int32, shape, src_axis)
        dst = lax.broadcasted_iota(jnp.int32, shape, 1 - src_axis)
        return ((src % HEAD_DIM == dst % HEAD_DIM)
                & (src // HEAD_DIM == dst // (B_GROUP * HEAD_DIM))).astype(bf16)

    def widen(x):
        return jnp.dot(x.astype(bf16), expansion((B_KV_WIDTH, width), 0),
                       preferred_element_type=jnp.float32).astype(bf16)

    def widen_t(xt):
        return jnp.dot(expansion((width, B_KV_WIDTH), 1), xt.astype(bf16),
                       preferred_element_type=jnp.float32).astype(bf16)

    k_n = widen(_pad_top(kbn_ref[...], t_new))
    v_n = widen(_pad_top(vbn_ref[...], t_new))
    qs = stack_q(qb_ref[...])
    s_c = jnp.dot(qs, widen_t(cbk_ref[...]),
                  preferred_element_type=jnp.float32) + bias_bc_ref[...]
    s_n = nt_dot(qs, k_n) + bias_bn_ref[...]
    ob_ref[...] = finish(s_c, s_n, widen_t(cbv_ref[...]), v_n, sink_ref[...]).astype(ob_ref.dtype)


def _attn_sample(qkv, kva32, kvb32, caches, l, bias_ac, bias_an, bias_bc, bias_bn,
                 sink_col, *, t_new):
    m = qkv.shape[0]
    cak, cav, cbk, cbv = caches
    a_len, b_len = cak.shape[-1], cbk.shape[-1]
    rows = A_HEADS * t_new
    const = lambda shape: pl.BlockSpec(shape, lambda b: (0,) * len(shape))
    cache_spec = lambda c: pl.BlockSpec((None, None) + c.shape[2:], lambda b: (l, b, 0, 0))
    return pl.pallas_call(
        functools.partial(_attn_sample_kernel, t_new=t_new),
        grid=(m // t_new,),
        in_specs=[
            pl.BlockSpec((t_new, A_WIDTH), lambda b: (b, QKV_QA)),
            pl.BlockSpec((t_new, B_WIDTH), lambda b: (b, QKV_QB)),
            pl.BlockSpec((t_new, A_WIDTH), lambda b: (b, 0)),
            pl.BlockSpec((t_new, A_WIDTH), lambda b: (b, 1)),
            pl.BlockSpec((t_new, B_KV_WIDTH), lambda b: (b, 0)),
            pl.BlockSpec((t_new, B_KV_WIDTH), lambda b: (b, 1)),
            cache_spec(cak), cache_spec(cav), cache_spec(cbk), cache_spec(cbv),
            const((rows, a_len)), const((rows, LANES)),
            const((rows, b_len)), const((rows, LANES)),
            const((rows, 1)),
        ],
        out_specs=[
            pl.BlockSpec((t_new, A_WIDTH), lambda b: (b, 0)),
            pl.BlockSpec((t_new, B_WIDTH), lambda b: (b, 0)),
        ],
        out_shape=[
            jax.ShapeDtypeStruct((m, A_WIDTH), jnp.bfloat16),
            jax.ShapeDtypeStruct((m, B_WIDTH), jnp.bfloat16),
        ],
        compiler_params=_params(("parallel",)),
        name="attn_sample",
    )(qkv, qkv, kva32, kva32, kvb32, kvb32, cak, cav, cbk, cbv,
      bias_ac, bias_an, bias_bc, bias_bn, sink_col)


def _roll_caches_kernel(kan_ref, van_ref, kbn_ref, vbn_ref,
                        cak_ref, cav_ref, cbk_ref, cbv_ref,
                        nak_ref, nav_ref, nbk_ref, nbv_ref, *, t_new):
    def roll_in(cache_ref, new_ref, out_ref):
        cache_t = cache_ref[...]
        n = cache_t.shape[1]
        shifted = pltpu.roll(cache_t, n - t_new, 1)
        lane = lax.broadcasted_iota(jnp.int32, (1, LANES), 1)
        tail = jnp.where(lane < LANES - t_new, shifted[:, n - LANES:],
                         _pad_top(new_ref[...], t_new).T)
        if n > LANES:
            out_ref[:, :n - LANES] = shifted[:, :n - LANES]
        out_ref[:, n - LANES:] = tail

    roll_in(cak_ref, kan_ref, nak_ref)
    roll_in(cav_ref, van_ref, nav_ref)
    roll_in(cbk_ref, kbn_ref, nbk_ref)
    roll_in(cbv_ref, vbn_ref, nbv_ref)


def _roll_caches(new_a, new_b, caches, *, t_new):
    depth, streams = caches[0].shape[:2]
    cache_spec = lambda c: pl.BlockSpec((None, None) + c.shape[2:], lambda l, b: (l, b, 0, 0))
    new_spec = lambda width, col: pl.BlockSpec((None, t_new, width), lambda l, b: (l, b, col))
    return pl.pallas_call(
        functools.partial(_roll_caches_kernel, t_new=t_new),
        grid=(depth, streams),
        in_specs=[new_spec(A_WIDTH, 0), new_spec(A_WIDTH, 1),
                  new_spec(B_KV_WIDTH, 0), new_spec(B_KV_WIDTH, 1)]
        + [cache_spec(c) for c in caches],
        out_specs=[cache_spec(c) for c in caches],
        out_shape=[jax.ShapeDtypeStruct(c.shape, jnp.float32) for c in caches],
        compiler_params=_params(("parallel", "parallel")),
        name="roll_caches",
    )(new_a, new_a, new_b, new_b, *caches)


def _prompt_caches_kernel(*refs, depth):
    in_refs, (ak_ref, av_ref, bk_ref, bv_ref) = refs[:4 * depth], refs[4 * depth:]
    for k in range(depth):
        @pl.when(pl.program_id(0) == k)
        def _(k=k):
            ka_ref, va_ref, kb_ref, vb_ref = in_refs[4 * k:4 * k + 4]
            ak_ref[...] = ka_ref[...].T
            av_ref[...] = va_ref[...].T
            bk_ref[...] = kb_ref[...].T
            bv_ref[...] = vb_ref[...].T


def _prompt_caches(kva32s, kvb32s, *, batch, seq, na, nb):
    depth = len(kva32s)

    def rows_map(k, tile, col):
        def index_map(l, b):
            bb = jnp.where(l < k, 0, jnp.where(l > k, batch - 1, b))
            return ((bb + 1) * tile - 1, col)
        return index_map

    in_specs, args = [], []
    for k in range(depth):
        in_specs += [
            pl.BlockSpec((na, A_WIDTH), rows_map(k, seq // na, 0)),
            pl.BlockSpec((na, A_WIDTH), rows_map(k, seq // na, 1)),
            pl.BlockSpec((nb, B_KV_WIDTH), rows_map(k, seq // nb, 0)),
            pl.BlockSpec((nb, B_KV_WIDTH), rows_map(k, seq // nb, 1)),
        ]
        args += [kva32s[k], kva32s[k], kvb32s[k], kvb32s[k]]
    out_shapes = [(depth, batch, A_WIDTH, na)] * 2 + [(depth, batch, B_KV_WIDTH, nb)] * 2
    return pl.pallas_call(
        functools.partial(_prompt_caches_kernel, depth=depth),
        grid=(depth, batch),
        in_specs=in_specs,
        out_specs=[pl.BlockSpec((None, None) + s[2:], lambda l, b: (l, b, 0, 0))
                   for s in out_shapes],
        out_shape=[jax.ShapeDtypeStruct(s, jnp.float32) for s in out_shapes],
        compiler_params=_params(("arbitrary", "arbitrary")),
        name="prompt_caches",
    )(*args)


def _merge_kernel(oa_ref, ob_ref, ga_ref, gb_ref, x_ref, wa_ref, wb_ref, wo_ref,
                  g_ref, y_ref):
    ta = jnp.dot(oa_ref[...], wa_ref[...], preferred_element_type=jnp.float32)
    tb = jnp.dot(ob_ref[...], wb_ref[...], preferred_element_type=jnp.float32)
    mixed = (jax.nn.sigmoid(ga_ref[...].astype(jnp.float32)) * ta
             + jax.nn.sigmoid(gb_ref[...].astype(jnp.float32)) * tb)
    z = jnp.dot(mixed.astype(jnp.bfloat16), wo_ref[...],
                preferred_element_type=jnp.float32)
    y_ref[...] = x_ref[...] + _rms_scale(z, g_ref[...])


def _merge(oa, ob, gates, x, wa, wb, wo, g, l, *, tm):
    m, d = x.shape
    resident = lambda w: pl.BlockSpec(w.shape, lambda i: (0, 0),
                                      pipeline_mode=pl.Buffered(1))
    return pl.pallas_call(
        _merge_kernel,
        grid=(m // tm,),
        in_specs=[
            pl.BlockSpec((tm, A_WIDTH), lambda i: (i, 0)),
            pl.BlockSpec((tm, B_WIDTH), lambda i: (i, 0)),
            pl.BlockSpec((tm, d), lambda i: (i, 0)),
            pl.BlockSpec((tm, d), lambda i: (i, 1)),
            pl.BlockSpec((tm, d), lambda i: (i, 0)),
            resident(wa), resident(wb), resident(wo),
            _layer_vec_spec(l, d),
        ],
        out_specs=pl.BlockSpec((tm, d), lambda i: (i, 0)),
        out_shape=jax.ShapeDtypeStruct((m, d), jnp.float32),
        compiler_params=_params(("parallel",)),
        name="merge",
    )(oa, ob, gates, gates, x, wa, wb, wo, g)


def _ffn_kernel(x_ref, gpre_ref, wu_ref, wd_ref, gpost_ref, *refs):
    f = pl.program_id(1)
    if len(refs) == 4:
        w32_ref, y_ref, w16_ref, h_ref = refs
        slab = w32_ref[...].astype(w16_ref.dtype)
        for c in range(w16_ref.shape[0]):
            w16_ref[c] = slab[:, c * MXU_COLS:(c + 1) * MXU_COLS]
    else:
        y_ref, h_ref = refs

    @pl.when(f == 0)
    def _():
        h_ref[...] = _rms_scale(x_ref[...], gpre_ref[...]).astype(h_ref.dtype)
        y_ref[...] = jnp.zeros_like(y_ref)

    u = jnp.dot(h_ref[...], wu_ref[...], preferred_element_type=jnp.float32)
    u = jnp.square(jnp.maximum(u, 0.0)).astype(jnp.bfloat16)
    y_ref[...] += jnp.dot(u, wd_ref[...], preferred_element_type=jnp.float32)

    @pl.when(f == pl.num_programs(1) - 1)
    def _():
        y_ref[...] = x_ref[...] + _rms_scale(y_ref[...], gpost_ref[...])


def _ffn(x, gpre, wu, wd, gpost, l, cast=None, *, tm, tf):
    m, d = x.shape
    dff = wu.shape[-1]
    n_f = dff // tf
    in_specs = [
        pl.BlockSpec((tm, d), lambda i, f: (i, 0)),
        _layer_vec_spec(l, d),
        pl.BlockSpec((d, tf), lambda i, f: (0, f)),
        pl.BlockSpec((tf, d), lambda i, f: (f, 0)),
        _layer_vec_spec(l, d),
    ]
    out_specs = [pl.BlockSpec((tm, d), lambda i, f: (i, 0))]
    out_shape = [jax.ShapeDtypeStruct((m, d), jnp.float32)]
    args = [x, gpre, wu, wd, gpost]
    if cast is not None:
        w, lc = cast
        rows = w.shape[1] // ((m // tm) * n_f)
        in_specs.append(pl.BlockSpec((None, rows, w.shape[2]),
                                     lambda i, f: (lc, i * n_f + f, 0)))
        n_blocks = w.shape[2] // MXU_COLS
        out_specs.append(pl.BlockSpec((n_blocks, rows, MXU_COLS),
                                      lambda i, f: (0, i * n_f + f, 0)))
        out_shape.append(jax.ShapeDtypeStruct((n_blocks, w.shape[1], MXU_COLS), jnp.bfloat16))
        args.append(w)
    return pl.pallas_call(
        _ffn_kernel,
        grid=(m // tm, n_f),
        in_specs=in_specs,
        out_specs=out_specs,
        out_shape=out_shape,
        scratch_shapes=[pltpu.VMEM((tm, d), jnp.bfloat16)],
        compiler_params=_params(("arbitrary", "arbitrary")),
        name="ffn",
    )(*args)


def _t5_bucket(rel):
    half = T5_BUCKETS // 2
    exact = half // 2
    ret = jnp.where(rel > 0, half, 0)
    n = jnp.abs(rel)
    large = exact + (jnp.log(jnp.maximum(n, 1).astype(jnp.float32) / exact)
                     / math.log(T5_MAX_DIST / exact) * (half - exact)).astype(jnp.int32)
    large = jnp.minimum(large, half - 1)
    return ret + jnp.where(n < exact, n, large)


def _a_bias_of_rel(table):
    scaled = table.T * LOG2E
    return lambda rel: scaled[:, jnp.clip(rel, -A_REL_CLIP, A_REL_CLIP) + A_REL_CLIP]


def _b_bias_of_rel(table):
    scaled = table.T * LOG2E
    return lambda rel: scaled[:, _t5_bucket(-rel)]


def _hankel(u, q, n):
    heads, k = u.shape
    period = q + n
    u = jnp.pad(u, ((0, 0), (0, period - k)))
    flat = jnp.tile(u, (1, q + 1))[:, :q * (period + 1)]
    return flat.reshape(heads, q, period + 1)[:, :, :n]


def _rel_bias(bias_of_rel, q_len, n_keys, k0):
    k = jnp.arange(q_len + n_keys - 1)
    u = bias_of_rel(k - (n_keys - 1) - k0)
    return _hankel(u, q_len, n_keys)[:, :, ::-1]


def _band_tile(bias_of_rel, tq, n_prev):
    window = tq + n_prev * CHUNK
    band = _rel_bias(bias_of_rel, CHUNK, (n_prev + 1) * CHUNK, -n_prev * CHUNK)
    blocks = []
    for c in range(tq // CHUNK):
        left = c * CHUNK
        right = window - left - band.shape[-1]
        blocks.append(jnp.pad(band, ((0, 0), (0, 0), (left, right)),
                              constant_values=NEG_INF))
    return jnp.concatenate(blocks, axis=1)


def _sample_bias(bias_of_rel, t_new, n_cache):
    full = _rel_bias(bias_of_rel, t_new, n_cache + t_new, -n_cache)
    full = full.reshape(-1, n_cache + t_new)
    new = jnp.pad(full[:, n_cache:], ((0, 0), (LANES - t_new, 0)),
                  constant_values=NEG_INF)
    return full[:, :n_cache], new


def _cache_t(c):
    d, s, r, h, e = c.shape
    return jnp.transpose(c, (0, 1, 3, 4, 2)).reshape(d, s, h * e, r)


def _cache_untranspose(c, heads):
    d, s, w, r = c.shape
    return jnp.transpose(c.reshape(d, s, heads, w // heads, r), (0, 1, 4, 2, 3))


def kernel(x_prompt, x_sample, cache_a_k, cache_a_v, cache_b_k, cache_b_v, w_in,
           w_a_out, w_b_out, w_out, a_rel_table, t5_table, b_sinks, g_mix_pre,
           g_mix_post, g_ffn_pre, g_ffn_post, w_up, w_down):
    depth = w_in.shape[0]
    batch, seq, d = x_prompt.shape
    dec_batch, t_new, _ = x_sample.shape
    a_len = cache_a_k.shape[2]
    b_len = cache_b_k.shape[2]
    bf16 = jnp.bfloat16
    mp = batch * seq
    ms = dec_batch * t_new

    yp = x_prompt.reshape(mp, d)
    ys = x_sample.reshape(ms, d)

    w_in16 = jnp.transpose(
        w_in[0].astype(bf16).reshape(d, w_in.shape[2] // MXU_COLS, MXU_COLS), (1, 0, 2))
    later_weights = (w_a_out, w_b_out, w_out, w_up, w_down)
    g_pre = g_mix_pre[:, None, :]
    g_post = g_mix_post[:, None, :]
    gf_pre = g_ffn_pre[:, None, :]
    gf_post = g_ffn_post[:, None, :]
    caches = tuple(_cache_t(c) for c in (cache_a_k, cache_a_v, cache_b_k, cache_b_v))

    b_of_rel = _b_bias_of_rel(t5_table)
    bias_b_tile = _band_tile(b_of_rel, B_TQ, B_LEFT_CHUNKS)
    bias_bc, bias_bn = _sample_bias(b_of_rel, t_new, b_len)

    prompt_kva, prompt_kvb, sample_kva, sample_kvb = [], [], [], []
    for l in range(depth):
        a_of_rel = _a_bias_of_rel(a_rel_table[l])
        sinks = b_sinks[l] * LOG2E

        qkv, kvb16, gates, kva32, kvb32 = _in_proj(yp, g_pre, w_in16, l, tm=1024)
        oa, wa16, wb16, wo16, wu16, wd16 = _attn_a(
            qkv, _band_tile(a_of_rel, A_TQ, A_LEFT_CHUNKS), later_weights, l, m=mp, seq=seq)
        ob = _attn_b(qkv, kvb16, bias_b_tile, sinks, m=mp, seq=seq)
        yp = _merge(oa, ob, gates, yp, wa16, wb16, wo16, g_post, l, tm=256)
        w_in16_this = w_in16
        if l + 1 < depth:
            yp, w_in16 = _ffn(yp, gf_pre, wu16, wd16, gf_post, l, (w_in, l + 1),
                              tm=512, tf=1024)
        else:
            yp, = _ffn(yp, gf_pre, wu16, wd16, gf_post, l, tm=512, tf=1024)
        prompt_kva.append(kva32)
        prompt_kvb.append(kvb32)

        qkv, _, gates, kva32, kvb32 = _in_proj(ys, g_pre, w_in16_this, l, tm=ms)
        bias_ac, bias_an = _sample_bias(a_of_rel, t_new, a_len)
        sink_col = jnp.repeat(sinks, t_new)[:, None]
        oa, ob = _attn_sample(qkv, kva32, kvb32, caches, l, bias_ac, bias_an, bias_bc,
                              bias_bn, sink_col, t_new=t_new)
        ys = _merge(oa, ob, gates, ys, wa16, wb16, wo16, g_post, l, tm=ms)
        ys, = _ffn(ys, gf_pre, wu16, wd16, gf_post, l, tm=ms, tf=1024)
        sample_kva.append(kva32)
        sample_kvb.append(kvb32)

    akp, avp, bkp, bvp = _prompt_caches(prompt_kva, prompt_kvb, batch=batch, seq=seq,
                                        na=min(a_len, seq), nb=min(b_len, seq))
    aks, avs, bks, bvs = _roll_caches(jnp.stack(sample_kva), jnp.stack(sample_kvb),
                                      caches, t_new=t_new)
    return (yp.reshape(batch, seq, d), ys.reshape(dec_batch, t_new, d),
            _cache_untranspose(akp, A_HEADS), _cache_untranspose(avp, A_HEADS),
            _cache_untranspose(bkp, B_KV_HEADS), _cache_untranspose(bvp, B_KV_HEADS),
            _cache_untranspose(aks, A_HEADS), _cache_untranspose(avs, A_HEADS),
            _cache_untranspose(bks, B_KV_HEADS), _cache_untranspose(bvs, B_KV_HEADS))
```

```python
import functools
import math

import jax
import jax.numpy as jnp
from jax import lax
from jax.experimental import pallas as pl
from jax.experimental.pallas import tpu as pltpu

D_MODEL = 2048
CHUNK = 64
HEAD_DIM = 64
A_HEADS = 16
A_WIDTH = A_HEADS * HEAD_DIM
A_LEFT_CHUNKS = 8
A_REL_CLIP = 256
B_HEADS = 16
B_KV_HEADS = 2
B_GROUP = B_HEADS // B_KV_HEADS
B_WIDTH = B_HEADS * HEAD_DIM
B_KV_WIDTH = B_KV_HEADS * HEAD_DIM
B_LEFT_CHUNKS = 2
T5_BUCKETS = 32
T5_MAX_DIST = 128
EPS = 1e-6
NEG_INF = -1e30
LOG2E = math.log2(math.e)
Q_SCALE = HEAD_DIM ** -0.5 * LOG2E

LANES = 128
MXU_COLS = 256

COL_QA = 0
COL_KA = A_WIDTH
COL_VA = 2 * A_WIDTH
COL_QB = 3 * A_WIDTH
COL_KB = 3 * A_WIDTH + B_WIDTH
COL_GA = COL_KB + 2 * B_KV_WIDTH
PROJ_TN = 4 * MXU_COLS
QKV_KA, QKV_VA, QKV_QA, QKV_QB = 0, 1, 2, 3

A_TQ = 256
B_TQ = 128
A_KBLOCKS = A_LEFT_CHUNKS * CHUNK // A_TQ + 1
B_KBLOCKS = B_LEFT_CHUNKS * CHUNK // B_TQ + 1

VMEM_LIMIT = 56 * 1024 * 1024


def _params(sem, vmem=VMEM_LIMIT):
    return pltpu.CompilerParams(dimension_semantics=sem, vmem_limit_bytes=vmem)


def _rms_scale(x, g):
    return x * lax.rsqrt(jnp.mean(x * x, axis=-1, keepdims=True) + EPS) * g


def _layer_vec_spec(l, d):
    return pl.BlockSpec((None, 1, d), lambda *_: (l, 0, 0))


PROJ_KV_STEPS = 2
PROJ_Q_STEPS = 4
PROJ_KVB_STEP = 4
PROJ_GATE_STEP0 = 5


def _in_proj_kernel(x_ref, g_ref, *refs):
    n_w = PROJ_TN // MXU_COLS
    w_refs = refs[:n_w]
    qkv_ref, kvb16_ref, gates_ref, kva32_ref, kvb32_ref, h_ref = refs[n_w:]
    j = pl.program_id(1)

    @pl.when(j == 0)
    def _():
        h_ref[...] = _rms_scale(x_ref[...], g_ref[...]).astype(h_ref.dtype)

    def tile(out16_ref, out32_ref, scale=None):
        h = h_ref[...]
        for k, w_ref in enumerate(w_refs):
            cols = slice(k * MXU_COLS, (k + 1) * MXU_COLS)
            acc = jnp.dot(h, w_ref[...], preferred_element_type=jnp.float32)
            if out32_ref is not None:
                out32_ref[:, cols] = acc
            if scale is not None:
                acc = acc * scale
            out16_ref[:, cols] = acc.astype(out16_ref.dtype)

    @pl.when(j < PROJ_KV_STEPS)
    def _():
        tile(qkv_ref, kva32_ref)

    @pl.when((j >= PROJ_KV_STEPS) & (j < PROJ_Q_STEPS))
    def _():
        tile(qkv_ref, None, Q_SCALE)

    @pl.when(j == PROJ_KVB_STEP)
    def _():
        acc = jnp.dot(h_ref[...], w_refs[0][...], preferred_element_type=jnp.float32)
        kvb16_ref[...] = acc.astype(kvb16_ref.dtype)
        kvb32_ref[...] = acc

    @pl.when(j >= PROJ_GATE_STEP0)
    def _():
        tile(gates_ref, None)


def _in_proj(x, g, w_in, l, *, tm):
    m, d = x.shape
    n_w = PROJ_TN // MXU_COLS
    n_gate = 2 * D_MODEL // PROJ_TN
    src = [c // MXU_COLS for c in (COL_KA, COL_VA, COL_QA, COL_QB, COL_KB)]
    ga0 = COL_GA // MXU_COLS

    def w_map(k):
        def index_map(i, j):
            base = src[PROJ_KVB_STEP] if k == 0 else src[PROJ_KVB_STEP - 1] + k
            for step in range(PROJ_KVB_STEP - 1, -1, -1):
                base = jnp.where(j == step, src[step] + k, base)
            gate = ga0 + (j - PROJ_GATE_STEP0) * n_w + k
            return (0, jnp.where(j >= PROJ_GATE_STEP0, gate, base))
        return index_map

    kvb_n = 2 * B_KV_WIDTH
    return pl.pallas_call(
        _in_proj_kernel,
        grid=(m // tm, PROJ_GATE_STEP0 + n_gate),
        in_specs=[
            pl.BlockSpec((tm, d), lambda i, j: (i, 0)),
            _layer_vec_spec(l, d),
        ] + [pl.BlockSpec((d, MXU_COLS), w_map(k)) for k in range(n_w)],
        out_specs=[
            pl.BlockSpec((tm, PROJ_TN), lambda i, j: (i, jnp.minimum(j, PROJ_Q_STEPS - 1))),
            pl.BlockSpec((tm, kvb_n), lambda i, j: (i, 0)),
            pl.BlockSpec((tm, PROJ_TN),
                         lambda i, j: (i, jnp.clip(j - PROJ_GATE_STEP0, 0, n_gate - 1))),
            pl.BlockSpec((tm, PROJ_TN), lambda i, j: (i, jnp.minimum(j, PROJ_KV_STEPS - 1))),
            pl.BlockSpec((tm, kvb_n), lambda i, j: (i, 0)),
        ],
        out_shape=[
            jax.ShapeDtypeStruct((m, PROJ_Q_STEPS * PROJ_TN), jnp.bfloat16),
            jax.ShapeDtypeStruct((m, kvb_n), jnp.bfloat16),
            jax.ShapeDtypeStruct((m, 2 * D_MODEL), jnp.bfloat16),
            jax.ShapeDtypeStruct((m, PROJ_KV_STEPS * PROJ_TN), jnp.float32),
            jax.ShapeDtypeStruct((m, kvb_n), jnp.float32),
        ],
        scratch_shapes=[pltpu.VMEM((tm, d), jnp.bfloat16)],
        compiler_params=_params(("parallel", "arbitrary")),
        name="in_proj",
    )(x, g, *([w_in] * n_w))


def _low_half():
    return lax.broadcasted_iota(jnp.int32, (1, LANES), 1) < HEAD_DIM


def _start_mask(tile_in_seq, n_kblocks, tq):
    lane = lax.broadcasted_iota(jnp.int32, (1, n_kblocks * tq), 1)
    first_valid = jnp.maximum(n_kblocks - 1 - tile_in_seq, 0) * tq
    return jnp.where(lane >= first_valid, 0.0, NEG_INF)


def _attn_a_kernel(q_ref, *refs, tiles_per_seq):
    k_refs = refs[:A_KBLOCKS]
    v_refs = refs[A_KBLOCKS:2 * A_KBLOCKS]
    bias_ref = refs[2 * A_KBLOCKS]
    n_cast = (len(refs) - 2 * A_KBLOCKS - 2) // 2
    w32_refs = refs[2 * A_KBLOCKS + 1:2 * A_KBLOCKS + 1 + n_cast]
    o_ref = refs[2 * A_KBLOCKS + 1 + n_cast]
    w16_refs = refs[2 * A_KBLOCKS + 2 + n_cast:]
    tile_in_seq = pl.program_id(0) % tiles_per_seq
    low = _low_half()

    def heads(start):
        for w32_ref, w16_ref in zip(w32_refs, w16_refs):
            w16_ref[...] = w32_ref[...].astype(w16_ref.dtype)
        for p in range(A_HEADS // 2):
            sl = slice(p * LANES, (p + 1) * LANES)
            q2 = q_ref[:, sl]
            k2 = jnp.concatenate([r[:, sl] for r in k_refs], axis=0)
            v2 = jnp.concatenate([r[:, sl] for r in v_refs], axis=0)
            qs = jnp.concatenate([jnp.where(low, q2, 0), jnp.where(low, 0, q2)], axis=0)
            s_all = lax.dot_general(qs, k2, (((1,), (1,)), ((), ())),
                                    preferred_element_type=jnp.float32)
            probs = []
            for hh in range(2):
                s = s_all[hh * A_TQ:(hh + 1) * A_TQ] + bias_ref[2 * p + hh]
                if start is not None:
                    s = s + start
                s = s.astype(v2.dtype)
                probs.append(jnp.exp2(s - jnp.max(s, axis=-1, keepdims=True)))
            v_ones = jnp.concatenate([v2, jnp.ones_like(v2)], axis=1)
            r = jnp.dot(jnp.concatenate(probs, axis=0), v_ones,
                        preferred_element_type=jnp.float32)
            out = r[:, :LANES] / r[:, LANES:]
            o_ref[:, sl] = jnp.where(low, out[:A_TQ], out[A_TQ:]).astype(o_ref.dtype)

    @pl.when(tile_in_seq >= A_KBLOCKS - 1)
    def _():
        heads(None)

    @pl.when(tile_in_seq < A_KBLOCKS - 1)
    def _():
        heads(_start_mask(tile_in_seq, A_KBLOCKS, A_TQ))


def _band_kv_map(j, col, n_kblocks, tiles_per_seq):
    def index_map(t, *_):
        i = t % tiles_per_seq
        return (t - i + jnp.maximum(i - (n_kblocks - 1) + j, 0), col)
    return index_map


def _attn_a(qkv, bias, weights, l, *, m, seq):
    tiles_per_seq = seq // A_TQ
    steps = m // A_TQ
    kv_spec = lambda j, col: pl.BlockSpec(
        (A_TQ, A_WIDTH), _band_kv_map(j, col, A_KBLOCKS, tiles_per_seq))
    slab32 = lambda w: pl.BlockSpec((None, w.shape[1] // steps, w.shape[2]),
                                    lambda t: (l, t, 0))
    slab16 = lambda w: pl.BlockSpec((w.shape[1] // steps, w.shape[2]), lambda t: (t, 0))
    return pl.pallas_call(
        functools.partial(_attn_a_kernel, tiles_per_seq=tiles_per_seq),
        grid=(steps,),
        in_specs=[pl.BlockSpec((A_TQ, A_WIDTH), lambda t: (t, QKV_QA))]
        + [kv_spec(j, QKV_KA) for j in range(A_KBLOCKS)]
        + [kv_spec(j, QKV_VA) for j in range(A_KBLOCKS)]
        + [pl.BlockSpec((A_HEADS, A_TQ, A_KBLOCKS * A_TQ), lambda t: (0, 0, 0),
                        pipeline_mode=pl.Buffered(1))]
        + [slab32(w) for w in weights],
        out_specs=[pl.BlockSpec((A_TQ, A_WIDTH), lambda t: (t, 0))]
        + [slab16(w) for w in weights],
        out_shape=[jax.ShapeDtypeStruct((m, A_WIDTH), jnp.bfloat16)]
        + [jax.ShapeDtypeStruct(w.shape[1:], jnp.bfloat16) for w in weights],
        compiler_params=_params(("parallel",)),
        name="attn_a",
    )(qkv, *([qkv] * (2 * A_KBLOCKS)), bias, *weights)


def _dup_half(x, g):
    swapped = pltpu.roll(x, HEAD_DIM, 1)
    low = _low_half()
    return jnp.where(low, x, swapped) if g == 0 else jnp.where(low, swapped, x)


def _attn_b_kernel(sink_ref, q_ref, *refs, tiles_per_seq):
    k_refs = refs[:B_KBLOCKS]
    v_refs = refs[B_KBLOCKS:2 * B_KBLOCKS]
    bias_ref, o_ref = refs[2 * B_KBLOCKS:]
    start = _start_mask(pl.program_id(0) % tiles_per_seq, B_KBLOCKS, B_TQ)
    low = _low_half()
    k2 = jnp.concatenate([r[...] for r in k_refs], axis=0).astype(jnp.float32)
    v2 = jnp.concatenate([r[...] for r in v_refs], axis=0).astype(jnp.float32)
    pairs = B_GROUP // 2
    for g in range(B_KV_HEADS):
        kd = _dup_half(k2, g).astype(jnp.bfloat16)
        vd = _dup_half(v2, g).astype(jnp.bfloat16)
        stacked = []
        for p in range(pairs):
            c0 = (g * pairs + p) * LANES
            q2 = q_ref[:, c0:c0 + LANES]
            stacked.append(jnp.where(low, q2, 0))
            stacked.append(jnp.where(low, 0, q2))
        qs = jnp.concatenate(stacked, axis=0)
        s_all = lax.dot_general(qs, kd, (((1,), (1,)), ((), ())),
                                preferred_element_type=jnp.float32)
        probs, sink_terms = [], []
        for hl in range(B_GROUP):
            h = g * B_GROUP + hl
            s = s_all[hl * B_TQ:(hl + 1) * B_TQ] + bias_ref[h] + start
            sink = sink_ref[h]
            mx = jnp.maximum(jnp.max(s, axis=-1, keepdims=True), sink)
            probs.append(jnp.exp2(s - mx).astype(jnp.bfloat16))
            sink_terms.append(jnp.exp2(sink - mx))
        v_ones = jnp.concatenate([vd, jnp.ones_like(vd)], axis=1)
        r = jnp.dot(jnp.concatenate(probs, axis=0), v_ones,
                    preferred_element_type=jnp.float32)
        outs = []
        for hl in range(B_GROUP):
            rh = r[hl * B_TQ:(hl + 1) * B_TQ]
            outs.append(rh[:, :LANES] / (rh[:, LANES:] + sink_terms[hl]))
        for p in range(pairs):
            c0 = (g * pairs + p) * LANES
            o_ref[:, c0:c0 + LANES] = jnp.where(
                low, outs[2 * p], outs[2 * p + 1]).astype(o_ref.dtype)


def _attn_b(qkv, kvb16, bias, sinks, *, m, seq):
    tiles_per_seq = seq // B_TQ
    kv_spec = lambda j, col: pl.BlockSpec(
        (B_TQ, LANES), _band_kv_map(j, col, B_KBLOCKS, tiles_per_seq))
    return pl.pallas_call(
        functools.partial(_attn_b_kernel, tiles_per_seq=tiles_per_seq),
        grid_spec=pltpu.PrefetchScalarGridSpec(
            num_scalar_prefetch=1,
            grid=(m // B_TQ,),
            in_specs=[pl.BlockSpec((B_TQ, B_WIDTH), lambda t, s: (t, QKV_QB))]
            + [kv_spec(j, 0) for j in range(B_KBLOCKS)]
            + [kv_spec(j, 1) for j in range(B_KBLOCKS)]
            + [pl.BlockSpec((B_HEADS, B_TQ, B_KBLOCKS * B_TQ), lambda t, s: (0, 0, 0))],
            out_specs=pl.BlockSpec((B_TQ, B_WIDTH), lambda t, s: (t, 0)),
        ),
        out_shape=jax.ShapeDtypeStruct((m, B_WIDTH), jnp.bfloat16),
        compiler_params=_params(("parallel",)),
        name="attn_b",
    )(sinks, qkv, *([kvb16] * (2 * B_KBLOCKS)), bias)


def _pad_top(x, t_new):
    return jnp.concatenate([jnp.zeros((LANES - t_new, x.shape[1]), x.dtype), x], axis=0)


def _roll_in(cache_t, new_rows, out_ref, layer, t_new):
    n = cache_t.shape[1]
    shifted = pltpu.roll(cache_t, n - t_new, 1)
    lane = lax.broadcasted_iota(jnp.int32, (1, LANES), 1)
    tail = jnp.where(lane < LANES - t_new, shifted[:, n - LANES:],
                     _pad_top(new_rows, t_new).T)
    if n > LANES:
        out_ref[layer, :, :n - LANES] = shifted[:, :n - LANES]
    out_ref[layer, :, n - LANES:] = tail


def _attn_sample_kernel(qa_ref, qb_ref, kan_ref, van_ref, kbn_ref, vbn_ref,
                        cak_ref, cav_ref, cbk_ref, cbv_ref,
                        bias_ac_ref, bias_an_ref, bias_bc_ref, bias_bn_ref, sink_ref,
                        *refs, t_new, layer, rolling):
    if rolling:
        prev_refs, (oa_ref, ob_ref), out_refs = refs[:4], refs[4:6], refs[6:]
        for cache_ref, new_ref, prev_ref, out_ref in zip(
                (cak_ref, cav_ref, cbk_ref, cbv_ref), (kan_ref, van_ref, kbn_ref, vbn_ref),
                prev_refs, out_refs):
            for l2 in range(layer + 1):
                rows_new = new_ref[...] if l2 == layer else prev_ref[l2]
                _roll_in(cache_ref[l2], rows_new, out_ref, l2, t_new)
        cak_ref, cav_ref, cbk_ref, cbv_ref = (
            r.at[layer] for r in (cak_ref, cav_ref, cbk_ref, cbv_ref))
    else:
        oa_ref, ob_ref = refs
    heads = A_HEADS
    rows = heads * t_new
    width = heads * HEAD_DIM
    bf16 = jnp.bfloat16
    row_head = lax.broadcasted_iota(jnp.int32, (rows, width), 0) // t_new
    lane_head = lax.broadcasted_iota(jnp.int32, (rows, width), 1) // HEAD_DIM
    own = row_head == lane_head

    def stack_q(q):
        return jnp.where(own, jnp.concatenate([q] * heads, axis=0), 0)

    def nt_dot(a, b):
        return lax.dot_general(a, b, (((1,), (1,)), ((), ())),
                               preferred_element_type=jnp.float32)

    def finish(s_c, s_n, vt_c, v_n, sink):
        mx = jnp.maximum(jnp.max(s_c, axis=-1, keepdims=True),
                         jnp.max(s_n, axis=-1, keepdims=True))
        if sink is not None:
            mx = jnp.maximum(mx, sink)
        e_c = jnp.exp2(s_c - mx)
        e_n = jnp.exp2(s_n - mx)
        denom = jnp.sum(e_c, axis=-1, keepdims=True) + jnp.sum(e_n, axis=-1, keepdims=True)
        if sink is not None:
            denom = denom + jnp.exp2(sink - mx)
        o = nt_dot(e_c.astype(bf16), vt_c)
        o = o + jnp.dot(e_n.astype(bf16), v_n, preferred_element_type=jnp.float32)
        o = jnp.where(own, o / denom, 0.0)
        return jnp.sum(o.reshape(heads, t_new, width), axis=0)

    k_n = _pad_top(kan_ref[...], t_new).astype(bf16)
    v_n = _pad_top(van_ref[...], t_new).astype(bf16)
    qs = stack_q(qa_ref[...])
    s_c = jnp.dot(qs, cak_ref[...].astype(bf16),
                  preferred_element_type=jnp.float32) + bias_ac_ref[...]
    s_n = nt_dot(qs, k_n) + bias_an_ref[...]
    oa_ref[...] = finish(s_c, s_n, cav_ref[...].astype(bf16), v_n, None).astype(oa_ref.dtype)

    def expansion(shape, src_axis):
        src = lax.broadcasted_iota(jnp.int32, shape, src_axis)
        dst = lax.broadcasted_iota(jnp.int32, shape, 1 - src_axis)
        return ((src % HEAD_DIM == dst % HEAD_DIM)
                & (src // HEAD_DIM == dst // (B_GROUP * HEAD_DIM))).astype(bf16)

    def widen(x):
        return jnp.dot(x.astype(bf16), expansion((B_KV_WIDTH, width), 0),
                       preferred_element_type=jnp.float32).astype(bf16)

    def widen_t(xt):
        return jnp.dot(expansion((width, B_KV_WIDTH), 1), xt.astype(bf16),
                       preferred_element_type=jnp.float32).astype(bf16)

    k_n = widen(_pad_top(kbn_ref[...], t_new))
    v_n = widen(_pad_top(vbn_ref[...], t_new))
    qs = stack_q(qb_ref[...])
    s_c = jnp.dot(qs, widen_t(cbk_ref[...]),
                  preferred_element_type=jnp.float32) + bias_bc_ref[...]
    s_n = nt_dot(qs, k_n) + bias_bn_ref[...]
    ob_ref[...] = finish(s_c, s_n, widen_t(cbv_ref[...]), v_n, sink_ref[...]).astype(ob_ref.dtype)


def _attn_sample(qkv, kva32, kvb32, caches, l, bias_ac, bias_an, bias_bc, bias_bn,
                 sink_col, prev_new=None, *, t_new):
    m = qkv.shape[0]
    cak, cav, cbk, cbv = caches
    a_len, b_len = cak.shape[-1], cbk.shape[-1]
    rows = A_HEADS * t_new
    rolling = prev_new is not None
    const = lambda shape: pl.BlockSpec(shape, lambda b: (0,) * len(shape))
    if rolling:
        assert l == cak.shape[0] - 1 and prev_new[0].shape[0] == l
        cache_spec = lambda c: pl.BlockSpec((c.shape[0], None) + c.shape[2:],
                                            lambda b: (0, b, 0, 0))
    else:
        cache_spec = lambda c: pl.BlockSpec((None, None) + c.shape[2:],
                                            lambda b: (l, b, 0, 0))
    in_specs = [
        pl.BlockSpec((t_new, A_WIDTH), lambda b: (b, QKV_QA)),
        pl.BlockSpec((t_new, B_WIDTH), lambda b: (b, QKV_QB)),
        pl.BlockSpec((t_new, A_WIDTH), lambda b: (b, 0)),
        pl.BlockSpec((t_new, A_WIDTH), lambda b: (b, 1)),
        pl.BlockSpec((t_new, B_KV_WIDTH), lambda b: (b, 0)),
        pl.BlockSpec((t_new, B_KV_WIDTH), lambda b: (b, 1)),
        cache_spec(cak), cache_spec(cav), cache_spec(cbk), cache_spec(cbv),
        const((rows, a_len)), const((rows, LANES)),
        const((rows, b_len)), const((rows, LANES)),
        const((rows, 1)),
    ]
    args = [qkv, qkv, kva32, kva32, kvb32, kvb32, cak, cav, cbk, cbv,
            bias_ac, bias_an, bias_bc, bias_bn, sink_col]
    out_specs = [
        pl.BlockSpec((t_new, A_WIDTH), lambda b: (b, 0)),
        pl.BlockSpec((t_new, B_WIDTH), lambda b: (b, 0)),
    ]
    out_shape = [
        jax.ShapeDtypeStruct((m, A_WIDTH), jnp.bfloat16),
        jax.ShapeDtypeStruct((m, B_WIDTH), jnp.bfloat16),
    ]
    if rolling:
        new_a, new_b = prev_new
        prev_spec = lambda width, col: pl.BlockSpec((l, t_new, width), lambda b: (0, b, col))
        in_specs += [prev_spec(A_WIDTH, 0), prev_spec(A_WIDTH, 1),
                     prev_spec(B_KV_WIDTH, 0), prev_spec(B_KV_WIDTH, 1)]
        args += [new_a, new_a, new_b, new_b]
        out_specs += [cache_spec(c) for c in caches]
        out_shape += [jax.ShapeDtypeStruct(c.shape, jnp.float32) for c in caches]
    return pl.pallas_call(
        functools.partial(_attn_sample_kernel, t_new=t_new, layer=l, rolling=rolling),
        grid=(m // t_new,),
        in_specs=in_specs,
        out_specs=out_specs,
        out_shape=out_shape,
        compiler_params=_params(("parallel",)),
        name="attn_sample",
    )(*args)


def _prompt_caches_kernel(*refs, depth):
    in_refs, (ak_ref, av_ref, bk_ref, bv_ref) = refs[:4 * depth], refs[4 * depth:]
    for k in range(depth):
        @pl.when(pl.program_id(0) == k)
        def _(k=k):
            ka_ref, va_ref, kb_ref, vb_ref = in_refs[4 * k:4 * k + 4]
            ak_ref[...] = ka_ref[...].T
            av_ref[...] = va_ref[...].T
            bk_ref[...] = kb_ref[...].T
            bv_ref[...] = vb_ref[...].T


def _prompt_caches(kva32s, kvb32s, *, batch, seq, na, nb):
    depth = len(kva32s)

    def rows_map(k, tile, col):
        def index_map(l, b):
            bb = jnp.where(l < k, 0, jnp.where(l > k, batch - 1, b))
            return ((bb + 1) * tile - 1, col)
        return index_map

    in_specs, args = [], []
    for k in range(depth):
        in_specs += [
            pl.BlockSpec((na, A_WIDTH), rows_map(k, seq // na, 0)),
            pl.BlockSpec((na, A_WIDTH), rows_map(k, seq // na, 1)),
            pl.BlockSpec((nb, B_KV_WIDTH), rows_map(k, seq // nb, 0)),
            pl.BlockSpec((nb, B_KV_WIDTH), rows_map(k, seq // nb, 1)),
        ]
        args += [kva32s[k], kva32s[k], kvb32s[k], kvb32s[k]]
    out_shapes = [(depth, batch, A_WIDTH, na)] * 2 + [(depth, batch, B_KV_WIDTH, nb)] * 2
    return pl.pallas_call(
        functools.partial(_prompt_caches_kernel, depth=depth),
        grid=(depth, batch),
        in_specs=in_specs,
        out_specs=[pl.BlockSpec((None, None) + s[2:], lambda l, b: (l, b, 0, 0))
                   for s in out_shapes],
        out_shape=[jax.ShapeDtypeStruct(s, jnp.float32) for s in out_shapes],
        compiler_params=_params(("arbitrary", "arbitrary")),
        name="prompt_caches",
    )(*args)


def _merge_kernel(oa_ref, ob_ref, ga_ref, gb_ref, x_ref, wa_ref, wb_ref, wo_ref,
                  g_ref, gnext_ref, y_ref, h_ref):
    ta = jnp.dot(oa_ref[...], wa_ref[...], preferred_element_type=jnp.float32)
    tb = jnp.dot(ob_ref[...], wb_ref[...], preferred_element_type=jnp.float32)
    mixed = (jax.nn.sigmoid(ga_ref[...].astype(jnp.float32)) * ta
             + jax.nn.sigmoid(gb_ref[...].astype(jnp.float32)) * tb)
    z = jnp.dot(mixed.astype(jnp.bfloat16), wo_ref[...],
                preferred_element_type=jnp.float32)
    y = x_ref[...] + _rms_scale(z, g_ref[...])
    y_ref[...] = y
    h_ref[...] = _rms_scale(y, gnext_ref[...]).astype(h_ref.dtype)


def _merge(oa, ob, gates, x, wa, wb, wo, g, g_next, l, *, tm):
    m, d = x.shape
    resident = lambda w: pl.BlockSpec(w.shape, lambda i: (0, 0),
                                      pipeline_mode=pl.Buffered(1))
    return pl.pallas_call(
        _merge_kernel,
        grid=(m // tm,),
        in_specs=[
            pl.BlockSpec((tm, A_WIDTH), lambda i: (i, 0)),
            pl.BlockSpec((tm, B_WIDTH), lambda i: (i, 0)),
            pl.BlockSpec((tm, d), lambda i: (i, 0)),
            pl.BlockSpec((tm, d), lambda i: (i, 1)),
            pl.BlockSpec((tm, d), lambda i: (i, 0)),
            resident(wa), resident(wb), resident(wo),
            _layer_vec_spec(l, d),
            _layer_vec_spec(l, d),
        ],
        out_specs=[pl.BlockSpec((tm, d), lambda i: (i, 0))] * 2,
        out_shape=[jax.ShapeDtypeStruct((m, d), jnp.float32),
                   jax.ShapeDtypeStruct((m, d), jnp.bfloat16)],
        compiler_params=_params(("parallel",)),
        name="merge",
    )(oa, ob, gates, gates, x, wa, wb, wo, g, g_next)


def _ffn_kernel(x_ref, h_ref, wu_ref, wd_ref, gpost_ref, *refs):
    f = pl.program_id(1)
    if len(refs) == 3:
        w32_ref, y_ref, w16_ref = refs
        w16_ref[...] = w32_ref[...].astype(w16_ref.dtype)
    else:
        y_ref, = refs

    @pl.when(f == 0)
    def _():
        y_ref[...] = jnp.zeros_like(y_ref)

    u = jnp.dot(h_ref[...], wu_ref[...], preferred_element_type=jnp.float32)
    u = jnp.square(jnp.maximum(u, 0.0)).astype(jnp.bfloat16)
    y_ref[...] += jnp.dot(u, wd_ref[...], preferred_element_type=jnp.float32)

    @pl.when(f == pl.num_programs(1) - 1)
    def _():
        y_ref[...] = x_ref[...] + _rms_scale(y_ref[...], gpost_ref[...])


def _ffn(x, h, wu, wd, gpost, l, cast=None, *, tm, tf):
    m, d = x.shape
    dff = wu.shape[-1]
    n_f = dff // tf
    in_specs = [
        pl.BlockSpec((tm, d), lambda i, f: (i, 0)),
        pl.BlockSpec((tm, d), lambda i, f: (i, 0)),
        pl.BlockSpec((d, tf), lambda i, f: (0, f)),
        pl.BlockSpec((tf, d), lambda i, f: (f, 0)),
        _layer_vec_spec(l, d),
    ]
    out_specs = [pl.BlockSpec((tm, d), lambda i, f: (i, 0))]
    out_shape = [jax.ShapeDtypeStruct((m, d), jnp.float32)]
    args = [x, h, wu, wd, gpost]
    if cast is not None:
        w, lc = cast
        rows = w.shape[1] // ((m // tm) * n_f)
        in_specs.append(pl.BlockSpec((None, rows, w.shape[2]),
                                     lambda i, f: (lc, i * n_f + f, 0)))
        out_specs.append(pl.BlockSpec((rows, w.shape[2]), lambda i, f: (i * n_f + f, 0)))
        out_shape.append(jax.ShapeDtypeStruct(w.shape[1:], jnp.bfloat16))
        args.append(w)
    return pl.pallas_call(
        _ffn_kernel,
        grid=(m // tm, n_f),
        in_specs=in_specs,
        out_specs=out_specs,
        out_shape=out_shape,
        compiler_params=_params(("arbitrary", "arbitrary")),
        name="ffn",
    )(*args)


def _t5_bucket(rel):
    half = T5_BUCKETS // 2
    exact = half // 2
    ret = jnp.where(rel > 0, half, 0)
    n = jnp.abs(rel)
    large = exact + (jnp.log(jnp.maximum(n, 1).astype(jnp.float32) / exact)
                     / math.log(T5_MAX_DIST / exact) * (half - exact)).astype(jnp.int32)
    large = jnp.minimum(large, half - 1)
    return ret + jnp.where(n < exact, n, large)


def _a_bias_of_rel(table):
    scaled = table.T * LOG2E
    return lambda rel: scaled[:, jnp.clip(rel, -A_REL_CLIP, A_REL_CLIP) + A_REL_CLIP]


def _b_bias_of_rel(table):
    scaled = table.T * LOG2E
    return lambda rel: scaled[:, _t5_bucket(-rel)]


def _hankel(u, q, n):
    heads, k = u.shape
    period = q + n
    u = jnp.pad(u, ((0, 0), (0, period - k)))
    flat = jnp.tile(u, (1, q + 1))[:, :q * (period + 1)]
    return flat.reshape(heads, q, period + 1)[:, :, :n]


def _rel_bias(bias_of_rel, q_len, n_keys, k0):
    k = jnp.arange(q_len + n_keys - 1)
    u = bias_of_rel(k - (n_keys - 1) - k0)
    return _hankel(u, q_len, n_keys)[:, :, ::-1]


def _band_tile(bias_of_rel, tq, n_prev):
    window = tq + n_prev * CHUNK
    band = _rel_bias(bias_of_rel, CHUNK, (n_prev + 1) * CHUNK, -n_prev * CHUNK)
    blocks = []
    for c in range(tq // CHUNK):
        left = c * CHUNK
        right = window - left - band.shape[-1]
        blocks.append(jnp.pad(band, ((0, 0), (0, 0), (left, right)),
                              constant_values=NEG_INF))
    return jnp.concatenate(blocks, axis=1)


def _sample_bias(bias_of_rel, t_new, n_cache):
    full = _rel_bias(bias_of_rel, t_new, n_cache + t_new, -n_cache)
    full = full.reshape(-1, n_cache + t_new)
    new = jnp.pad(full[:, n_cache:], ((0, 0), (LANES - t_new, 0)),
                  constant_values=NEG_INF)
    return full[:, :n_cache], new


def _cache_t(c):
    d, s, r, h, e = c.shape
    return jnp.transpose(c, (0, 1, 3, 4, 2)).reshape(d, s, h * e, r)


def _cache_untranspose(c, heads):
    d, s, w, r = c.shape
    return jnp.transpose(c.reshape(d, s, heads, w // heads, r), (0, 1, 4, 2, 3))


def kernel(x_prompt, x_sample, cache_a_k, cache_a_v, cache_b_k, cache_b_v, w_in,
           w_a_out, w_b_out, w_out, a_rel_table, t5_table, b_sinks, g_mix_pre,
           g_mix_post, g_ffn_pre, g_ffn_post, w_up, w_down):
    depth = w_in.shape[0]
    assert depth >= 2, "the last layer's sample kernel rolls the earlier layers' caches"
    batch, seq, d = x_prompt.shape
    dec_batch, t_new, _ = x_sample.shape
    a_len = cache_a_k.shape[2]
    b_len = cache_b_k.shape[2]
    bf16 = jnp.bfloat16
    mp = batch * seq
    ms = dec_batch * t_new

    yp = x_prompt.reshape(mp, d)
    ys = x_sample.reshape(ms, d)

    w_in16 = w_in[0].astype(bf16)
    later_weights = (w_a_out, w_b_out, w_out, w_up, w_down)
    g_pre = g_mix_pre[:, None, :]
    g_post = g_mix_post[:, None, :]
    gf_pre = g_ffn_pre[:, None, :]
    gf_post = g_ffn_post[:, None, :]
    caches = tuple(_cache_t(c) for c in (cache_a_k, cache_a_v, cache_b_k, cache_b_v))

    b_of_rel = _b_bias_of_rel(t5_table)
    bias_b_tile = _band_tile(b_of_rel, B_TQ, B_LEFT_CHUNKS)
    bias_bc, bias_bn = _sample_bias(b_of_rel, t_new, b_len)

    prompt_kva, prompt_kvb, sample_kva, sample_kvb = [], [], [], []
    for l in range(depth):
        a_of_rel = _a_bias_of_rel(a_rel_table[l])
        sinks = b_sinks[l] * LOG2E

        qkv, kvb16, gates, kva32, kvb32 = _in_proj(yp, g_pre, w_in16, l, tm=1024)
        oa, wa16, wb16, wo16, wu16, wd16 = _attn_a(
            qkv, _band_tile(a_of_rel, A_TQ, A_LEFT_CHUNKS), later_weights, l, m=mp, seq=seq)
        ob = _attn_b(qkv, kvb16, bias_b_tile, sinks, m=mp, seq=seq)
        yp, hp = _merge(oa, ob, gates, yp, wa16, wb16, wo16, g_post, gf_pre, l, tm=256)
        w_in16_this = w_in16
        if l + 1 < depth:
            yp, w_in16 = _ffn(yp, hp, wu16, wd16, gf_post, l, (w_in, l + 1),
                              tm=512, tf=1024)
        else:
            yp, = _ffn(yp, hp, wu16, wd16, gf_post, l, tm=512, tf=1024)
        prompt_kva.append(kva32)
        prompt_kvb.append(kvb32)

        qkv, _, gates, kva32, kvb32 = _in_proj(ys, g_pre, w_in16_this, l, tm=ms)
        bias_ac, bias_an = _sample_bias(a_of_rel, t_new, a_len)
        sink_col = jnp.repeat(sinks, t_new)[:, None]
        if l + 1 < depth:
            oa, ob = _attn_sample(qkv, kva32, kvb32, caches, l, bias_ac, bias_an, bias_bc,
                                  bias_bn, sink_col, t_new=t_new)
            sample_kva.append(kva32)
            sample_kvb.append(kvb32)
        else:
            prev_new = (jnp.stack(sample_kva), jnp.stack(sample_kvb))
            oa, ob, aks, avs, bks, bvs = _attn_sample(
                qkv, kva32, kvb32, caches, l, bias_ac, bias_an, bias_bc, bias_bn,
                sink_col, prev_new, t_new=t_new)
        ys, hs = _merge(oa, ob, gates, ys, wa16, wb16, wo16, g_post, gf_pre, l, tm=ms)
        ys, = _ffn(ys, hs, wu16, wd16, gf_post, l, tm=ms, tf=1024)

    akp, avp, bkp, bvp = _prompt_caches(prompt_kva, prompt_kvb, batch=batch, seq=seq,
                                        na=min(a_len, seq), nb=min(b_len, seq))
    return (yp.reshape(batch, seq, d), ys.reshape(dec_batch, t_new, d),
            _cache_untranspose(akp, A_HEADS), _cache_untranspose(avp, A_HEADS),
            _cache_untranspose(bkp, B_KV_HEADS), _cache_untranspose(bvp, B_KV_HEADS),
            _cache_untranspose(aks, A_HEADS), _cache_untranspose(avs, A_HEADS),
            _cache_untranspose(bks, B_KV_HEADS), _cache_untranspose(bvs, B_KV_HEADS))
```

```python
import functools
import math

import jax
import jax.numpy as jnp
from jax import lax
from jax.experimental import pallas as pl
from jax.experimental.pallas import tpu as pltpu

D_MODEL = 2048
CHUNK = 64
HEAD_DIM = 64
A_HEADS = 16
A_WIDTH = A_HEADS * HEAD_DIM
A_LEFT_CHUNKS = 8
A_REL_CLIP = 256
B_HEADS = 16
B_KV_HEADS = 2
B_GROUP = B_HEADS // B_KV_HEADS
B_WIDTH = B_HEADS * HEAD_DIM
B_KV_WIDTH = B_KV_HEADS * HEAD_DIM
B_LEFT_CHUNKS = 2
T5_BUCKETS = 32
T5_MAX_DIST = 128
EPS = 1e-6
NEG_INF = -1e30
LOG2E = math.log2(math.e)
Q_SCALE = HEAD_DIM ** -0.5 * LOG2E

LANES = 128
MXU_COLS = 256

COL_QA = 0
COL_KA = A_WIDTH
COL_VA = 2 * A_WIDTH
COL_QB = 3 * A_WIDTH
COL_KB = 3 * A_WIDTH + B_WIDTH
COL_GA = COL_KB + 2 * B_KV_WIDTH
PROJ_TN = 4 * MXU_COLS
QKV_KA, QKV_VA, QKV_QA, QKV_QB = 0, 1, 2, 3

A_TQ = 256
B_TQ = 128
A_KBLOCKS = A_LEFT_CHUNKS * CHUNK // A_TQ + 1
B_KBLOCKS = B_LEFT_CHUNKS * CHUNK // B_TQ + 1

MERGE_SLAB = 256

VMEM_LIMIT = 56 * 1024 * 1024


def _params(sem, vmem=VMEM_LIMIT):
    return pltpu.CompilerParams(dimension_semantics=sem, vmem_limit_bytes=vmem)


def _rms_scale(x, g):
    return x * lax.rsqrt(jnp.mean(x * x, axis=-1, keepdims=True) + EPS) * g


def _layer_vec_spec(l, d):
    return pl.BlockSpec((None, 1, d), lambda *_: (l, 0, 0))


PROJ_KV_STEPS = 2
PROJ_Q_STEPS = 4
PROJ_KVB_STEP = 4
PROJ_GATE_STEP0 = 5


def _in_proj_kernel(x_ref, g_ref, *refs):
    n_w = PROJ_TN // MXU_COLS
    w_refs = refs[:n_w]
    qkv_ref, kvb16_ref, gates_ref, kva32_ref, kvb32_ref, h_ref = refs[n_w:]
    j = pl.program_id(1)

    @pl.when(j == 0)
    def _():
        h_ref[...] = _rms_scale(x_ref[...], g_ref[...]).astype(h_ref.dtype)

    def tile(out16_ref, out32_ref, scale=None):
        h = h_ref[...]
        for k, w_ref in enumerate(w_refs):
            cols = slice(k * MXU_COLS, (k + 1) * MXU_COLS)
            acc = jnp.dot(h, w_ref[...], preferred_element_type=jnp.float32)
            if out32_ref is not None:
                out32_ref[:, cols] = acc
            if scale is not None:
                acc = acc * scale
            out16_ref[:, cols] = acc.astype(out16_ref.dtype)

    @pl.when(j < PROJ_KV_STEPS)
    def _():
        tile(qkv_ref, kva32_ref)

    @pl.when((j >= PROJ_KV_STEPS) & (j < PROJ_Q_STEPS))
    def _():
        tile(qkv_ref, None, Q_SCALE)

    @pl.when(j == PROJ_KVB_STEP)
    def _():
        acc = jnp.dot(h_ref[...], w_refs[0][...], preferred_element_type=jnp.float32)
        kvb16_ref[...] = acc.astype(kvb16_ref.dtype)
        kvb32_ref[...] = acc

    @pl.when(j >= PROJ_GATE_STEP0)
    def _():
        tile(gates_ref, None)


def _in_proj(x, g, w_in, l, *, tm):
    m, d = x.shape
    n_w = PROJ_TN // MXU_COLS
    n_gate = 2 * D_MODEL // PROJ_TN
    src = [c // MXU_COLS for c in (COL_KA, COL_VA, COL_QA, COL_QB, COL_KB)]
    ga0 = COL_GA // MXU_COLS

    def w_map(k):
        def index_map(i, j):
            base = src[PROJ_KVB_STEP] if k == 0 else src[PROJ_KVB_STEP - 1] + k
            for step in range(PROJ_KVB_STEP - 1, -1, -1):
                base = jnp.where(j == step, src[step] + k, base)
            gate = ga0 + (j - PROJ_GATE_STEP0) * n_w + k
            return (0, jnp.where(j >= PROJ_GATE_STEP0, gate, base))
        return index_map

    kvb_n = 2 * B_KV_WIDTH
    return pl.pallas_call(
        _in_proj_kernel,
        grid=(m // tm, PROJ_GATE_STEP0 + n_gate),
        in_specs=[
            pl.BlockSpec((tm, d), lambda i, j: (i, 0)),
            _layer_vec_spec(l, d),
        ] + [pl.BlockSpec((d, MXU_COLS), w_map(k)) for k in range(n_w)],
        out_specs=[
            pl.BlockSpec((tm, PROJ_TN), lambda i, j: (i, jnp.minimum(j, PROJ_Q_STEPS - 1))),
            pl.BlockSpec((tm, kvb_n), lambda i, j: (i, 0)),
            pl.BlockSpec((tm, PROJ_TN),
                         lambda i, j: (i, jnp.clip(j - PROJ_GATE_STEP0, 0, n_gate - 1))),
            pl.BlockSpec((tm, PROJ_TN), lambda i, j: (i, jnp.minimum(j, PROJ_KV_STEPS - 1))),
            pl.BlockSpec((tm, kvb_n), lambda i, j: (i, 0)),
        ],
        out_shape=[
            jax.ShapeDtypeStruct((m, PROJ_Q_STEPS * PROJ_TN), jnp.bfloat16),
            jax.ShapeDtypeStruct((m, kvb_n), jnp.bfloat16),
            jax.ShapeDtypeStruct((m, 2 * D_MODEL), jnp.bfloat16),
            jax.ShapeDtypeStruct((m, PROJ_KV_STEPS * PROJ_TN), jnp.float32),
            jax.ShapeDtypeStruct((m, kvb_n), jnp.float32),
        ],
        scratch_shapes=[pltpu.VMEM((tm, d), jnp.bfloat16)],
        compiler_params=_params(("parallel", "arbitrary")),
        name="in_proj",
    )(x, g, *([w_in] * n_w))


def _low_half():
    return lax.broadcasted_iota(jnp.int32, (1, LANES), 1) < HEAD_DIM


def _start_mask(tile_in_seq, n_kblocks, tq):
    lane = lax.broadcasted_iota(jnp.int32, (1, n_kblocks * tq), 1)
    first_valid = jnp.maximum(n_kblocks - 1 - tile_in_seq, 0) * tq
    return jnp.where(lane >= first_valid, 0.0, NEG_INF)


def _attn_a_kernel(q_ref, *refs, tiles_per_seq):
    k_refs = refs[:A_KBLOCKS]
    v_refs = refs[A_KBLOCKS:2 * A_KBLOCKS]
    bias_ref = refs[2 * A_KBLOCKS]
    n_cast = (len(refs) - 2 * A_KBLOCKS - 2) // 2
    w32_refs = refs[2 * A_KBLOCKS + 1:2 * A_KBLOCKS + 1 + n_cast]
    o_ref = refs[2 * A_KBLOCKS + 1 + n_cast]
    w16_refs = refs[2 * A_KBLOCKS + 2 + n_cast:]
    tile_in_seq = pl.program_id(0) % tiles_per_seq
    low = _low_half()

    def heads(start):
        for w32_ref, w16_ref in zip(w32_refs, w16_refs):
            w16_ref[...] = w32_ref[...].astype(w16_ref.dtype)
        for p in range(A_HEADS // 2):
            sl = slice(p * LANES, (p + 1) * LANES)
            q2 = q_ref[:, sl]
            k2 = jnp.concatenate([r[:, sl] for r in k_refs], axis=0)
            v2 = jnp.concatenate([r[:, sl] for r in v_refs], axis=0)
            qs = jnp.concatenate([jnp.where(low, q2, 0), jnp.where(low, 0, q2)], axis=0)
            s_all = lax.dot_general(qs, k2, (((1,), (1,)), ((), ())),
                                    preferred_element_type=jnp.float32)
            probs = []
            for hh in range(2):
                s = s_all[hh * A_TQ:(hh + 1) * A_TQ] + bias_ref[2 * p + hh]
                if start is not None:
                    s = s + start
                s = s.astype(v2.dtype)
                probs.append(jnp.exp2(s - jnp.max(s, axis=-1, keepdims=True)))
            v_ones = jnp.concatenate([v2, jnp.ones_like(v2)], axis=1)
            r = jnp.dot(jnp.concatenate(probs, axis=0), v_ones,
                        preferred_element_type=jnp.float32)
            out = r[:, :LANES] / r[:, LANES:]
            o_ref[:, sl] = jnp.where(low, out[:A_TQ], out[A_TQ:]).astype(o_ref.dtype)

    @pl.when(tile_in_seq >= A_KBLOCKS - 1)
    def _():
        heads(None)

    @pl.when(tile_in_seq < A_KBLOCKS - 1)
    def _():
        heads(_start_mask(tile_in_seq, A_KBLOCKS, A_TQ))


def _band_kv_map(j, col, n_kblocks, tiles_per_seq):
    def index_map(t, *_):
        i = t % tiles_per_seq
        return (t - i + jnp.maximum(i - (n_kblocks - 1) + j, 0), col)
    return index_map


def _attn_a(qkv, bias, weights, l, *, m, seq):
    tiles_per_seq = seq // A_TQ
    steps = m // A_TQ
    kv_spec = lambda j, col: pl.BlockSpec(
        (A_TQ, A_WIDTH), _band_kv_map(j, col, A_KBLOCKS, tiles_per_seq))
    slab32 = lambda w: pl.BlockSpec((None, w.shape[1] // steps, w.shape[2]),
                                    lambda t: (l, t, 0))
    slab16 = lambda w: pl.BlockSpec((w.shape[1] // steps, w.shape[2]), lambda t: (t, 0))
    return pl.pallas_call(
        functools.partial(_attn_a_kernel, tiles_per_seq=tiles_per_seq),
        grid=(steps,),
        in_specs=[pl.BlockSpec((A_TQ, A_WIDTH), lambda t: (t, QKV_QA))]
        + [kv_spec(j, QKV_KA) for j in range(A_KBLOCKS)]
        + [kv_spec(j, QKV_VA) for j in range(A_KBLOCKS)]
        + [pl.BlockSpec((A_HEADS, A_TQ, A_KBLOCKS * A_TQ), lambda t: (0, 0, 0),
                        pipeline_mode=pl.Buffered(1))]
        + [slab32(w) for w in weights],
        out_specs=[pl.BlockSpec((A_TQ, A_WIDTH), lambda t: (t, 0))]
        + [slab16(w) for w in weights],
        out_shape=[jax.ShapeDtypeStruct((m, A_WIDTH), jnp.bfloat16)]
        + [jax.ShapeDtypeStruct(w.shape[1:], jnp.bfloat16) for w in weights],
        compiler_params=_params(("parallel",)),
        name="attn_a",
    )(qkv, *([qkv] * (2 * A_KBLOCKS)), bias, *weights)


def _dup_half(x, g):
    swapped = pltpu.roll(x, HEAD_DIM, 1)
    low = _low_half()
    return jnp.where(low, x, swapped) if g == 0 else jnp.where(low, swapped, x)


def _attn_b_kernel(sink_ref, q_ref, *refs, tiles_per_seq):
    k_refs = refs[:B_KBLOCKS]
    v_refs = refs[B_KBLOCKS:2 * B_KBLOCKS]
    bias_ref, o_ref = refs[2 * B_KBLOCKS:]
    start = _start_mask(pl.program_id(0) % tiles_per_seq, B_KBLOCKS, B_TQ)
    low = _low_half()
    k2 = jnp.concatenate([r[...] for r in k_refs], axis=0).astype(jnp.float32)
    v2 = jnp.concatenate([r[...] for r in v_refs], axis=0).astype(jnp.float32)
    pairs = B_GROUP // 2
    for g in range(B_KV_HEADS):
        kd = _dup_half(k2, g).astype(jnp.bfloat16)
        vd = _dup_half(v2, g).astype(jnp.bfloat16)
        stacked = []
        for p in range(pairs):
            c0 = (g * pairs + p) * LANES
            q2 = q_ref[:, c0:c0 + LANES]
            stacked.append(jnp.where(low, q2, 0))
            stacked.append(jnp.where(low, 0, q2))
        qs = jnp.concatenate(stacked, axis=0)
        s_all = lax.dot_general(qs, kd, (((1,), (1,)), ((), ())),
                                preferred_element_type=jnp.float32)
        probs, sink_terms = [], []
        for hl in range(B_GROUP):
            h = g * B_GROUP + hl
            s = s_all[hl * B_TQ:(hl + 1) * B_TQ] + bias_ref[h] + start
            sink = sink_ref[h]
            mx = jnp.maximum(jnp.max(s, axis=-1, keepdims=True), sink)
            probs.append(jnp.exp2(s - mx).astype(jnp.bfloat16))
            sink_terms.append(jnp.exp2(sink - mx))
        v_ones = jnp.concatenate([vd, jnp.ones_like(vd)], axis=1)
        r = jnp.dot(jnp.concatenate(probs, axis=0), v_ones,
                    preferred_element_type=jnp.float32)
        outs = []
        for hl in range(B_GROUP):
            rh = r[hl * B_TQ:(hl + 1) * B_TQ]
            outs.append(rh[:, :LANES] / (rh[:, LANES:] + sink_terms[hl]))
        for p in range(pairs):
            c0 = (g * pairs + p) * LANES
            o_ref[:, c0:c0 + LANES] = jnp.where(
                low, outs[2 * p], outs[2 * p + 1]).astype(o_ref.dtype)


def _attn_b(qkv, kvb16, bias, sinks, *, m, seq):
    tiles_per_seq = seq // B_TQ
    kv_spec = lambda j, col: pl.BlockSpec(
        (B_TQ, LANES), _band_kv_map(j, col, B_KBLOCKS, tiles_per_seq))
    return pl.pallas_call(
        functools.partial(_attn_b_kernel, tiles_per_seq=tiles_per_seq),
        grid_spec=pltpu.PrefetchScalarGridSpec(
            num_scalar_prefetch=1,
            grid=(m // B_TQ,),
            in_specs=[pl.BlockSpec((B_TQ, B_WIDTH), lambda t, s: (t, QKV_QB))]
            + [kv_spec(j, 0) for j in range(B_KBLOCKS)]
            + [kv_spec(j, 1) for j in range(B_KBLOCKS)]
            + [pl.BlockSpec((B_HEADS, B_TQ, B_KBLOCKS * B_TQ), lambda t, s: (0, 0, 0))],
            out_specs=pl.BlockSpec((B_TQ, B_WIDTH), lambda t, s: (t, 0)),
        ),
        out_shape=jax.ShapeDtypeStruct((m, B_WIDTH), jnp.bfloat16),
        compiler_params=_params(("parallel",)),
        name="attn_b",
    )(sinks, qkv, *([kvb16] * (2 * B_KBLOCKS)), bias)


def _pad_top(x, t_new):
    return jnp.concatenate([jnp.zeros((LANES - t_new, x.shape[1]), x.dtype), x], axis=0)


def _roll_in(cache_t, new_rows, out_ref, layer, t_new):
    n = cache_t.shape[1]
    shifted = pltpu.roll(cache_t, n - t_new, 1)
    lane = lax.broadcasted_iota(jnp.int32, (1, LANES), 1)
    tail = jnp.where(lane < LANES - t_new, shifted[:, n - LANES:],
                     _pad_top(new_rows, t_new).T)
    if n > LANES:
        out_ref[layer, :, :n - LANES] = shifted[:, :n - LANES]
    out_ref[layer, :, n - LANES:] = tail


def _attn_sample_kernel(qa_ref, qb_ref, kan_ref, van_ref, kbn_ref, vbn_ref,
                        cak_ref, cav_ref, cbk_ref, cbv_ref,
                        bias_ac_ref, bias_an_ref, bias_bc_ref, bias_bn_ref, sink_ref,
                        *refs, t_new, layer, rolling):
    if rolling:
        prev_refs, (oa_ref, ob_ref), out_refs = refs[:4], refs[4:6], refs[6:]
        for cache_ref, new_ref, prev_ref, out_ref in zip(
                (cak_ref, cav_ref, cbk_ref, cbv_ref), (kan_ref, van_ref, kbn_ref, vbn_ref),
                prev_refs, out_refs):
            for l2 in range(layer + 1):
                rows_new = new_ref[...] if l2 == layer else prev_ref[l2]
                _roll_in(cache_ref[l2], rows_new, out_ref, l2, t_new)
        cak_ref, cav_ref, cbk_ref, cbv_ref = (
            r.at[layer] for r in (cak_ref, cav_ref, cbk_ref, cbv_ref))
    else:
        oa_ref, ob_ref = refs
    heads = A_HEADS
    rows = heads * t_new
    width = heads * HEAD_DIM
    bf16 = jnp.bfloat16
    row_head = lax.broadcasted_iota(jnp.int32, (rows, width), 0) // t_new
    lane_head = lax.broadcasted_iota(jnp.int32, (rows, width), 1) // HEAD_DIM
    own = row_head == lane_head

    def stack_q(q):
        return jnp.where(own, jnp.concatenate([q] * heads, axis=0), 0)

    def nt_dot(a, b):
        return lax.dot_general(a, b, (((1,), (1,)), ((), ())),
                               preferred_element_type=jnp.float32)

    def finish(s_c, s_n, vt_c, v_n, sink):
        mx = jnp.maximum(jnp.max(s_c, axis=-1, keepdims=True),
                         jnp.max(s_n, axis=-1, keepdims=True))
        if sink is not None:
            mx = jnp.maximum(mx, sink)
        e_c = jnp.exp2(s_c - mx)
        e_n = jnp.exp2(s_n - mx)
        denom = jnp.sum(e_c, axis=-1, keepdims=True) + jnp.sum(e_n, axis=-1, keepdims=True)
        if sink is not None:
            denom = denom + jnp.exp2(sink - mx)
        o = nt_dot(e_c.astype(bf16), vt_c)
        o = o + jnp.dot(e_n.astype(bf16), v_n, preferred_element_type=jnp.float32)
        o = jnp.where(own, o / denom, 0.0)
        return jnp.sum(o.reshape(heads, t_new, width), axis=0)

    k_n = _pad_top(kan_ref[...], t_new).astype(bf16)
    v_n = _pad_top(van_ref[...], t_new).astype(bf16)
    qs = stack_q(qa_ref[...])
    s_c = jnp.dot(qs, cak_ref[...].astype(bf16),
                  preferred_element_type=jnp.float32) + bias_ac_ref[...]
    s_n = nt_dot(qs, k_n) + bias_an_ref[...]
    oa_ref[...] = finish(s_c, s_n, cav_ref[...].astype(bf16), v_n, None).astype(oa_ref.dtype)

    def expansion(shape, src_axis):
        src = lax.broadcasted_iota(jnp.int32, shape, src_axis)
        dst = lax.broadcasted_iota(jnp.int32, shape, 1 - src_axis)
        return ((src % HEAD_DIM == dst % HEAD_DIM)
                & (src // HEAD_DIM == dst // (B_GROUP * HEAD_DIM))).astype(bf16)

    def widen(x):
        return jnp.dot(x.astype(bf16), expansion((B_KV_WIDTH, width), 0),
                       preferred_element_type=jnp.float32).astype(bf16)

    def widen_t(xt):
        return jnp.dot(expansion((width, B_KV_WIDTH), 1), xt.astype(bf16),
                       preferred_element_type=jnp.float32).astype(bf16)

    k_n = widen(_pad_top(kbn_ref[...], t_new))
    v_n = widen(_pad_top(vbn_ref[...], t_new))
    qs = stack_q(qb_ref[...])
    s_c = jnp.dot(qs, widen_t(cbk_ref[...]),
                  preferred_element_type=jnp.float32) + bias_bc_ref[...]
    s_n = nt_dot(qs, k_n) + bias_bn_ref[...]
    ob_ref[...] = finish(s_c, s_n, widen_t(cbv_ref[...]), v_n, sink_ref[...]).astype(ob_ref.dtype)


def _attn_sample(qkv, kva32, kvb32, caches, l, bias_ac, bias_an, bias_bc, bias_bn,
                 sink_col, prev_new=None, *, t_new):
    m = qkv.shape[0]
    cak, cav, cbk, cbv = caches
    a_len, b_len = cak.shape[-1], cbk.shape[-1]
    rows = A_HEADS * t_new
    rolling = prev_new is not None
    const = lambda shape: pl.BlockSpec(shape, lambda b: (0,) * len(shape))
    if rolling:
        assert l == cak.shape[0] - 1 and prev_new[0].shape[0] == l
        cache_spec = lambda c: pl.BlockSpec((c.shape[0], None) + c.shape[2:],
                                            lambda b: (0, b, 0, 0))
    else:
        cache_spec = lambda c: pl.BlockSpec((None, None) + c.shape[2:],
                                            lambda b: (l, b, 0, 0))
    in_specs = [
        pl.BlockSpec((t_new, A_WIDTH), lambda b: (b, QKV_QA)),
        pl.BlockSpec((t_new, B_WIDTH), lambda b: (b, QKV_QB)),
        pl.BlockSpec((t_new, A_WIDTH), lambda b: (b, 0)),
        pl.BlockSpec((t_new, A_WIDTH), lambda b: (b, 1)),
        pl.BlockSpec((t_new, B_KV_WIDTH), lambda b: (b, 0)),
        pl.BlockSpec((t_new, B_KV_WIDTH), lambda b: (b, 1)),
        cache_spec(cak), cache_spec(cav), cache_spec(cbk), cache_spec(cbv),
        const((rows, a_len)), const((rows, LANES)),
        const((rows, b_len)), const((rows, LANES)),
        const((rows, 1)),
    ]
    args = [qkv, qkv, kva32, kva32, kvb32, kvb32, cak, cav, cbk, cbv,
            bias_ac, bias_an, bias_bc, bias_bn, sink_col]
    out_specs = [
        pl.BlockSpec((t_new, A_WIDTH), lambda b: (b, 0)),
        pl.BlockSpec((t_new, B_WIDTH), lambda b: (b, 0)),
    ]
    out_shape = [
        jax.ShapeDtypeStruct((m, A_WIDTH), jnp.bfloat16),
        jax.ShapeDtypeStruct((m, B_WIDTH), jnp.bfloat16),
    ]
    if rolling:
        new_a, new_b = prev_new
        prev_spec = lambda width, col: pl.BlockSpec((l, t_new, width), lambda b: (0, b, col))
        in_specs += [prev_spec(A_WIDTH, 0), prev_spec(A_WIDTH, 1),
                     prev_spec(B_KV_WIDTH, 0), prev_spec(B_KV_WIDTH, 1)]
        args += [new_a, new_a, new_b, new_b]
        out_specs += [cache_spec(c) for c in caches]
        out_shape += [jax.ShapeDtypeStruct(c.shape, jnp.float32) for c in caches]
    return pl.pallas_call(
        functools.partial(_attn_sample_kernel, t_new=t_new, layer=l, rolling=rolling),
        grid=(m // t_new,),
        in_specs=in_specs,
        out_specs=out_specs,
        out_shape=out_shape,
        compiler_params=_params(("parallel",)),
        name="attn_sample",
    )(*args)


def _prompt_caches_kernel(*refs, depth):
    in_refs, (ak_ref, av_ref, bk_ref, bv_ref) = refs[:4 * depth], refs[4 * depth:]
    for k in range(depth):
        @pl.when(pl.program_id(0) == k)
        def _(k=k):
            ka_ref, va_ref, kb_ref, vb_ref = in_refs[4 * k:4 * k + 4]
            ak_ref[...] = ka_ref[...].T
            av_ref[...] = va_ref[...].T
            bk_ref[...] = kb_ref[...].T
            bv_ref[...] = vb_ref[...].T


def _prompt_caches(kva32s, kvb32s, *, batch, seq, na, nb):
    depth = len(kva32s)

    def rows_map(k, tile, col):
        def index_map(l, b):
            bb = jnp.where(l < k, 0, jnp.where(l > k, batch - 1, b))
            return ((bb + 1) * tile - 1, col)
        return index_map

    in_specs, args = [], []
    for k in range(depth):
        in_specs += [
            pl.BlockSpec((na, A_WIDTH), rows_map(k, seq // na, 0)),
            pl.BlockSpec((na, A_WIDTH), rows_map(k, seq // na, 1)),
            pl.BlockSpec((nb, B_KV_WIDTH), rows_map(k, seq // nb, 0)),
            pl.BlockSpec((nb, B_KV_WIDTH), rows_map(k, seq // nb, 1)),
        ]
        args += [kva32s[k], kva32s[k], kvb32s[k], kvb32s[k]]
    out_shapes = [(depth, batch, A_WIDTH, na)] * 2 + [(depth, batch, B_KV_WIDTH, nb)] * 2
    return pl.pallas_call(
        functools.partial(_prompt_caches_kernel, depth=depth),
        grid=(depth, batch),
        in_specs=in_specs,
        out_specs=[pl.BlockSpec((None, None) + s[2:], lambda l, b: (l, b, 0, 0))
                   for s in out_shapes],
        out_shape=[jax.ShapeDtypeStruct(s, jnp.float32) for s in out_shapes],
        compiler_params=_params(("arbitrary", "arbitrary")),
        name="prompt_caches",
    )(*args)


def _merge_kernel(oa_ref, ob_ref, ga_ref, gb_ref, x_ref, wa_ref, wb_ref, wo_ref,
                  g_ref, gnext_ref, y_ref, h_ref):
    tm = x_ref.shape[0]
    slab = min(tm, MERGE_SLAB)
    for r0 in range(0, tm, slab):
        rows = slice(r0, r0 + slab)
        ta = jnp.dot(oa_ref[rows], wa_ref[...], preferred_element_type=jnp.float32)
        tb = jnp.dot(ob_ref[rows], wb_ref[...], preferred_element_type=jnp.float32)
        mixed = (jax.nn.sigmoid(ga_ref[rows].astype(jnp.float32)) * ta
                 + jax.nn.sigmoid(gb_ref[rows].astype(jnp.float32)) * tb)
        z = jnp.dot(mixed.astype(jnp.bfloat16), wo_ref[...],
                    preferred_element_type=jnp.float32)
        y = x_ref[rows] + _rms_scale(z, g_ref[...])
        y_ref[rows] = y
        h_ref[rows] = _rms_scale(y, gnext_ref[...]).astype(h_ref.dtype)


def _merge(oa, ob, gates, x, wa, wb, wo, g, g_next, l, *, tm):
    m, d = x.shape
    resident = lambda w: pl.BlockSpec(w.shape, lambda i: (0, 0),
                                      pipeline_mode=pl.Buffered(1))
    return pl.pallas_call(
        _merge_kernel,
        grid=(m // tm,),
        in_specs=[
            pl.BlockSpec((tm, A_WIDTH), lambda i: (i, 0)),
            pl.BlockSpec((tm, B_WIDTH), lambda i: (i, 0)),
            pl.BlockSpec((tm, d), lambda i: (i, 0)),
            pl.BlockSpec((tm, d), lambda i: (i, 1)),
            pl.BlockSpec((tm, d), lambda i: (i, 0)),
            resident(wa), resident(wb), resident(wo),
            _layer_vec_spec(l, d),
            _layer_vec_spec(l, d),
        ],
        out_specs=[pl.BlockSpec((tm, d), lambda i: (i, 0))] * 2,
        out_shape=[jax.ShapeDtypeStruct((m, d), jnp.float32),
                   jax.ShapeDtypeStruct((m, d), jnp.bfloat16)],
        compiler_params=_params(("parallel",)),
        name="merge",
    )(oa, ob, gates, gates, x, wa, wb, wo, g, g_next)


def _ffn_kernel(x_ref, h_ref, wu_ref, wd_ref, gpost_ref, *refs):
    f = pl.program_id(1)
    if len(refs) == 3:
        w32_ref, y_ref, w16_ref = refs
        w16_ref[...] = w32_ref[...].astype(w16_ref.dtype)
    else:
        y_ref, = refs

    last = pl.num_programs(1) - 1

    @pl.when(f == 0)
    def _():
        y_ref[...] = jnp.zeros_like(y_ref)

    def partial(rows):
        u = jnp.dot(h_ref[rows], wu_ref[...], preferred_element_type=jnp.float32)
        u = jnp.square(jnp.maximum(u, 0.0)).astype(jnp.bfloat16)
        return y_ref[rows] + jnp.dot(u, wd_ref[...], preferred_element_type=jnp.float32)

    @pl.when(f < last)
    def _():
        y_ref[...] = partial(slice(None))

    @pl.when(f == last)
    def _():
        tm = x_ref.shape[0]
        slab = tm // 2 if tm % 32 == 0 else tm
        for r0 in range(0, tm, slab):
            rows = slice(r0, r0 + slab)
            y_ref[rows] = x_ref[rows] + _rms_scale(partial(rows), gpost_ref[...])


def _ffn(x, h, wu, wd, gpost, l, cast=None, *, tm, tf):
    m, d = x.shape
    dff = wu.shape[-1]
    n_f = dff // tf
    in_specs = [
        pl.BlockSpec((tm, d), lambda i, f: (i, 0)),
        pl.BlockSpec((tm, d), lambda i, f: (i, 0)),
        pl.BlockSpec((d, tf), lambda i, f: (0, f)),
        pl.BlockSpec((tf, d), lambda i, f: (f, 0)),
        _layer_vec_spec(l, d),
    ]
    out_specs = [pl.BlockSpec((tm, d), lambda i, f: (i, 0))]
    out_shape = [jax.ShapeDtypeStruct((m, d), jnp.float32)]
    args = [x, h, wu, wd, gpost]
    if cast is not None:
        w, lc = cast
        rows = w.shape[1] // ((m // tm) * n_f)
        in_specs.append(pl.BlockSpec((None, rows, w.shape[2]),
                                     lambda i, f: (lc, i * n_f + f, 0)))
        out_specs.append(pl.BlockSpec((rows, w.shape[2]), lambda i, f: (i * n_f + f, 0)))
        out_shape.append(jax.ShapeDtypeStruct(w.shape[1:], jnp.bfloat16))
        args.append(w)
    return pl.pallas_call(
        _ffn_kernel,
        grid=(m // tm, n_f),
        in_specs=in_specs,
        out_specs=out_specs,
        out_shape=out_shape,
        compiler_params=_params(("arbitrary", "arbitrary")),
        name="ffn",
    )(*args)


def _t5_bucket(rel):
    half = T5_BUCKETS // 2
    exact = half // 2
    ret = jnp.where(rel > 0, half, 0)
    n = jnp.abs(rel)
    large = exact + (jnp.log(jnp.maximum(n, 1).astype(jnp.float32) / exact)
                     / math.log(T5_MAX_DIST / exact) * (half - exact)).astype(jnp.int32)
    large = jnp.minimum(large, half - 1)
    return ret + jnp.where(n < exact, n, large)


def _a_bias_of_rel(table):
    scaled = table.T * LOG2E
    return lambda rel: scaled[:, jnp.clip(rel, -A_REL_CLIP, A_REL_CLIP) + A_REL_CLIP]


def _b_bias_of_rel(table):
    scaled = table.T * LOG2E
    return lambda rel: scaled[:, _t5_bucket(-rel)]


def _hankel(u, q, n):
    heads, k = u.shape
    period = q + n
    u = jnp.pad(u, ((0, 0), (0, period - k)))
    flat = jnp.tile(u, (1, q + 1))[:, :q * (period + 1)]
    return flat.reshape(heads, q, period + 1)[:, :, :n]


def _rel_bias(bias_of_rel, q_len, n_keys, k0):
    k = jnp.arange(q_len + n_keys - 1)
    u = bias_of_rel(k - (n_keys - 1) - k0)
    return _hankel(u, q_len, n_keys)[:, :, ::-1]


def _band_tile(bias_of_rel, tq, n_prev):
    window = tq + n_prev * CHUNK
    band = _rel_bias(bias_of_rel, CHUNK, (n_prev + 1) * CHUNK, -n_prev * CHUNK)
    blocks = []
    for c in range(tq // CHUNK):
        left = c * CHUNK
        right = window - left - band.shape[-1]
        blocks.append(jnp.pad(band, ((0, 0), (0, 0), (left, right)),
                              constant_values=NEG_INF))
    return jnp.concatenate(blocks, axis=1)


def _sample_bias(bias_of_rel, t_new, n_cache):
    full = _rel_bias(bias_of_rel, t_new, n_cache + t_new, -n_cache)
    full = full.reshape(-1, n_cache + t_new)
    new = jnp.pad(full[:, n_cache:], ((0, 0), (LANES - t_new, 0)),
                  constant_values=NEG_INF)
    return full[:, :n_cache], new


def _cache_t(c):
    d, s, r, h, e = c.shape
    return jnp.transpose(c, (0, 1, 3, 4, 2)).reshape(d, s, h * e, r)


def _cache_untranspose(c, heads):
    d, s, w, r = c.shape
    return jnp.transpose(c.reshape(d, s, heads, w // heads, r), (0, 1, 4, 2, 3))


def kernel(x_prompt, x_sample, cache_a_k, cache_a_v, cache_b_k, cache_b_v, w_in,
           w_a_out, w_b_out, w_out, a_rel_table, t5_table, b_sinks, g_mix_pre,
           g_mix_post, g_ffn_pre, g_ffn_post, w_up, w_down):
    depth = w_in.shape[0]
    assert depth >= 2, "the last layer's sample kernel rolls the earlier layers' caches"
    batch, seq, d = x_prompt.shape
    dec_batch, t_new, _ = x_sample.shape
    a_len = cache_a_k.shape[2]
    b_len = cache_b_k.shape[2]
    bf16 = jnp.bfloat16
    mp = batch * seq
    ms = dec_batch * t_new

    yp = x_prompt.reshape(mp, d)
    ys = x_sample.reshape(ms, d)

    w_in16 = w_in[0].astype(bf16)
    later_weights = (w_a_out, w_b_out, w_out, w_up, w_down)
    g_pre = g_mix_pre[:, None, :]
    g_post = g_mix_post[:, None, :]
    gf_pre = g_ffn_pre[:, None, :]
    gf_post = g_ffn_post[:, None, :]
    caches = tuple(_cache_t(c) for c in (cache_a_k, cache_a_v, cache_b_k, cache_b_v))

    b_of_rel = _b_bias_of_rel(t5_table)
    bias_b_tile = _band_tile(b_of_rel, B_TQ, B_LEFT_CHUNKS)
    bias_bc, bias_bn = _sample_bias(b_of_rel, t_new, b_len)

    prompt_kva, prompt_kvb, sample_kva, sample_kvb = [], [], [], []
    for l in range(depth):
        a_of_rel = _a_bias_of_rel(a_rel_table[l])
        sinks = b_sinks[l] * LOG2E

        qkv, kvb16, gates, kva32, kvb32 = _in_proj(yp, g_pre, w_in16, l, tm=1024)
        oa, wa16, wb16, wo16, wu16, wd16 = _attn_a(
            qkv, _band_tile(a_of_rel, A_TQ, A_LEFT_CHUNKS), later_weights, l, m=mp, seq=seq)
        ob = _attn_b(qkv, kvb16, bias_b_tile, sinks, m=mp, seq=seq)
        yp, hp = _merge(oa, ob, gates, yp, wa16, wb16, wo16, g_post, gf_pre, l, tm=512)
        w_in16_this = w_in16
        if l + 1 < depth:
            yp, w_in16 = _ffn(yp, hp, wu16, wd16, gf_post, l, (w_in, l + 1),
                              tm=512, tf=1024)
        else:
            yp, = _ffn(yp, hp, wu16, wd16, gf_post, l, tm=512, tf=1024)
        prompt_kva.append(kva32)
        prompt_kvb.append(kvb32)

        qkv, _, gates, kva32, kvb32 = _in_proj(ys, g_pre, w_in16_this, l, tm=ms)
        bias_ac, bias_an = _sample_bias(a_of_rel, t_new, a_len)
        sink_col = jnp.repeat(sinks, t_new)[:, None]
        if l + 1 < depth:
            oa, ob = _attn_sample(qkv, kva32, kvb32, caches, l, bias_ac, bias_an, bias_bc,
                                  bias_bn, sink_col, t_new=t_new)
            sample_kva.append(kva32)
            sample_kvb.append(kvb32)
        else:
            prev_new = (jnp.stack(sample_kva), jnp.stack(sample_kvb))
            oa, ob, aks, avs, bks, bvs = _attn_sample(
                qkv, kva32, kvb32, caches, l, bias_ac, bias_an, bias_bc, bias_bn,
                sink_col, prev_new, t_new=t_new)
        ys, hs = _merge(oa, ob, gates, ys, wa16, wb16, wo16, g_post, gf_pre, l, tm=ms)
        ys, = _ffn(ys, hs, wu16, wd16, gf_post, l, tm=ms, tf=1024)

    akp, avp, bkp, bvp = _prompt_caches(prompt_kva, prompt_kvb, batch=batch, seq=seq,
                                        na=min(a_len, seq), nb=min(b_len, seq))
    return (yp.reshape(batch, seq, d), ys.reshape(dec_batch, t_new, d),
            _cache_untranspose(akp, A_HEADS), _cache_untranspose(avp, A_HEADS),
            _cache_untranspose(bkp, B_KV_HEADS), _cache_untranspose(bvp, B_KV_HEADS),
            _cache_untranspose(aks, A_HEADS), _cache_untranspose(avs, A_HEADS),
            _cache_untranspose(bks, B_KV_HEADS), _cache_untranspose(bvs, B_KV_HEADS))
```

```python
import functools
import math

import jax
import jax.numpy as jnp
from jax import lax
from jax.experimental import pallas as pl
from jax.experimental.pallas import tpu as pltpu

D_MODEL = 2048
CHUNK = 64
HEAD_DIM = 64
A_HEADS = 16
A_WIDTH = A_HEADS * HEAD_DIM
A_LEFT_CHUNKS = 8
A_REL_CLIP = 256
B_HEADS = 16
B_KV_HEADS = 2
B_GROUP = B_HEADS // B_KV_HEADS
B_WIDTH = B_HEADS * HEAD_DIM
B_KV_WIDTH = B_KV_HEADS * HEAD_DIM
B_LEFT_CHUNKS = 2
T5_BUCKETS = 32
T5_MAX_DIST = 128
EPS = 1e-6
NEG_INF = -1e30
LOG2E = math.log2(math.e)
Q_SCALE = HEAD_DIM ** -0.5 * LOG2E

LANES = 128
MXU_COLS = 256

COL_QA = 0
COL_KA = A_WIDTH
COL_VA = 2 * A_WIDTH
COL_QB = 3 * A_WIDTH
COL_KB = 3 * A_WIDTH + B_WIDTH
COL_GA = COL_KB + 2 * B_KV_WIDTH
PROJ_TN = 4 * MXU_COLS
QKV_KA, QKV_VA, QKV_QA, QKV_QB = 0, 1, 2, 3

A_TQ = 256
B_TQ = 128
A_KBLOCKS = A_LEFT_CHUNKS * CHUNK // A_TQ + 1
B_KBLOCKS = B_LEFT_CHUNKS * CHUNK // B_TQ + 1

MERGE_SLAB = 256

VMEM_LIMIT = 56 * 1024 * 1024


def _params(sem, vmem=VMEM_LIMIT):
    return pltpu.CompilerParams(dimension_semantics=sem, vmem_limit_bytes=vmem)


def _rms_scale(x, g):
    return x * lax.rsqrt(jnp.mean(x * x, axis=-1, keepdims=True) + EPS) * g


def _layer_vec_spec(l, d):
    return pl.BlockSpec((None, 1, d), lambda *_: (l, 0, 0))


PROJ_KV_STEPS = 2
PROJ_Q_STEPS = 4
PROJ_GATE_STEP0 = 4


def _in_proj_kernel(x_ref, g_ref, *refs):
    n_w = PROJ_TN // MXU_COLS
    w_refs, wkvb_ref = refs[:n_w], refs[n_w]
    qkv_ref, kvb16_ref, gates_ref, kva32_ref, kvb32_ref, h_ref = refs[n_w + 1:]
    j = pl.program_id(1)
    tm = x_ref.shape[0]

    def tile(out16_ref, out32_ref, scale=None, rows=slice(None)):
        h = h_ref[rows]
        for k, w_ref in enumerate(w_refs):
            cols = slice(k * MXU_COLS, (k + 1) * MXU_COLS)
            acc = jnp.dot(h, w_ref[...], preferred_element_type=jnp.float32)
            if out32_ref is not None:
                out32_ref[rows, cols] = acc
            if scale is not None:
                acc = acc * scale
            out16_ref[rows, cols] = acc.astype(out16_ref.dtype)

    @pl.when(j == 0)
    def _():
        slab = tm // 2 if tm % 32 == 0 else tm
        for r0 in range(0, tm, slab):
            rows = slice(r0, r0 + slab)
            h_ref[rows] = _rms_scale(x_ref[rows], g_ref[...]).astype(h_ref.dtype)
            tile(qkv_ref, kva32_ref, rows=rows)

    @pl.when((j > 0) & (j < PROJ_KV_STEPS))
    def _():
        tile(qkv_ref, kva32_ref)

    @pl.when((j >= PROJ_KV_STEPS) & (j < PROJ_Q_STEPS))
    def _():
        tile(qkv_ref, None, Q_SCALE)

    @pl.when(j == PROJ_Q_STEPS - 1)
    def _():
        acc = jnp.dot(h_ref[...], wkvb_ref[...], preferred_element_type=jnp.float32)
        kvb16_ref[...] = acc.astype(kvb16_ref.dtype)
        kvb32_ref[...] = acc

    @pl.when(j >= PROJ_GATE_STEP0)
    def _():
        tile(gates_ref, None)


def _in_proj(x, g, w_in, l, *, tm):
    m, d = x.shape
    n_w = PROJ_TN // MXU_COLS
    n_gate = 2 * D_MODEL // PROJ_TN
    src = [c // MXU_COLS for c in (COL_KA, COL_VA, COL_QA, COL_QB)]
    ga0 = COL_GA // MXU_COLS

    def w_map(k):
        def index_map(i, j):
            base = src[-1] + k
            for step in range(len(src) - 2, -1, -1):
                base = jnp.where(j == step, src[step] + k, base)
            gate = ga0 + (j - PROJ_GATE_STEP0) * n_w + k
            return (0, jnp.where(j >= PROJ_GATE_STEP0, gate, base))
        return index_map

    kvb_n = 2 * B_KV_WIDTH
    return pl.pallas_call(
        _in_proj_kernel,
        grid=(m // tm, PROJ_GATE_STEP0 + n_gate),
        in_specs=[
            pl.BlockSpec((tm, d), lambda i, j: (i, 0)),
            _layer_vec_spec(l, d),
        ] + [pl.BlockSpec((d, MXU_COLS), w_map(k)) for k in range(n_w)]
        + [pl.BlockSpec((d, kvb_n), lambda i, j: (0, COL_KB // kvb_n))],
        out_specs=[
            pl.BlockSpec((tm, PROJ_TN), lambda i, j: (i, jnp.minimum(j, PROJ_Q_STEPS - 1))),
            pl.BlockSpec((tm, kvb_n), lambda i, j: (i, 0)),
            pl.BlockSpec((tm, PROJ_TN),
                         lambda i, j: (i, jnp.clip(j - PROJ_GATE_STEP0, 0, n_gate - 1))),
            pl.BlockSpec((tm, PROJ_TN), lambda i, j: (i, jnp.minimum(j, PROJ_KV_STEPS - 1))),
            pl.BlockSpec((tm, kvb_n), lambda i, j: (i, 0)),
        ],
        out_shape=[
            jax.ShapeDtypeStruct((m, PROJ_Q_STEPS * PROJ_TN), jnp.bfloat16),
            jax.ShapeDtypeStruct((m, kvb_n), jnp.bfloat16),
            jax.ShapeDtypeStruct((m, 2 * D_MODEL), jnp.bfloat16),
            jax.ShapeDtypeStruct((m, PROJ_KV_STEPS * PROJ_TN), jnp.float32),
            jax.ShapeDtypeStruct((m, kvb_n), jnp.float32),
        ],
        scratch_shapes=[pltpu.VMEM((tm, d), jnp.bfloat16)],
        compiler_params=_params(("parallel", "arbitrary")),
        name="in_proj",
    )(x, g, *([w_in] * (n_w + 1)))


def _low_half():
    return lax.broadcasted_iota(jnp.int32, (1, LANES), 1) < HEAD_DIM


def _start_mask(tile_in_seq, n_kblocks, tq):
    lane = lax.broadcasted_iota(jnp.int32, (1, n_kblocks * tq), 1)
    first_valid = jnp.maximum(n_kblocks - 1 - tile_in_seq, 0) * tq
    return jnp.where(lane >= first_valid, 0.0, NEG_INF)


def _attn_a_kernel(q_ref, *refs, tiles_per_seq):
    k_refs = refs[:A_KBLOCKS]
    v_refs = refs[A_KBLOCKS:2 * A_KBLOCKS]
    bias_ref = refs[2 * A_KBLOCKS]
    n_cast = (len(refs) - 2 * A_KBLOCKS - 2) // 2
    w32_refs = refs[2 * A_KBLOCKS + 1:2 * A_KBLOCKS + 1 + n_cast]
    o_ref = refs[2 * A_KBLOCKS + 1 + n_cast]
    w16_refs = refs[2 * A_KBLOCKS + 2 + n_cast:]
    tile_in_seq = pl.program_id(0) % tiles_per_seq
    low = _low_half()

    def heads(start):
        for w32_ref, w16_ref in zip(w32_refs, w16_refs):
            w16_ref[...] = w32_ref[...].astype(w16_ref.dtype)
        for p in range(A_HEADS // 2):
            sl = slice(p * LANES, (p + 1) * LANES)
            q2 = q_ref[:, sl]
            k2 = jnp.concatenate([r[:, sl] for r in k_refs], axis=0)
            v2 = jnp.concatenate([r[:, sl] for r in v_refs], axis=0)
            qs = jnp.concatenate([jnp.where(low, q2, 0), jnp.where(low, 0, q2)], axis=0)
            s_all = lax.dot_general(qs, k2, (((1,), (1,)), ((), ())),
                                    preferred_element_type=jnp.float32)
            probs = []
            for hh in range(2):
                s = s_all[hh * A_TQ:(hh + 1) * A_TQ] + bias_ref[2 * p + hh]
                if start is not None:
                    s = s + start
                s = s.astype(v2.dtype)
                probs.append(jnp.exp2(s - jnp.max(s, axis=-1, keepdims=True)))
            v_ones = jnp.concatenate([v2, jnp.ones_like(v2)], axis=1)
            r = jnp.dot(jnp.concatenate(probs, axis=0), v_ones,
                        preferred_element_type=jnp.float32)
            out = r[:, :LANES] / r[:, LANES:]
            o_ref[:, sl] = jnp.where(low, out[:A_TQ], out[A_TQ:]).astype(o_ref.dtype)

    @pl.when(tile_in_seq >= A_KBLOCKS - 1)
    def _():
        heads(None)

    @pl.when(tile_in_seq < A_KBLOCKS - 1)
    def _():
        heads(_start_mask(tile_in_seq, A_KBLOCKS, A_TQ))


def _band_kv_map(j, col, n_kblocks, tiles_per_seq):
    def index_map(t, *_):
        i = t % tiles_per_seq
        return (t - i + jnp.maximum(i - (n_kblocks - 1) + j, 0), col)
    return index_map


def _attn_a(qkv, bias, weights, l, *, m, seq):
    tiles_per_seq = seq // A_TQ
    steps = m // A_TQ
    kv_spec = lambda j, col: pl.BlockSpec(
        (A_TQ, A_WIDTH), _band_kv_map(j, col, A_KBLOCKS, tiles_per_seq))
    slab32 = lambda w: pl.BlockSpec((None, w.shape[1] // steps, w.shape[2]),
                                    lambda t: (l, t, 0))
    slab16 = lambda w: pl.BlockSpec((w.shape[1] // steps, w.shape[2]), lambda t: (t, 0))
    return pl.pallas_call(
        functools.partial(_attn_a_kernel, tiles_per_seq=tiles_per_seq),
        grid=(steps,),
        in_specs=[pl.BlockSpec((A_TQ, A_WIDTH), lambda t: (t, QKV_QA))]
        + [kv_spec(j, QKV_KA) for j in range(A_KBLOCKS)]
        + [kv_spec(j, QKV_VA) for j in range(A_KBLOCKS)]
        + [pl.BlockSpec((A_HEADS, A_TQ, A_KBLOCKS * A_TQ), lambda t: (l, 0, 0),
                        pipeline_mode=pl.Buffered(1))]
        + [slab32(w) for w in weights],
        out_specs=[pl.BlockSpec((A_TQ, A_WIDTH), lambda t: (t, 0))]
        + [slab16(w) for w in weights],
        out_shape=[jax.ShapeDtypeStruct((m, A_WIDTH), jnp.bfloat16)]
        + [jax.ShapeDtypeStruct(w.shape[1:], jnp.bfloat16) for w in weights],
        compiler_params=_params(("parallel",)),
        name="attn_a",
    )(qkv, *([qkv] * (2 * A_KBLOCKS)), bias, *weights)


def _dup_half(x, g):
    swapped = pltpu.roll(x, HEAD_DIM, 1)
    low = _low_half()
    return jnp.where(low, x, swapped) if g == 0 else jnp.where(low, swapped, x)


def _attn_b_kernel(sink_ref, q_ref, *refs, tiles_per_seq):
    k_refs = refs[:B_KBLOCKS]
    v_refs = refs[B_KBLOCKS:2 * B_KBLOCKS]
    bias_ref, o_ref = refs[2 * B_KBLOCKS:]
    start = _start_mask(pl.program_id(0) % tiles_per_seq, B_KBLOCKS, B_TQ)
    low = _low_half()
    k2 = jnp.concatenate([r[...] for r in k_refs], axis=0).astype(jnp.float32)
    v2 = jnp.concatenate([r[...] for r in v_refs], axis=0).astype(jnp.float32)
    pairs = B_GROUP // 2
    for g in range(B_KV_HEADS):
        kd = _dup_half(k2, g).astype(jnp.bfloat16)
        vd = _dup_half(v2, g).astype(jnp.bfloat16)
        stacked = []
        for p in range(pairs):
            c0 = (g * pairs + p) * LANES
            q2 = q_ref[:, c0:c0 + LANES]
            stacked.append(jnp.where(low, q2, 0))
            stacked.append(jnp.where(low, 0, q2))
        qs = jnp.concatenate(stacked, axis=0)
        s_all = lax.dot_general(qs, kd, (((1,), (1,)), ((), ())),
                                preferred_element_type=jnp.float32)
        probs, sink_terms = [], []
        for hl in range(B_GROUP):
            h = g * B_GROUP + hl
            s = s_all[hl * B_TQ:(hl + 1) * B_TQ] + bias_ref[h] + start
            sink = sink_ref[h]
            mx = jnp.maximum(jnp.max(s, axis=-1, keepdims=True), sink)
            probs.append(jnp.exp2(s - mx).astype(jnp.bfloat16))
            sink_terms.append(jnp.exp2(sink - mx))
        v_ones = jnp.concatenate([vd, jnp.ones_like(vd)], axis=1)
        r = jnp.dot(jnp.concatenate(probs, axis=0), v_ones,
                    preferred_element_type=jnp.float32)
        outs = []
        for hl in range(B_GROUP):
            rh = r[hl * B_TQ:(hl + 1) * B_TQ]
            outs.append(rh[:, :LANES] / (rh[:, LANES:] + sink_terms[hl]))
        for p in range(pairs):
            c0 = (g * pairs + p) * LANES
            o_ref[:, c0:c0 + LANES] = jnp.where(
                low, outs[2 * p], outs[2 * p + 1]).astype(o_ref.dtype)


def _attn_b(qkv, kvb16, bias, sinks, *, m, seq):
    tiles_per_seq = seq // B_TQ
    kv_spec = lambda j, col: pl.BlockSpec(
        (B_TQ, LANES), _band_kv_map(j, col, B_KBLOCKS, tiles_per_seq))
    return pl.pallas_call(
        functools.partial(_attn_b_kernel, tiles_per_seq=tiles_per_seq),
        grid_spec=pltpu.PrefetchScalarGridSpec(
            num_scalar_prefetch=1,
            grid=(m // B_TQ,),
            in_specs=[pl.BlockSpec((B_TQ, B_WIDTH), lambda t, s: (t, QKV_QB))]
            + [kv_spec(j, 0) for j in range(B_KBLOCKS)]
            + [kv_spec(j, 1) for j in range(B_KBLOCKS)]
            + [pl.BlockSpec((B_HEADS, B_TQ, B_KBLOCKS * B_TQ), lambda t, s: (0, 0, 0))],
            out_specs=pl.BlockSpec((B_TQ, B_WIDTH), lambda t, s: (t, 0)),
        ),
        out_shape=jax.ShapeDtypeStruct((m, B_WIDTH), jnp.bfloat16),
        compiler_params=_params(("parallel",)),
        name="attn_b",
    )(sinks, qkv, *([kvb16] * (2 * B_KBLOCKS)), bias)


def _pad_top(x, t_new):
    return jnp.concatenate([jnp.zeros((LANES - t_new, x.shape[1]), x.dtype), x], axis=0)


def _roll_in(cache_t, new_rows, out_ref, layer, t_new):
    n = cache_t.shape[1]
    shifted = pltpu.roll(cache_t, n - t_new, 1)
    lane = lax.broadcasted_iota(jnp.int32, (1, LANES), 1)
    tail = jnp.where(lane < LANES - t_new, shifted[:, n - LANES:],
                     _pad_top(new_rows, t_new).T)
    if n > LANES:
        out_ref[layer, :, :n - LANES] = shifted[:, :n - LANES]
    out_ref[layer, :, n - LANES:] = tail


def _attn_sample_kernel(qa_ref, qb_ref, kan_ref, van_ref, kbn_ref, vbn_ref,
                        cak_ref, cav_ref, cbk_ref, cbv_ref,
                        bias_ac_ref, bias_an_ref, bias_bc_ref, bias_bn_ref, sink_ref,
                        *refs, t_new, layer, rolling):
    if rolling:
        prev_refs, (oa_ref, ob_ref), out_refs = refs[:4], refs[4:6], refs[6:]
        for cache_ref, new_ref, prev_ref, out_ref in zip(
                (cak_ref, cav_ref, cbk_ref, cbv_ref), (kan_ref, van_ref, kbn_ref, vbn_ref),
                prev_refs, out_refs):
            for l2 in range(layer + 1):
                rows_new = new_ref[...] if l2 == layer else prev_ref[l2]
                _roll_in(cache_ref[l2], rows_new, out_ref, l2, t_new)
        cak_ref, cav_ref, cbk_ref, cbv_ref = (
            r.at[layer] for r in (cak_ref, cav_ref, cbk_ref, cbv_ref))
    else:
        oa_ref, ob_ref = refs
    heads = A_HEADS
    rows = heads * t_new
    width = heads * HEAD_DIM
    bf16 = jnp.bfloat16
    row_head = lax.broadcasted_iota(jnp.int32, (rows, width), 0) // t_new
    lane_head = lax.broadcasted_iota(jnp.int32, (rows, width), 1) // HEAD_DIM
    own = row_head == lane_head

    def stack_q(q):
        return jnp.where(own, jnp.concatenate([q] * heads, axis=0), 0)

    def nt_dot(a, b):
        return lax.dot_general(a, b, (((1,), (1,)), ((), ())),
                               preferred_element_type=jnp.float32)

    def finish(s_c, s_n, vt_c, v_n, sink):
        mx = jnp.maximum(jnp.max(s_c, axis=-1, keepdims=True),
                         jnp.max(s_n, axis=-1, keepdims=True))
        if sink is not None:
            mx = jnp.maximum(mx, sink)
        e_c = jnp.exp2(s_c - mx)
        e_n = jnp.exp2(s_n - mx)
        denom = jnp.sum(e_c, axis=-1, keepdims=True) + jnp.sum(e_n, axis=-1, keepdims=True)
        if sink is not None:
            denom = denom + jnp.exp2(sink - mx)
        o = nt_dot(e_c.astype(bf16), vt_c)
        o = o + jnp.dot(e_n.astype(bf16), v_n, preferred_element_type=jnp.float32)
        o = jnp.where(own, o / denom, 0.0)
        return jnp.sum(o.reshape(heads, t_new, width), axis=0)

    k_n = _pad_top(kan_ref[...], t_new).astype(bf16)
    v_n = _pad_top(van_ref[...], t_new).astype(bf16)
    qs = stack_q(qa_ref[...])
    s_c = jnp.dot(qs, cak_ref[...].astype(bf16),
                  preferred_element_type=jnp.float32) + bias_ac_ref[...]
    s_n = nt_dot(qs, k_n) + bias_an_ref[...]
    oa_ref[...] = finish(s_c, s_n, cav_ref[...].astype(bf16), v_n, None).astype(oa_ref.dtype)

    def expansion(shape, src_axis):
        src = lax.broadcasted_iota(jnp.int32, shape, src_axis)
        dst = lax.broadcasted_iota(jnp.int32, shape, 1 - src_axis)
        return ((src % HEAD_DIM == dst % HEAD_DIM)
                & (src // HEAD_DIM == dst // (B_GROUP * HEAD_DIM))).astype(bf16)

    def widen(x):
        return jnp.dot(x.astype(bf16), expansion((B_KV_WIDTH, width), 0),
                       preferred_element_type=jnp.float32).astype(bf16)

    def widen_t(xt):
        return jnp.dot(expansion((width, B_KV_WIDTH), 1), xt.astype(bf16),
                       preferred_element_type=jnp.float32).astype(bf16)

    k_n = widen(_pad_top(kbn_ref[...], t_new))
    v_n = widen(_pad_top(vbn_ref[...], t_new))
    qs = stack_q(qb_ref[...])
    s_c = jnp.dot(qs, widen_t(cbk_ref[...]),
                  preferred_element_type=jnp.float32) + bias_bc_ref[...]
    s_n = nt_dot(qs, k_n) + bias_bn_ref[...]
    ob_ref[...] = finish(s_c, s_n, widen_t(cbv_ref[...]), v_n, sink_ref[...]).astype(ob_ref.dtype)


def _attn_sample(qkv, kva32, kvb32, caches, l, bias_ac, bias_an, bias_bc, bias_bn,
                 sink_col, prev_new=None, *, t_new):
    m = qkv.shape[0]
    cak, cav, cbk, cbv = caches
    a_len, b_len = cak.shape[-1], cbk.shape[-1]
    rows = A_HEADS * t_new
    rolling = prev_new is not None
    const = lambda shape: pl.BlockSpec(shape, lambda b: (0,) * len(shape))
    if rolling:
        assert l == cak.shape[0] - 1 and prev_new[0].shape[0] == l
        cache_spec = lambda c: pl.BlockSpec((c.shape[0], None) + c.shape[2:],
                                            lambda b: (0, b, 0, 0))
    else:
        cache_spec = lambda c: pl.BlockSpec((None, None) + c.shape[2:],
                                            lambda b: (l, b, 0, 0))
    in_specs = [
        pl.BlockSpec((t_new, A_WIDTH), lambda b: (b, QKV_QA)),
        pl.BlockSpec((t_new, B_WIDTH), lambda b: (b, QKV_QB)),
        pl.BlockSpec((t_new, A_WIDTH), lambda b: (b, 0)),
        pl.BlockSpec((t_new, A_WIDTH), lambda b: (b, 1)),
        pl.BlockSpec((t_new, B_KV_WIDTH), lambda b: (b, 0)),
        pl.BlockSpec((t_new, B_KV_WIDTH), lambda b: (b, 1)),
        cache_spec(cak), cache_spec(cav), cache_spec(cbk), cache_spec(cbv),
        pl.BlockSpec((rows, a_len), lambda b: (l, 0)),
        pl.BlockSpec((rows, LANES), lambda b: (l, 0)),
        const((rows, b_len)), const((rows, LANES)),
        const((rows, 1)),
    ]
    args = [qkv, qkv, kva32, kva32, kvb32, kvb32, cak, cav, cbk, cbv,
            bias_ac, bias_an, bias_bc, bias_bn, sink_col]
    out_specs = [
        pl.BlockSpec((t_new, A_WIDTH), lambda b: (b, 0)),
        pl.BlockSpec((t_new, B_WIDTH), lambda b: (b, 0)),
    ]
    out_shape = [
        jax.ShapeDtypeStruct((m, A_WIDTH), jnp.bfloat16),
        jax.ShapeDtypeStruct((m, B_WIDTH), jnp.bfloat16),
    ]
    if rolling:
        new_a, new_b = prev_new
        prev_spec = lambda width, col: pl.BlockSpec((l, t_new, width), lambda b: (0, b, col))
        in_specs += [prev_spec(A_WIDTH, 0), prev_spec(A_WIDTH, 1),
                     prev_spec(B_KV_WIDTH, 0), prev_spec(B_KV_WIDTH, 1)]
        args += [new_a, new_a, new_b, new_b]
        out_specs += [cache_spec(c) for c in caches]
        out_shape += [jax.ShapeDtypeStruct(c.shape, jnp.float32) for c in caches]
    return pl.pallas_call(
        functools.partial(_attn_sample_kernel, t_new=t_new, layer=l, rolling=rolling),
        grid=(m // t_new,),
        in_specs=in_specs,
        out_specs=out_specs,
        out_shape=out_shape,
        compiler_params=_params(("parallel",)),
        name="attn_sample",
    )(*args)


def _prompt_caches_kernel(*refs, depth):
    in_refs, (ak_ref, av_ref, bk_ref, bv_ref) = refs[:4 * depth], refs[4 * depth:]
    for k in range(depth):
        @pl.when(pl.program_id(0) == k)
        def _(k=k):
            ka_ref, va_ref, kb_ref, vb_ref = in_refs[4 * k:4 * k + 4]
            ak_ref[...] = ka_ref[...].T
            av_ref[...] = va_ref[...].T
            bk_ref[...] = kb_ref[...].T
            bv_ref[...] = vb_ref[...].T


def _prompt_caches(kva32s, kvb32s, *, batch, seq, na, nb):
    depth = len(kva32s)

    def rows_map(k, tile, col):
        def index_map(l, b):
            bb = jnp.where(l < k, 0, jnp.where(l > k, batch - 1, b))
            return ((bb + 1) * tile - 1, col)
        return index_map

    in_specs, args = [], []
    for k in range(depth):
        in_specs += [
            pl.BlockSpec((na, A_WIDTH), rows_map(k, seq // na, 0)),
            pl.BlockSpec((na, A_WIDTH), rows_map(k, seq // na, 1)),
            pl.BlockSpec((nb, B_KV_WIDTH), rows_map(k, seq // nb, 0)),
            pl.BlockSpec((nb, B_KV_WIDTH), rows_map(k, seq // nb, 1)),
        ]
        args += [kva32s[k], kva32s[k], kvb32s[k], kvb32s[k]]
    out_shapes = [(depth, batch, A_WIDTH, na)] * 2 + [(depth, batch, B_KV_WIDTH, nb)] * 2
    return pl.pallas_call(
        functools.partial(_prompt_caches_kernel, depth=depth),
        grid=(depth, batch),
        in_specs=in_specs,
        out_specs=[pl.BlockSpec((None, None) + s[2:], lambda l, b: (l, b, 0, 0))
                   for s in out_shapes],
        out_shape=[jax.ShapeDtypeStruct(s, jnp.float32) for s in out_shapes],
        compiler_params=_params(("arbitrary", "arbitrary")),
        name="prompt_caches",
    )(*args)


def _merge_kernel(oa_ref, ob_ref, ga_ref, gb_ref, x_ref, wa_ref, wb_ref, wo_ref,
                  g_ref, gnext_ref, y_ref, h_ref):
    tm = x_ref.shape[0]
    slab = min(tm, MERGE_SLAB)
    for r0 in range(0, tm, slab):
        rows = slice(r0, r0 + slab)
        ta = jnp.dot(oa_ref[rows], wa_ref[...], preferred_element_type=jnp.float32)
        tb = jnp.dot(ob_ref[rows], wb_ref[...], preferred_element_type=jnp.float32)
        mixed = (jax.nn.sigmoid(ga_ref[rows].astype(jnp.float32)) * ta
                 + jax.nn.sigmoid(gb_ref[rows].astype(jnp.float32)) * tb)
        z = jnp.dot(mixed.astype(jnp.bfloat16), wo_ref[...],
                    preferred_element_type=jnp.float32)
        y = x_ref[rows] + _rms_scale(z, g_ref[...])
        y_ref[rows] = y
        h_ref[rows] = _rms_scale(y, gnext_ref[...]).astype(h_ref.dtype)


def _merge(oa, ob, gates, x, wa, wb, wo, g, g_next, l, *, tm):
    m, d = x.shape
    resident = lambda w: pl.BlockSpec(w.shape, lambda i: (0, 0),
                                      pipeline_mode=pl.Buffered(1))
    return pl.pallas_call(
        _merge_kernel,
        grid=(m // tm,),
        in_specs=[
            pl.BlockSpec((tm, A_WIDTH), lambda i: (i, 0)),
            pl.BlockSpec((tm, B_WIDTH), lambda i: (i, 0)),
            pl.BlockSpec((tm, d), lambda i: (i, 0)),
            pl.BlockSpec((tm, d), lambda i: (i, 1)),
            pl.BlockSpec((tm, d), lambda i: (i, 0)),
            resident(wa), resident(wb), resident(wo),
            _layer_vec_spec(l, d),
            _layer_vec_spec(l, d),
        ],
        out_specs=[pl.BlockSpec((tm, d), lambda i: (i, 0))] * 2,
        out_shape=[jax.ShapeDtypeStruct((m, d), jnp.float32),
                   jax.ShapeDtypeStruct((m, d), jnp.bfloat16)],
        compiler_params=_params(("parallel",)),
        name="merge",
    )(oa, ob, gates, gates, x, wa, wb, wo, g, g_next)


def _ffn_kernel(x_ref, h_ref, wu_ref, wd_ref, gpost_ref, *refs):
    f = pl.program_id(1)
    if len(refs) == 3:
        w32_ref, y_ref, w16_ref = refs
    else:
        (y_ref,), w32_ref, w16_ref = refs, None, None

    last = pl.num_programs(1) - 1

    @pl.when(f == 0)
    def _():
        y_ref[...] = jnp.zeros_like(y_ref)

    def round_w_in_slab():
        if w32_ref is not None:
            w16_ref[...] = w32_ref[...].astype(w16_ref.dtype)

    def partial(rows):
        u = jnp.dot(h_ref[rows], wu_ref[...], preferred_element_type=jnp.float32)
        u = jnp.square(jnp.maximum(u, 0.0)).astype(jnp.bfloat16)
        return y_ref[rows] + jnp.dot(u, wd_ref[...], preferred_element_type=jnp.float32)

    @pl.when(f < last)
    def _():
        round_w_in_slab()
        y_ref[...] = partial(slice(None))

    @pl.when(f == last)
    def _():
        round_w_in_slab()
        tm = x_ref.shape[0]
        slab = tm // 2 if tm % 32 == 0 else tm
        for r0 in range(0, tm, slab):
            rows = slice(r0, r0 + slab)
            y_ref[rows] = x_ref[rows] + _rms_scale(partial(rows), gpost_ref[...])


def _ffn(x, h, wu, wd, gpost, l, cast=None, *, tm, tf):
    m, d = x.shape
    dff = wu.shape[-1]
    n_f = dff // tf
    in_specs = [
        pl.BlockSpec((tm, d), lambda i, f: (i, 0)),
        pl.BlockSpec((tm, d), lambda i, f: (i, 0)),
        pl.BlockSpec((d, tf), lambda i, f: (0, f)),
        pl.BlockSpec((tf, d), lambda i, f: (f, 0)),
        _layer_vec_spec(l, d),
    ]
    out_specs = [pl.BlockSpec((tm, d), lambda i, f: (i, 0))]
    out_shape = [jax.ShapeDtypeStruct((m, d), jnp.float32)]
    args = [x, h, wu, wd, gpost]
    if cast is not None:
        w, lc = cast
        rows = w.shape[1] // ((m // tm) * n_f)
        in_specs.append(pl.BlockSpec((None, rows, w.shape[2]),
                                     lambda i, f: (lc, i * n_f + f, 0)))
        out_specs.append(pl.BlockSpec((rows, w.shape[2]), lambda i, f: (i * n_f + f, 0)))
        out_shape.append(jax.ShapeDtypeStruct(w.shape[1:], jnp.bfloat16))
        args.append(w)
    return pl.pallas_call(
        _ffn_kernel,
        grid=(m // tm, n_f),
        in_specs=in_specs,
        out_specs=out_specs,
        out_shape=out_shape,
        compiler_params=_params(("arbitrary", "arbitrary")),
        name="ffn",
    )(*args)


def _t5_bucket(rel):
    half = T5_BUCKETS // 2
    exact = half // 2
    ret = jnp.where(rel > 0, half, 0)
    n = jnp.abs(rel)
    large = exact + (jnp.log(jnp.maximum(n, 1).astype(jnp.float32) / exact)
                     / math.log(T5_MAX_DIST / exact) * (half - exact)).astype(jnp.int32)
    large = jnp.minimum(large, half - 1)
    return ret + jnp.where(n < exact, n, large)


def _a_bias_of_rel(table):
    scaled = table.T * LOG2E
    return lambda rel: scaled[:, jnp.clip(rel, -A_REL_CLIP, A_REL_CLIP) + A_REL_CLIP]


def _b_bias_of_rel(table):
    scaled = table.T * LOG2E
    return lambda rel: scaled[:, _t5_bucket(-rel)]


def _hankel(u, q, n):
    heads, k = u.shape
    period = q + n
    u = jnp.pad(u, ((0, 0), (0, period - k)))
    flat = jnp.tile(u, (1, q + 1))[:, :q * (period + 1)]
    return flat.reshape(heads, q, period + 1)[:, :, :n]


def _rel_bias(bias_of_rel, q_len, n_keys, k0):
    k = jnp.arange(q_len + n_keys - 1)
    u = bias_of_rel(k - (n_keys - 1) - k0)
    return _hankel(u, q_len, n_keys)[:, :, ::-1]


def _band_tile(bias_of_rel, tq, n_prev):
    window = tq + n_prev * CHUNK
    band = _rel_bias(bias_of_rel, CHUNK, (n_prev + 1) * CHUNK, -n_prev * CHUNK)
    blocks = []
    for c in range(tq // CHUNK):
        left = c * CHUNK
        right = window - left - band.shape[-1]
        blocks.append(jnp.pad(band, ((0, 0), (0, 0), (left, right)),
                              constant_values=NEG_INF))
    return jnp.concatenate(blocks, axis=1)


def _sample_bias(bias_of_rel, t_new, n_cache):
    full = _rel_bias(bias_of_rel, t_new, n_cache + t_new, -n_cache)
    full = full.reshape(-1, n_cache + t_new)
    new = jnp.pad(full[:, n_cache:], ((0, 0), (LANES - t_new, 0)),
                  constant_values=NEG_INF)
    return full[:, :n_cache], new


def _cache_t(c):
    d, s, r, h, e = c.shape
    return jnp.transpose(c, (0, 1, 3, 4, 2)).reshape(d, s, h * e, r)


def _cache_untranspose(c, heads):
    d, s, w, r = c.shape
    return jnp.transpose(c.reshape(d, s, heads, w // heads, r), (0, 1, 4, 2, 3))


def kernel(x_prompt, x_sample, cache_a_k, cache_a_v, cache_b_k, cache_b_v, w_in,
           w_a_out, w_b_out, w_out, a_rel_table, t5_table, b_sinks, g_mix_pre,
           g_mix_post, g_ffn_pre, g_ffn_post, w_up, w_down):
    depth = w_in.shape[0]
    assert depth >= 2, "the last layer's sample kernel rolls the earlier layers' caches"
    batch, seq, d = x_prompt.shape
    dec_batch, t_new, _ = x_sample.shape
    a_len = cache_a_k.shape[2]
    b_len = cache_b_k.shape[2]
    bf16 = jnp.bfloat16
    mp = batch * seq
    ms = dec_batch * t_new

    yp = x_prompt.reshape(mp, d)
    ys = x_sample.reshape(ms, d)

    w_in16 = w_in[0].astype(bf16)
    later_weights = (w_a_out, w_b_out, w_out, w_up, w_down)
    g_pre = g_mix_pre[:, None, :]
    g_post = g_mix_post[:, None, :]
    gf_pre = g_ffn_pre[:, None, :]
    gf_post = g_ffn_post[:, None, :]
    caches = tuple(_cache_t(c) for c in (cache_a_k, cache_a_v, cache_b_k, cache_b_v))

    b_of_rel = _b_bias_of_rel(t5_table)
    bias_b_tile = _band_tile(b_of_rel, B_TQ, B_LEFT_CHUNKS)
    bias_bc, bias_bn = _sample_bias(b_of_rel, t_new, b_len)
    a_of_rel = _a_bias_of_rel(jnp.concatenate([a_rel_table[l] for l in range(depth)], axis=1))
    bias_a_tile = _band_tile(a_of_rel, A_TQ, A_LEFT_CHUNKS)
    bias_ac, bias_an = _sample_bias(a_of_rel, t_new, a_len)

    prompt_kva, prompt_kvb, sample_kva, sample_kvb = [], [], [], []
    for l in range(depth):
        sinks = b_sinks[l] * LOG2E

        qkv, kvb16, gates, kva32, kvb32 = _in_proj(yp, g_pre, w_in16, l, tm=1024)
        oa, wa16, wb16, wo16, wu16, wd16 = _attn_a(
            qkv, bias_a_tile, later_weights, l, m=mp, seq=seq)
        ob = _attn_b(qkv, kvb16, bias_b_tile, sinks, m=mp, seq=seq)
        yp, hp = _merge(oa, ob, gates, yp, wa16, wb16, wo16, g_post, gf_pre, l, tm=512)
        w_in16_this = w_in16
        if l + 1 < depth:
            yp, w_in16 = _ffn(yp, hp, wu16, wd16, gf_post, l, (w_in, l + 1),
                              tm=512, tf=1024)
        else:
            yp, = _ffn(yp, hp, wu16, wd16, gf_post, l, tm=512, tf=1024)
        prompt_kva.append(kva32)
        prompt_kvb.append(kvb32)

        qkv, _, gates, kva32, kvb32 = _in_proj(ys, g_pre, w_in16_this, l, tm=ms)
        sink_col = jnp.repeat(sinks, t_new)[:, None]
        if l + 1 < depth:
            oa, ob = _attn_sample(qkv, kva32, kvb32, caches, l, bias_ac, bias_an, bias_bc,
                                  bias_bn, sink_col, t_new=t_new)
            sample_kva.append(kva32)
            sample_kvb.append(kvb32)
        else:
            prev_new = (jnp.stack(sample_kva), jnp.stack(sample_kvb))
            oa, ob, aks, avs, bks, bvs = _attn_sample(
                qkv, kva32, kvb32, caches, l, bias_ac, bias_an, bias_bc, bias_bn,
                sink_col, prev_new, t_new=t_new)
        ys, hs = _merge(oa, ob, gates, ys, wa16, wb16, wo16, g_post, gf_pre, l, tm=ms)
        ys, = _ffn(ys, hs, wu16, wd16, gf_post, l, tm=ms, tf=1024)

    akp, avp, bkp, bvp = _prompt_caches(prompt_kva, prompt_kvb, batch=batch, seq=seq,
                                        na=min(a_len, seq), nb=min(b_len, seq))
    return (yp.reshape(batch, seq, d), ys.reshape(dec_batch, t_new, d),
            _cache_untranspose(akp, A_HEADS), _cache_untranspose(avp, A_HEADS),
            _cache_untranspose(bkp, B_KV_HEADS), _cache_untranspose(bvp, B_KV_HEADS),
            _cache_untranspose(aks, A_HEADS), _cache_untranspose(avs, A_HEADS),
            _cache_untranspose(bks, B_KV_HEADS), _cache_untranspose(bvs, B_KV_HEADS))
```

```python
import functools
import math

import jax
import jax.numpy as jnp
from jax import lax
from jax.experimental import pallas as pl
from jax.experimental.pallas import tpu as pltpu

D_MODEL = 2048
CHUNK = 64
HEAD_DIM = 64
A_HEADS = 16
A_WIDTH = A_HEADS * HEAD_DIM
A_LEFT_CHUNKS = 8
A_REL_CLIP = 256
B_HEADS = 16
B_KV_HEADS = 2
B_GROUP = B_HEADS // B_KV_HEADS
B_WIDTH = B_HEADS * HEAD_DIM
B_KV_WIDTH = B_KV_HEADS * HEAD_DIM
B_LEFT_CHUNKS = 2
T5_BUCKETS = 32
T5_MAX_DIST = 128
EPS = 1e-6
NEG_INF = -1e30
LOG2E = math.log2(math.e)
Q_SCALE = HEAD_DIM ** -0.5 * LOG2E

LANES = 128
MXU_COLS = 256

COL_QA = 0
COL_KA = A_WIDTH
COL_VA = 2 * A_WIDTH
COL_QB = 3 * A_WIDTH
COL_KB = 3 * A_WIDTH + B_WIDTH
COL_GA = COL_KB + 2 * B_KV_WIDTH
PROJ_TN = 4 * MXU_COLS
QKV_KA, QKV_VA, QKV_QA, QKV_QB = 0, 1, 2, 3

A_TQ = 256
B_TQ = 128
A_KBLOCKS = A_LEFT_CHUNKS * CHUNK // A_TQ + 1
B_KBLOCKS = B_LEFT_CHUNKS * CHUNK // B_TQ + 1

MERGE_SLAB = 256

VMEM_LIMIT = 56 * 1024 * 1024


def _params(sem, vmem=VMEM_LIMIT):
    return pltpu.CompilerParams(dimension_semantics=sem, vmem_limit_bytes=vmem)


def _rms_scale(x, g):
    return x * lax.rsqrt(jnp.mean(x * x, axis=-1, keepdims=True) + EPS) * g


def _layer_vec_spec(l, d):
    return pl.BlockSpec((None, 1, d), lambda *_: (l, 0, 0))


PROJ_KV_STEPS = 2
PROJ_Q_STEPS = 4
PROJ_GATE_STEP0 = 4


def _proj_first_blocks():
    n_w = PROJ_TN // MXU_COLS
    blocks = [c // MXU_COLS for c in (COL_KA, COL_VA, COL_QA, COL_QB)]
    return blocks + [COL_GA // MXU_COLS + g * n_w for g in range(2 * D_MODEL // PROJ_TN)]


def _in_proj_kernel(x_ref, g_ref, *refs, from_f32):
    n_w = PROJ_TN // MXU_COLS
    w_refs, wkvb_ref = refs[:n_w], refs[n_w]
    qkv_ref, kvb16_ref, gates_ref, kva32_ref, kvb32_ref = refs[n_w + 1:n_w + 6]
    pack_refs, h_ref = refs[n_w + 6:-1], refs[-1]
    j = pl.program_id(1)
    tm = x_ref.shape[0]

    def weight(k):
        w_ref = wkvb_ref if k == n_w else w_refs[k]
        if not from_f32:
            return w_ref[...]
        w = w_ref[...].astype(jnp.bfloat16)
        pack_refs[k][...] = w
        return w

    def tile(out16_ref, out32_ref, scale=None, rows=slice(None)):
        h = h_ref[rows]
        for k in range(n_w):
            cols = slice(k * MXU_COLS, (k + 1) * MXU_COLS)
            acc = jnp.dot(h, weight(k), preferred_element_type=jnp.float32)
            if out32_ref is not None:
                out32_ref[rows, cols] = acc
            if scale is not None:
                acc = acc * scale
            out16_ref[rows, cols] = acc.astype(out16_ref.dtype)

    @pl.when(j == 0)
    def _():
        slab = tm // 2 if tm >= 2 * MXU_COLS else tm
        for r0 in range(0, tm, slab):
            rows = slice(r0, r0 + slab)
            h_ref[rows] = _rms_scale(x_ref[rows], g_ref[...]).astype(h_ref.dtype)
            tile(qkv_ref, kva32_ref, rows=rows)

    @pl.when((j > 0) & (j < PROJ_KV_STEPS))
    def _():
        tile(qkv_ref, kva32_ref)

    @pl.when((j >= PROJ_KV_STEPS) & (j < PROJ_Q_STEPS))
    def _():
        tile(qkv_ref, None, Q_SCALE)

    @pl.when(j == PROJ_Q_STEPS - 1)
    def _():
        acc = jnp.dot(h_ref[...], weight(n_w), preferred_element_type=jnp.float32)
        kvb16_ref[...] = acc.astype(kvb16_ref.dtype)
        kvb32_ref[...] = acc

    @pl.when(j >= PROJ_GATE_STEP0)
    def _():
        tile(gates_ref, None)


def _in_proj(x, g, w, l, *, tm, from_f32=False):
    m, d = x.shape
    n_w = PROJ_TN // MXU_COLS
    n_gate = 2 * D_MODEL // PROJ_TN
    steps = PROJ_GATE_STEP0 + n_gate
    first = _proj_first_blocks()
    kvb_n = 2 * B_KV_WIDTH

    def f32_map(k):
        def index_map(i, j):
            blk = first[-1] + k
            for step in range(steps - 2, -1, -1):
                blk = jnp.where(j == step, first[step] + k, blk)
            return (l, 0, blk)
        return index_map

    packed_spec = pl.BlockSpec((d, MXU_COLS), lambda i, j: (0, j))
    packed_kvb_spec = pl.BlockSpec((d, kvb_n), lambda i, j: (0, 0))
    if from_f32:
        assert m == tm, "the packed weights are written once per row tile"
        w_specs = [pl.BlockSpec((None, d, MXU_COLS), f32_map(k)) for k in range(n_w)]
        w_specs.append(pl.BlockSpec((None, d, kvb_n), lambda i, j: (l, 0, COL_KB // kvb_n)))
        w_args = [w] * (n_w + 1)
        pack_specs = [packed_spec] * n_w + [packed_kvb_spec]
        pack_shapes = [jax.ShapeDtypeStruct((d, steps * MXU_COLS), jnp.bfloat16)] * n_w
        pack_shapes.append(jax.ShapeDtypeStruct((d, kvb_n), jnp.bfloat16))
    else:
        w_specs = [packed_spec] * n_w + [packed_kvb_spec]
        w_args = list(w)
        pack_specs, pack_shapes = [], []
    outs = pl.pallas_call(
        functools.partial(_in_proj_kernel, from_f32=from_f32),
        grid=(m // tm, steps),
        in_specs=[
            pl.BlockSpec((tm, d), lambda i, j: (i, 0)),
            _layer_vec_spec(l, d),
        ] + w_specs,
        out_specs=[
            pl.BlockSpec((tm, PROJ_TN), lambda i, j: (i, jnp.minimum(j, PROJ_Q_STEPS - 1))),
            pl.BlockSpec((tm, kvb_n), lambda i, j: (i, 0)),
            pl.BlockSpec((tm, PROJ_TN),
                         lambda i, j: (i, jnp.clip(j - PROJ_GATE_STEP0, 0, n_gate - 1))),
            pl.BlockSpec((tm, PROJ_TN), lambda i, j: (i, jnp.minimum(j, PROJ_KV_STEPS - 1))),
            pl.BlockSpec((tm, kvb_n), lambda i, j: (i, 0)),
        ] + pack_specs,
        out_shape=[
            jax.ShapeDtypeStruct((m, PROJ_Q_STEPS * PROJ_TN), jnp.bfloat16),
            jax.ShapeDtypeStruct((m, kvb_n), jnp.bfloat16),
            jax.ShapeDtypeStruct((m, 2 * D_MODEL), jnp.bfloat16),
            jax.ShapeDtypeStruct((m, PROJ_KV_STEPS * PROJ_TN), jnp.float32),
            jax.ShapeDtypeStruct((m, kvb_n), jnp.float32),
        ] + pack_shapes,
        scratch_shapes=[pltpu.VMEM((tm, d), jnp.bfloat16)],
        compiler_params=_params(("parallel", "arbitrary")),
        name="in_proj",
    )(x, g, *w_args)
    return (outs[:5], tuple(outs[5:])) if from_f32 else outs


def _low_half():
    return lax.broadcasted_iota(jnp.int32, (1, LANES), 1) < HEAD_DIM


def _start_mask(tile_in_seq, n_kblocks, tq):
    lane = lax.broadcasted_iota(jnp.int32, (1, n_kblocks * tq), 1)
    first_valid = jnp.maximum(n_kblocks - 1 - tile_in_seq, 0) * tq
    return jnp.where(lane >= first_valid, 0.0, NEG_INF)


def _pack_w_in_slab(w32_ref, pack_refs):
    slab = w32_ref[...].astype(jnp.bfloat16)
    block = lambda b: slab[:, b * MXU_COLS:(b + 1) * MXU_COLS]
    for k, pack_ref in enumerate(pack_refs[:-1]):
        pack_ref[...] = jnp.concatenate([block(b + k) for b in _proj_first_blocks()], axis=1)
    pack_refs[-1][...] = block(COL_KB // MXU_COLS)


def _attn_a_kernel(q_ref, *refs, tiles_per_seq, n_cast, packing):
    k_refs = refs[:A_KBLOCKS]
    v_refs = refs[A_KBLOCKS:2 * A_KBLOCKS]
    bias_ref = refs[2 * A_KBLOCKS]
    rest = refs[2 * A_KBLOCKS + 1:]
    w32_refs, rest = rest[:n_cast], rest[n_cast:]
    if packing:
        w_in_ref, rest = rest[0], rest[1:]
    o_ref, w16_refs, pack_refs = rest[0], rest[1:1 + n_cast], rest[1 + n_cast:]
    tile_in_seq = pl.program_id(0) % tiles_per_seq
    low = _low_half()

    def heads(start):
        for w32_ref, w16_ref in zip(w32_refs, w16_refs):
            w16_ref[...] = w32_ref[...].astype(w16_ref.dtype)
        if packing:
            _pack_w_in_slab(w_in_ref, pack_refs)
        def scores(p):
            sl = slice(p * LANES, (p + 1) * LANES)
            q2 = q_ref[:, sl]
            k2 = jnp.concatenate([r[:, sl] for r in k_refs], axis=0)
            qs = jnp.concatenate([jnp.where(low, q2, 0), jnp.where(low, 0, q2)], axis=0)
            return lax.dot_general(qs, k2, (((1,), (1,)), ((), ())),
                                   preferred_element_type=jnp.float32)

        n_pairs = A_HEADS // 2
        ahead = 2
        pending = [scores(p) for p in range(ahead)]
        for p in range(n_pairs):
            sl = slice(p * LANES, (p + 1) * LANES)
            v2 = jnp.concatenate([r[:, sl] for r in v_refs], axis=0)
            if p + ahead < n_pairs:
                pending.append(scores(p + ahead))
            s_all = pending.pop(0)
            probs = []
            for hh in range(2):
                s = s_all[hh * A_TQ:(hh + 1) * A_TQ] + bias_ref[2 * p + hh]
                if start is not None:
                    s = s + start
                s = s.astype(v2.dtype)
                probs.append(jnp.exp2(s - jnp.max(s, axis=-1, keepdims=True)))
            v_ones = jnp.concatenate([v2, jnp.ones_like(v2)], axis=1)
            r = jnp.dot(jnp.concatenate(probs, axis=0), v_ones,
                        preferred_element_type=jnp.float32)
            out = r[:, :LANES] / r[:, LANES:]
            o_ref[:, sl] = jnp.where(low, out[:A_TQ], out[A_TQ:]).astype(o_ref.dtype)

    @pl.when(tile_in_seq >= A_KBLOCKS - 1)
    def _():
        heads(None)

    @pl.when(tile_in_seq < A_KBLOCKS - 1)
    def _():
        heads(_start_mask(tile_in_seq, A_KBLOCKS, A_TQ))


def _band_kv_map(j, col, n_kblocks, tiles_per_seq):
    def index_map(t, *_):
        i = t % tiles_per_seq
        return (t - i + jnp.maximum(i - (n_kblocks - 1) + j, 0), col)
    return index_map


def _attn_a(qkv, bias, weights, l, w_in_next=None, *, m, seq):
    tiles_per_seq = seq // A_TQ
    steps = m // A_TQ
    kv_spec = lambda j, col: pl.BlockSpec(
        (A_TQ, A_WIDTH), _band_kv_map(j, col, A_KBLOCKS, tiles_per_seq))
    slab32 = lambda w, layer: pl.BlockSpec((None, w.shape[1] // steps, w.shape[2]),
                                           lambda t: (layer, t, 0))
    slab16 = lambda rows, width: pl.BlockSpec((rows // steps, width), lambda t: (t, 0))
    in_specs = ([pl.BlockSpec((A_TQ, A_WIDTH), lambda t: (t, QKV_QA))]
                + [kv_spec(j, QKV_KA) for j in range(A_KBLOCKS)]
                + [kv_spec(j, QKV_VA) for j in range(A_KBLOCKS)]
                + [pl.BlockSpec((A_HEADS, A_TQ, A_KBLOCKS * A_TQ), lambda t: (l, 0, 0),
                                pipeline_mode=pl.Buffered(1))]
                + [slab32(w, l) for w in weights])
    out_specs = ([pl.BlockSpec((A_TQ, A_WIDTH), lambda t: (t, 0))]
                 + [slab16(w.shape[1], w.shape[2]) for w in weights])
    out_shape = ([jax.ShapeDtypeStruct((m, A_WIDTH), jnp.bfloat16)]
                 + [jax.ShapeDtypeStruct(w.shape[1:], jnp.bfloat16) for w in weights])
    args = [qkv] * (2 * A_KBLOCKS + 1) + [bias, *weights]
    if w_in_next is not None:
        w_in, layer = w_in_next
        in_specs.append(slab32(w_in, layer))
        args.append(w_in)
        widths = [len(_proj_first_blocks()) * MXU_COLS] * (PROJ_TN // MXU_COLS)
        widths.append(2 * B_KV_WIDTH)
        out_specs += [slab16(w_in.shape[1], width) for width in widths]
        out_shape += [jax.ShapeDtypeStruct((w_in.shape[1], width), jnp.bfloat16)
                      for width in widths]
    return pl.pallas_call(
        functools.partial(_attn_a_kernel, tiles_per_seq=tiles_per_seq,
                          n_cast=len(weights), packing=w_in_next is not None),
        grid=(steps,),
        in_specs=in_specs,
        out_specs=out_specs,
        out_shape=out_shape,
        compiler_params=_params(("parallel",)),
        name="attn_a",
    )(*args)


def _dup_half(x, g):
    swapped = pltpu.roll(x, HEAD_DIM, 1)
    low = _low_half()
    return jnp.where(low, x, swapped) if g == 0 else jnp.where(low, swapped, x)


def _attn_b_kernel(sink_ref, q_ref, *refs, tiles_per_seq):
    k_refs = refs[:B_KBLOCKS]
    v_refs = refs[B_KBLOCKS:2 * B_KBLOCKS]
    bias_ref, o_ref = refs[2 * B_KBLOCKS:]
    start = _start_mask(pl.program_id(0) % tiles_per_seq, B_KBLOCKS, B_TQ)
    low = _low_half()
    k2 = jnp.concatenate([r[...] for r in k_refs], axis=0).astype(jnp.float32)
    v2 = jnp.concatenate([r[...] for r in v_refs], axis=0).astype(jnp.float32)
    pairs = B_GROUP // 2
    def scores(g):
        kd = _dup_half(k2, g).astype(jnp.bfloat16)
        stacked = []
        for p in range(pairs):
            c0 = (g * pairs + p) * LANES
            q2 = q_ref[:, c0:c0 + LANES]
            stacked.append(jnp.where(low, q2, 0))
            stacked.append(jnp.where(low, 0, q2))
        qs = jnp.concatenate(stacked, axis=0)
        return lax.dot_general(qs, kd, (((1,), (1,)), ((), ())),
                               preferred_element_type=jnp.float32)

    all_scores = [scores(g) for g in range(B_KV_HEADS)]
    for g in range(B_KV_HEADS):
        vd = _dup_half(v2, g).astype(jnp.bfloat16)
        s_all = all_scores[g]
        probs, sink_terms = [], []
        for hl in range(B_GROUP):
            h = g * B_GROUP + hl
            s = s_all[hl * B_TQ:(hl + 1) * B_TQ] + bias_ref[h] + start
            sink = sink_ref[h]
            mx = jnp.maximum(jnp.max(s, axis=-1, keepdims=True), sink)
            probs.append(jnp.exp2(s - mx).astype(jnp.bfloat16))
            sink_terms.append(jnp.exp2(sink - mx))
        v_ones = jnp.concatenate([vd, jnp.ones_like(vd)], axis=1)
        r = jnp.dot(jnp.concatenate(probs, axis=0), v_ones,
                    preferred_element_type=jnp.float32)
        outs = []
        for hl in range(B_GROUP):
            rh = r[hl * B_TQ:(hl + 1) * B_TQ]
            outs.append(rh[:, :LANES] / (rh[:, LANES:] + sink_terms[hl]))
        for p in range(pairs):
            c0 = (g * pairs + p) * LANES
            o_ref[:, c0:c0 + LANES] = jnp.where(
                low, outs[2 * p], outs[2 * p + 1]).astype(o_ref.dtype)


def _attn_b(qkv, kvb16, bias, sinks, *, m, seq):
    tiles_per_seq = seq // B_TQ
    kv_spec = lambda j, col: pl.BlockSpec(
        (B_TQ, LANES), _band_kv_map(j, col, B_KBLOCKS, tiles_per_seq))
    return pl.pallas_call(
        functools.partial(_attn_b_kernel, tiles_per_seq=tiles_per_seq),
        grid_spec=pltpu.PrefetchScalarGridSpec(
            num_scalar_prefetch=1,
            grid=(m // B_TQ,),
            in_specs=[pl.BlockSpec((B_TQ, B_WIDTH), lambda t, s: (t, QKV_QB))]
            + [kv_spec(j, 0) for j in range(B_KBLOCKS)]
            + [kv_spec(j, 1) for j in range(B_KBLOCKS)]
            + [pl.BlockSpec((B_HEADS, B_TQ, B_KBLOCKS * B_TQ), lambda t, s: (0, 0, 0))],
            out_specs=pl.BlockSpec((B_TQ, B_WIDTH), lambda t, s: (t, 0)),
        ),
        out_shape=jax.ShapeDtypeStruct((m, B_WIDTH), jnp.bfloat16),
        compiler_params=_params(("parallel",)),
        name="attn_b",
    )(sinks, qkv, *([kvb16] * (2 * B_KBLOCKS)), bias)


def _pad_top(x, t_new):
    return jnp.concatenate([jnp.zeros((LANES - t_new, x.shape[1]), x.dtype), x], axis=0)


def _roll_in(cache_t, new_rows, out_ref, layer, t_new):
    n = cache_t.shape[1]
    shifted = pltpu.roll(cache_t, n - t_new, 1)
    lane = lax.broadcasted_iota(jnp.int32, (1, LANES), 1)
    tail = jnp.where(lane < LANES - t_new, shifted[:, n - LANES:],
                     _pad_top(new_rows, t_new).T)
    if n > LANES:
        out_ref[layer, :, :n - LANES] = shifted[:, :n - LANES]
    out_ref[layer, :, n - LANES:] = tail


def _attn_sample_kernel(qa_ref, qb_ref, kan_ref, van_ref, kbn_ref, vbn_ref,
                        cak_ref, cav_ref, cbk_ref, cbv_ref,
                        bias_ac_ref, bias_an_ref, bias_bc_ref, bias_bn_ref, sink_ref,
                        *refs, t_new, layer, rolling):
    if rolling:
        prev_refs, (oa_ref, ob_ref), out_refs = refs[:4], refs[4:6], refs[6:]
        for cache_ref, new_ref, prev_ref, out_ref in zip(
                (cak_ref, cav_ref, cbk_ref, cbv_ref), (kan_ref, van_ref, kbn_ref, vbn_ref),
                prev_refs, out_refs):
            for l2 in range(layer + 1):
                rows_new = new_ref[...] if l2 == layer else prev_ref[l2]
                _roll_in(cache_ref[l2], rows_new, out_ref, l2, t_new)
        cak_ref, cav_ref, cbk_ref, cbv_ref = (
            r.at[layer] for r in (cak_ref, cav_ref, cbk_ref, cbv_ref))
    else:
        oa_ref, ob_ref = refs
    heads = A_HEADS
    rows = heads * t_new
    width = heads * HEAD_DIM
    bf16 = jnp.bfloat16
    row_head = lax.broadcasted_iota(jnp.int32, (rows, width), 0) // t_new
    lane_head = lax.broadcasted_iota(jnp.int32, (rows, width), 1) // HEAD_DIM
    own = row_head == lane_head

    def stack_q(q):
        return jnp.where(own, jnp.concatenate([q] * heads, axis=0), 0)

    def nt_dot(a, b):
        return lax.dot_general(a, b, (((1,), (1,)), ((), ())),
                               preferred_element_type=jnp.float32)

    def finish(s_c, s_n, vt_c, v_n, sink):
        mx = jnp.maximum(jnp.max(s_c, axis=-1, keepdims=True),
                         jnp.max(s_n, axis=-1, keepdims=True))
        if sink is not None:
            mx = jnp.maximum(mx, sink)
        e_c = jnp.exp2(s_c - mx)
        e_n = jnp.exp2(s_n - mx)
        denom = jnp.sum(e_c, axis=-1, keepdims=True) + jnp.sum(e_n, axis=-1, keepdims=True)
        if sink is not None:
            denom = denom + jnp.exp2(sink - mx)
        o = nt_dot(e_c.astype(bf16), vt_c)
        o = o + jnp.dot(e_n.astype(bf16), v_n, preferred_element_type=jnp.float32)
        o = jnp.where(own, o / denom, 0.0)
        return jnp.sum(o.reshape(heads, t_new, width), axis=0)

    k_n = _pad_top(kan_ref[...], t_new).astype(bf16)
    v_n = _pad_top(van_ref[...], t_new).astype(bf16)
    qs = stack_q(qa_ref[...])
    s_c = jnp.dot(qs, cak_ref[...].astype(bf16),
                  preferred_element_type=jnp.float32) + bias_ac_ref[...]
    s_n = nt_dot(qs, k_n) + bias_an_ref[...]
    oa_ref[...] = finish(s_c, s_n, cav_ref[...].astype(bf16), v_n, None).astype(oa_ref.dtype)

    def expansion(shape, src_axis):
        src = lax.broadcasted_iota(jnp.int32, shape, src_axis)
        dst = lax.broadcasted_iota(jnp.int32, shape, 1 - src_axis)
        return ((src % HEAD_DIM == dst % HEAD_DIM)
                & (src // HEAD_DIM == dst // (B_GROUP * HEAD_DIM))).astype(bf16)

    def widen(x):
        return jnp.dot(x.astype(bf16), expansion((B_KV_WIDTH, width), 0),
                       preferred_element_type=jnp.float32).astype(bf16)

    def widen_t(xt):
        return jnp.dot(expansion((width, B_KV_WIDTH), 1), xt.astype(bf16),
                       preferred_element_type=jnp.float32).astype(bf16)

    k_n = widen(_pad_top(kbn_ref[...], t_new))
    v_n = widen(_pad_top(vbn_ref[...], t_new))
    qs = stack_q(qb_ref[...])
    s_c = jnp.dot(qs, widen_t(cbk_ref[...]),
                  preferred_element_type=jnp.float32) + bias_bc_ref[...]
    s_n = nt_dot(qs, k_n) + bias_bn_ref[...]
    ob_ref[...] = finish(s_c, s_n, widen_t(cbv_ref[...]), v_n, sink_ref[...]).astype(ob_ref.dtype)


def _attn_sample(qkv, kva32, kvb32, caches, l, bias_ac, bias_an, bias_bc, bias_bn,
                 sink_col, prev_new=None, *, t_new):
    m = qkv.shape[0]
    cak, cav, cbk, cbv = caches
    a_len, b_len = cak.shape[-1], cbk.shape[-1]
    rows = A_HEADS * t_new
    rolling = prev_new is not None
    const = lambda shape: pl.BlockSpec(shape, lambda b: (0,) * len(shape))
    if rolling:
        assert l == cak.shape[0] - 1 and prev_new[0].shape[0] == l
        cache_spec = lambda c: pl.BlockSpec((c.shape[0], None) + c.shape[2:],
                                            lambda b: (0, b, 0, 0))
    else:
        cache_spec = lambda c: pl.BlockSpec((None, None) + c.shape[2:],
                                            lambda b: (l, b, 0, 0))
    in_specs = [
        pl.BlockSpec((t_new, A_WIDTH), lambda b: (b, QKV_QA)),
        pl.BlockSpec((t_new, B_WIDTH), lambda b: (b, QKV_QB)),
        pl.BlockSpec((t_new, A_WIDTH), lambda b: (b, 0)),
        pl.BlockSpec((t_new, A_WIDTH), lambda b: (b, 1)),
        pl.BlockSpec((t_new, B_KV_WIDTH), lambda b: (b, 0)),
        pl.BlockSpec((t_new, B_KV_WIDTH), lambda b: (b, 1)),
        cache_spec(cak), cache_spec(cav), cache_spec(cbk), cache_spec(cbv),
        pl.BlockSpec((rows, a_len), lambda b: (l, 0)),
        pl.BlockSpec((rows, LANES), lambda b: (l, 0)),
        const((rows, b_len)), const((rows, LANES)),
        const((rows, 1)),
    ]
    args = [qkv, qkv, kva32, kva32, kvb32, kvb32, cak, cav, cbk, cbv,
            bias_ac, bias_an, bias_bc, bias_bn, sink_col]
    out_specs = [
        pl.BlockSpec((t_new, A_WIDTH), lambda b: (b, 0)),
        pl.BlockSpec((t_new, B_WIDTH), lambda b: (b, 0)),
    ]
    out_shape = [
        jax.ShapeDtypeStruct((m, A_WIDTH), jnp.bfloat16),
        jax.ShapeDtypeStruct((m, B_WIDTH), jnp.bfloat16),
    ]
    if rolling:
        new_a, new_b = prev_new
        prev_spec = lambda width, col: pl.BlockSpec((l, t_new, width), lambda b: (0, b, col))
        in_specs += [prev_spec(A_WIDTH, 0), prev_spec(A_WIDTH, 1),
                     prev_spec(B_KV_WIDTH, 0), prev_spec(B_KV_WIDTH, 1)]
        args += [new_a, new_a, new_b, new_b]
        out_specs += [cache_spec(c) for c in caches]
        out_shape += [jax.ShapeDtypeStruct(c.shape, jnp.float32) for c in caches]
    return pl.pallas_call(
        functools.partial(_attn_sample_kernel, t_new=t_new, layer=l, rolling=rolling),
        grid=(m // t_new,),
        in_specs=in_specs,
        out_specs=out_specs,
        out_shape=out_shape,
        compiler_params=_params(("parallel",)),
        name="attn_sample",
    )(*args)


def _prompt_caches_kernel(*refs, depth):
    in_refs, (ak_ref, av_ref, bk_ref, bv_ref) = refs[:4 * depth], refs[4 * depth:]
    for k in range(depth):
        @pl.when(pl.program_id(0) == k)
        def _(k=k):
            ka_ref, va_ref, kb_ref, vb_ref = in_refs[4 * k:4 * k + 4]
            ak_ref[...] = ka_ref[...].T
            av_ref[...] = va_ref[...].T
            bk_ref[...] = kb_ref[...].T
            bv_ref[...] = vb_ref[...].T


def _prompt_caches(kva32s, kvb32s, *, batch, seq, na, nb):
    depth = len(kva32s)

    def rows_map(k, tile, col):
        def index_map(l, b):
            bb = jnp.where(l < k, 0, jnp.where(l > k, batch - 1, b))
            return ((bb + 1) * tile - 1, col)
        return index_map

    in_specs, args = [], []
    for k in range(depth):
        in_specs += [
            pl.BlockSpec((na, A_WIDTH), rows_map(k, seq // na, 0)),
            pl.BlockSpec((na, A_WIDTH), rows_map(k, seq // na, 1)),
            pl.BlockSpec((nb, B_KV_WIDTH), rows_map(k, seq // nb, 0)),
            pl.BlockSpec((nb, B_KV_WIDTH), rows_map(k, seq // nb, 1)),
        ]
        args += [kva32s[k], kva32s[k], kvb32s[k], kvb32s[k]]
    out_shapes = [(depth, batch, A_WIDTH, na)] * 2 + [(depth, batch, B_KV_WIDTH, nb)] * 2
    return pl.pallas_call(
        functools.partial(_prompt_caches_kernel, depth=depth),
        grid=(depth, batch),
        in_specs=in_specs,
        out_specs=[pl.BlockSpec((None, None) + s[2:], lambda l, b: (l, b, 0, 0))
                   for s in out_shapes],
        out_shape=[jax.ShapeDtypeStruct(s, jnp.float32) for s in out_shapes],
        compiler_params=_params(("arbitrary", "arbitrary")),
        name="prompt_caches",
    )(*args)


def _merge_kernel(oa_ref, ob_ref, ga_ref, gb_ref, x_ref, wa_ref, wb_ref, wo_ref,
                  g_ref, gnext_ref, y_ref, h_ref):
    tm = x_ref.shape[0]
    slab = min(tm, MERGE_SLAB)
    for r0 in range(0, tm, slab):
        rows = slice(r0, r0 + slab)
        ta = jnp.dot(oa_ref[rows], wa_ref[...], preferred_element_type=jnp.float32)
        tb = jnp.dot(ob_ref[rows], wb_ref[...], preferred_element_type=jnp.float32)
        mixed = (jax.nn.sigmoid(ga_ref[rows].astype(jnp.float32)) * ta
                 + jax.nn.sigmoid(gb_ref[rows].astype(jnp.float32)) * tb)
        z = jnp.dot(mixed.astype(jnp.bfloat16), wo_ref[...],
                    preferred_element_type=jnp.float32)
        y = x_ref[rows] + _rms_scale(z, g_ref[...])
        y_ref[rows] = y
        h_ref[rows] = _rms_scale(y, gnext_ref[...]).astype(h_ref.dtype)


def _merge(oa, ob, gates, x, wa, wb, wo, g, g_next, l, *, tm):
    m, d = x.shape
    resident = lambda w: pl.BlockSpec(w.shape, lambda i: (0, 0),
                                      pipeline_mode=pl.Buffered(1))
    return pl.pallas_call(
        _merge_kernel,
        grid=(m // tm,),
        in_specs=[
            pl.BlockSpec((tm, A_WIDTH), lambda i: (i, 0)),
            pl.BlockSpec((tm, B_WIDTH), lambda i: (i, 0)),
            pl.BlockSpec((tm, d), lambda i: (i, 0)),
            pl.BlockSpec((tm, d), lambda i: (i, 1)),
            pl.BlockSpec((tm, d), lambda i: (i, 0)),
            resident(wa), resident(wb), resident(wo),
            _layer_vec_spec(l, d),
            _layer_vec_spec(l, d),
        ],
        out_specs=[pl.BlockSpec((tm, d), lambda i: (i, 0))] * 2,
        out_shape=[jax.ShapeDtypeStruct((m, d), jnp.float32),
                   jax.ShapeDtypeStruct((m, d), jnp.bfloat16)],
        compiler_params=_params(("parallel",)),
        name="merge",
    )(oa, ob, gates, gates, x, wa, wb, wo, g, g_next)


def _ffn_kernel(x_ref, h_ref, wu_ref, wd_ref, gpost_ref, y_ref):
    f = pl.program_id(1)
    last = pl.num_programs(1) - 1

    @pl.when(f == 0)
    def _():
        y_ref[...] = jnp.zeros_like(y_ref)

    def partial(rows):
        u = jnp.dot(h_ref[rows], wu_ref[...], preferred_element_type=jnp.float32)
        u = jnp.square(jnp.maximum(u, 0.0)).astype(jnp.bfloat16)
        return y_ref[rows] + jnp.dot(u, wd_ref[...], preferred_element_type=jnp.float32)

    @pl.when(f < last)
    def _():
        y_ref[...] = partial(slice(None))

    @pl.when(f == last)
    def _():
        tm = x_ref.shape[0]
        slab = tm // 2 if tm >= 2 * MXU_COLS else tm
        for r0 in range(0, tm, slab):
            rows = slice(r0, r0 + slab)
            y_ref[rows] = x_ref[rows] + _rms_scale(partial(rows), gpost_ref[...])


def _ffn(x, h, wu, wd, gpost, l, *, tm, tf):
    m, d = x.shape
    dff = wu.shape[-1]
    return pl.pallas_call(
        _ffn_kernel,
        grid=(m // tm, dff // tf),
        in_specs=[
            pl.BlockSpec((tm, d), lambda i, f: (i, 0)),
            pl.BlockSpec((tm, d), lambda i, f: (i, 0)),
            pl.BlockSpec((d, tf), lambda i, f: (0, f)),
            pl.BlockSpec((tf, d), lambda i, f: (f, 0)),
            _layer_vec_spec(l, d),
        ],
        out_specs=pl.BlockSpec((tm, d), lambda i, f: (i, 0)),
        out_shape=jax.ShapeDtypeStruct((m, d), jnp.float32),
        compiler_params=_params(("parallel", "arbitrary")),
        name="ffn",
    )(x, h, wu, wd, gpost)


def _t5_bucket(rel):
    half = T5_BUCKETS // 2
    exact = half // 2
    ret = jnp.where(rel > 0, half, 0)
    n = jnp.abs(rel)
    large = exact + (jnp.log(jnp.maximum(n, 1).astype(jnp.float32) / exact)
                     / math.log(T5_MAX_DIST / exact) * (half - exact)).astype(jnp.int32)
    large = jnp.minimum(large, half - 1)
    return ret + jnp.where(n < exact, n, large)


def _a_bias_of_rel(table):
    scaled = table.T * LOG2E
    return lambda rel: scaled[:, jnp.clip(rel, -A_REL_CLIP, A_REL_CLIP) + A_REL_CLIP]


def _b_bias_of_rel(table):
    scaled = table.T * LOG2E
    return lambda rel: scaled[:, _t5_bucket(-rel)]


def _hankel(u, q, n):
    heads, k = u.shape
    period = q + n
    u = jnp.pad(u, ((0, 0), (0, period - k)))
    flat = jnp.tile(u, (1, q + 1))[:, :q * (period + 1)]
    return flat.reshape(heads, q, period + 1)[:, :, :n]


def _rel_bias(bias_of_rel, q_len, n_keys, k0):
    k = jnp.arange(q_len + n_keys - 1)
    u = bias_of_rel(k - (n_keys - 1) - k0)
    return _hankel(u, q_len, n_keys)[:, :, ::-1]


def _band_tile(bias_of_rel, tq, n_prev):
    window = tq + n_prev * CHUNK
    band = _rel_bias(bias_of_rel, CHUNK, (n_prev + 1) * CHUNK, -n_prev * CHUNK)
    blocks = []
    for c in range(tq // CHUNK):
        left = c * CHUNK
        right = window - left - band.shape[-1]
        blocks.append(jnp.pad(band, ((0, 0), (0, 0), (left, right)),
                              constant_values=NEG_INF))
    return jnp.concatenate(blocks, axis=1)


def _sample_bias(bias_of_rel, t_new, n_cache):
    full = _rel_bias(bias_of_rel, t_new, n_cache + t_new, -n_cache)
    full = full.reshape(-1, n_cache + t_new)
    new = jnp.pad(full[:, n_cache:], ((0, 0), (LANES - t_new, 0)),
                  constant_values=NEG_INF)
    return full[:, :n_cache], new


def _cache_t(c):
    d, s, r, h, e = c.shape
    return jnp.transpose(c, (0, 1, 3, 4, 2)).reshape(d, s, h * e, r)


def _cache_untranspose(c, heads):
    d, s, w, r = c.shape
    return jnp.transpose(c.reshape(d, s, heads, w // heads, r), (0, 1, 4, 2, 3))


def kernel(x_prompt, x_sample, cache_a_k, cache_a_v, cache_b_k, cache_b_v, w_in,
           w_a_out, w_b_out, w_out, a_rel_table, t5_table, b_sinks, g_mix_pre,
           g_mix_post, g_ffn_pre, g_ffn_post, w_up, w_down):
    depth = w_in.shape[0]
    assert depth >= 2, "the last layer's sample kernel rolls the earlier layers' caches"
    batch, seq, d = x_prompt.shape
    dec_batch, t_new, _ = x_sample.shape
    a_len = cache_a_k.shape[2]
    b_len = cache_b_k.shape[2]
    mp = batch * seq
    ms = dec_batch * t_new

    yp = x_prompt.reshape(mp, d)
    ys = x_sample.reshape(ms, d)

    w_pack = None
    later_weights = (w_a_out, w_b_out, w_out, w_up, w_down)
    g_pre = g_mix_pre[:, None, :]
    g_post = g_mix_post[:, None, :]
    gf_pre = g_ffn_pre[:, None, :]
    gf_post = g_ffn_post[:, None, :]
    caches = tuple(_cache_t(c) for c in (cache_a_k, cache_a_v, cache_b_k, cache_b_v))

    b_of_rel = _b_bias_of_rel(t5_table)
    bias_b_tile = _band_tile(b_of_rel, B_TQ, B_LEFT_CHUNKS)
    bias_bc, bias_bn = _sample_bias(b_of_rel, t_new, b_len)
    a_of_rel = _a_bias_of_rel(jnp.concatenate([a_rel_table[l] for l in range(depth)], axis=1))
    bias_a_tile = _band_tile(a_of_rel, A_TQ, A_LEFT_CHUNKS)
    bias_ac, bias_an = _sample_bias(a_of_rel, t_new, a_len)

    prompt_kva, prompt_kvb, sample_kva, sample_kvb = [], [], [], []
    for l in range(depth):
        sinks = b_sinks[l] * LOG2E

        if w_pack is None:
            sample_proj, w_pack = _in_proj(ys, g_pre, w_in, l, tm=ms, from_f32=True)
        else:
            sample_proj = _in_proj(ys, g_pre, w_pack, l, tm=ms)

        qkv, kvb16, gates, kva32, kvb32 = _in_proj(yp, g_pre, w_pack, l, tm=1024)
        oa, wa16, wb16, wo16, wu16, wd16, *next_pack = _attn_a(
            qkv, bias_a_tile, later_weights, l, (w_in, l + 1) if l + 1 < depth else None,
            m=mp, seq=seq)
        w_pack = tuple(next_pack)
        ob = _attn_b(qkv, kvb16, bias_b_tile, sinks, m=mp, seq=seq)
        yp, hp = _merge(oa, ob, gates, yp, wa16, wb16, wo16, g_post, gf_pre, l, tm=512)
        yp = _ffn(yp, hp, wu16, wd16, gf_post, l, tm=512, tf=1024)
        prompt_kva.append(kva32)
        prompt_kvb.append(kvb32)

        qkv, _, gates, kva32, kvb32 = sample_proj
        sink_col = jnp.repeat(sinks, t_new)[:, None]
        if l + 1 < depth:
            oa, ob = _attn_sample(qkv, kva32, kvb32, caches, l, bias_ac, bias_an, bias_bc,
                                  bias_bn, sink_col, t_new=t_new)
            sample_kva.append(kva32)
            sample_kvb.append(kvb32)
        else:
            prev_new = (jnp.stack(sample_kva), jnp.stack(sample_kvb))
            oa, ob, aks, avs, bks, bvs = _attn_sample(
                qkv, kva32, kvb32, caches, l, bias_ac, bias_an, bias_bc, bias_bn,
                sink_col, prev_new, t_new=t_new)
        ys, hs = _merge(oa, ob, gates, ys, wa16, wb16, wo16, g_post, gf_pre, l, tm=ms)
        ys = _ffn(ys, hs, wu16, wd16, gf_post, l, tm=ms, tf=1024)

    akp, avp, bkp, bvp = _prompt_caches(prompt_kva, prompt_kvb, batch=batch, seq=seq,
                                        na=min(a_len, seq), nb=min(b_len, seq))
    return (yp.reshape(batch, seq, d), ys.reshape(dec_batch, t_new, d),
            _cache_untranspose(akp, A_HEADS), _cache_untranspose(avp, A_HEADS),
            _cache_untranspose(bkp, B_KV_HEADS), _cache_untranspose(bvp, B_KV_HEADS),
            _cache_untranspose(aks, A_HEADS), _cache_untranspose(avs, A_HEADS),
            _cache_untranspose(bks, B_KV_HEADS), _cache_untranspose(bvs, B_KV_HEADS))
```

```python
import functools
import math

import jax
import jax.numpy as jnp
from jax import lax
from jax.experimental import pallas as pl
from jax.experimental.pallas import tpu as pltpu

D_MODEL = 2048
CHUNK = 64
HEAD_DIM = 64
A_HEADS = 16
A_WIDTH = A_HEADS * HEAD_DIM
A_LEFT_CHUNKS = 8
A_REL_CLIP = 256
B_HEADS = 16
B_KV_HEADS = 2
B_GROUP = B_HEADS // B_KV_HEADS
B_WIDTH = B_HEADS * HEAD_DIM
B_KV_WIDTH = B_KV_HEADS * HEAD_DIM
B_LEFT_CHUNKS = 2
T5_BUCKETS = 32
T5_MAX_DIST = 128
EPS = 1e-6
NEG_INF = -1e30
LOG2E = math.log2(math.e)
Q_SCALE = HEAD_DIM ** -0.5 * LOG2E

LANES = 128
MXU_COLS = 256

COL_QA = 0
COL_KA = A_WIDTH
COL_VA = 2 * A_WIDTH
COL_QB = 3 * A_WIDTH
COL_KB = 3 * A_WIDTH + B_WIDTH
COL_GA = COL_KB + 2 * B_KV_WIDTH
PROJ_TN = 4 * MXU_COLS
QKV_KA, QKV_VA, QKV_QA, QKV_QB = 0, 1, 2, 3

A_TQ = 256
B_TQ = 128
A_KBLOCKS = A_LEFT_CHUNKS * CHUNK // A_TQ + 1
B_KBLOCKS = B_LEFT_CHUNKS * CHUNK // B_TQ + 1

MERGE_SLAB = 256

VMEM_LIMIT = 56 * 1024 * 1024


def _params(sem, vmem=VMEM_LIMIT):
    return pltpu.CompilerParams(dimension_semantics=sem, vmem_limit_bytes=vmem)


def _rms_scale(x, g):
    return x * lax.rsqrt(jnp.mean(x * x, axis=-1, keepdims=True) + EPS) * g


def _layer_vec_spec(l, d):
    return pl.BlockSpec((None, 1, d), lambda *_: (l, 0, 0))


PROJ_KV_STEPS = 2
PROJ_Q_STEPS = 4
PROJ_GATE_STEP0 = 4


def _proj_first_blocks():
    n_w = PROJ_TN // MXU_COLS
    blocks = [c // MXU_COLS for c in (COL_KA, COL_VA, COL_QA, COL_QB)]
    return blocks + [COL_GA // MXU_COLS + g * n_w for g in range(2 * D_MODEL // PROJ_TN)]


def _in_proj_kernel(x_ref, g_ref, *refs, from_f32):
    n_w = PROJ_TN // MXU_COLS
    w_refs, wkvb_ref = refs[:n_w], refs[n_w]
    qkv_ref, kvb16_ref, gates_ref, kva32_ref, kvb32_ref = refs[n_w + 1:n_w + 6]
    pack_refs, h_ref = refs[n_w + 6:-1], refs[-1]
    j = pl.program_id(1)
    tm = x_ref.shape[0]

    def weight(k):
        w_ref = wkvb_ref if k == n_w else w_refs[k]
        if not from_f32:
            return w_ref[...]
        w = w_ref[...].astype(jnp.bfloat16)
        pack_refs[k][...] = w
        return w

    def tile(out16_ref, out32_ref, scale=None, rows=slice(None)):
        h = h_ref[rows]
        for k in range(n_w):
            cols = slice(k * MXU_COLS, (k + 1) * MXU_COLS)
            acc = jnp.dot(h, weight(k), preferred_element_type=jnp.float32)
            if out32_ref is not None:
                out32_ref[rows, cols] = acc
            if scale is not None:
                acc = acc * scale
            out16_ref[rows, cols] = acc.astype(out16_ref.dtype)

    @pl.when(j == 0)
    def _():
        slab = tm // 2 if tm >= 2 * MXU_COLS else tm
        for r0 in range(0, tm, slab):
            rows = slice(r0, r0 + slab)
            h_ref[rows] = _rms_scale(x_ref[rows], g_ref[...]).astype(h_ref.dtype)
            tile(qkv_ref, kva32_ref, rows=rows)

    @pl.when((j > 0) & (j < PROJ_KV_STEPS))
    def _():
        tile(qkv_ref, kva32_ref)

    @pl.when((j >= PROJ_KV_STEPS) & (j < PROJ_Q_STEPS))
    def _():
        tile(qkv_ref, None, Q_SCALE)

    @pl.when(j == PROJ_Q_STEPS - 1)
    def _():
        acc = jnp.dot(h_ref[...], weight(n_w), preferred_element_type=jnp.float32)
        kvb16_ref[...] = acc.astype(kvb16_ref.dtype)
        kvb32_ref[...] = acc

    @pl.when(j >= PROJ_GATE_STEP0)
    def _():
        tile(gates_ref, None)


def _in_proj(x, g, w, l, *, tm, from_f32=False):
    m, d = x.shape
    n_w = PROJ_TN // MXU_COLS
    n_gate = 2 * D_MODEL // PROJ_TN
    steps = PROJ_GATE_STEP0 + n_gate
    first = _proj_first_blocks()
    kvb_n = 2 * B_KV_WIDTH

    def f32_map(k):
        def index_map(i, j):
            blk = first[-1] + k
            for step in range(steps - 2, -1, -1):
                blk = jnp.where(j == step, first[step] + k, blk)
            return (l, 0, blk)
        return index_map

    packed_spec = pl.BlockSpec((d, MXU_COLS), lambda i, j: (0, j))
    packed_kvb_spec = pl.BlockSpec((d, kvb_n), lambda i, j: (0, 0))
    if from_f32:
        assert m == tm, "the packed weights are written once per row tile"
        w_specs = [pl.BlockSpec((None, d, MXU_COLS), f32_map(k)) for k in range(n_w)]
        w_specs.append(pl.BlockSpec((None, d, kvb_n), lambda i, j: (l, 0, COL_KB // kvb_n)))
        w_args = [w] * (n_w + 1)
        pack_specs = [packed_spec] * n_w + [packed_kvb_spec]
        pack_shapes = [jax.ShapeDtypeStruct((d, steps * MXU_COLS), jnp.bfloat16)] * n_w
        pack_shapes.append(jax.ShapeDtypeStruct((d, kvb_n), jnp.bfloat16))
    else:
        w_specs = [packed_spec] * n_w + [packed_kvb_spec]
        w_args = list(w)
        pack_specs, pack_shapes = [], []
    outs = pl.pallas_call(
        functools.partial(_in_proj_kernel, from_f32=from_f32),
        grid=(m // tm, steps),
        in_specs=[
            pl.BlockSpec((tm, d), lambda i, j: (i, 0)),
            _layer_vec_spec(l, d),
        ] + w_specs,
        out_specs=[
            pl.BlockSpec((tm, PROJ_TN), lambda i, j: (i, jnp.minimum(j, PROJ_Q_STEPS - 1))),
            pl.BlockSpec((tm, kvb_n), lambda i, j: (i, 0)),
            pl.BlockSpec((tm, PROJ_TN),
                         lambda i, j: (i, jnp.clip(j - PROJ_GATE_STEP0, 0, n_gate - 1))),
            pl.BlockSpec((tm, PROJ_TN), lambda i, j: (i, jnp.minimum(j, PROJ_KV_STEPS - 1))),
            pl.BlockSpec((tm, kvb_n), lambda i, j: (i, 0)),
        ] + pack_specs,
        out_shape=[
            jax.ShapeDtypeStruct((m, PROJ_Q_STEPS * PROJ_TN), jnp.bfloat16),
            jax.ShapeDtypeStruct((m, kvb_n), jnp.bfloat16),
            jax.ShapeDtypeStruct((m, 2 * D_MODEL), jnp.bfloat16),
            jax.ShapeDtypeStruct((m, PROJ_KV_STEPS * PROJ_TN), jnp.float32),
            jax.ShapeDtypeStruct((m, kvb_n), jnp.float32),
        ] + pack_shapes,
        scratch_shapes=[pltpu.VMEM((tm, d), jnp.bfloat16)],
        compiler_params=_params(("parallel", "arbitrary")),
        name="in_proj",
    )(x, g, *w_args)
    return (outs[:5], tuple(outs[5:])) if from_f32 else outs


def _low_half():
    return lax.broadcasted_iota(jnp.int32, (1, LANES), 1) < HEAD_DIM


def _start_mask(tile_in_seq, n_kblocks, tq):
    lane = lax.broadcasted_iota(jnp.int32, (1, n_kblocks * tq), 1)
    first_valid = jnp.maximum(n_kblocks - 1 - tile_in_seq, 0) * tq
    return jnp.where(lane >= first_valid, 0.0, NEG_INF)


def _pack_w_in_slab(w32_ref, pack_refs):
    slab = w32_ref[...].astype(jnp.bfloat16)
    block = lambda b: slab[:, b * MXU_COLS:(b + 1) * MXU_COLS]
    for k, pack_ref in enumerate(pack_refs[:-1]):
        pack_ref[...] = jnp.concatenate([block(b + k) for b in _proj_first_blocks()], axis=1)
    pack_refs[-1][...] = block(COL_KB // MXU_COLS)


def _attn_a_kernel(q_ref, *refs, tiles_per_seq, n_cast, packing):
    k_refs = refs[:A_KBLOCKS]
    v_refs = refs[A_KBLOCKS:2 * A_KBLOCKS]
    bias_ref = refs[2 * A_KBLOCKS]
    rest = refs[2 * A_KBLOCKS + 1:]
    w32_refs, rest = rest[:n_cast], rest[n_cast:]
    if packing:
        w_in_ref, rest = rest[0], rest[1:]
    o_ref, w16_refs, pack_refs = rest[0], rest[1:1 + n_cast], rest[1 + n_cast:]
    tile_in_seq = pl.program_id(0) % tiles_per_seq
    low = _low_half()

    def heads(start):
        for w32_ref, w16_ref in zip(w32_refs, w16_refs):
            w16_ref[...] = w32_ref[...].astype(w16_ref.dtype)
        if packing:
            _pack_w_in_slab(w_in_ref, pack_refs)
        def scores(p):
            sl = slice(p * LANES, (p + 1) * LANES)
            q2 = q_ref[:, sl]
            k2 = jnp.concatenate([r[:, sl] for r in k_refs], axis=0)
            qs = jnp.concatenate([jnp.where(low, q2, 0), jnp.where(low, 0, q2)], axis=0)
            return lax.dot_general(qs, k2, (((1,), (1,)), ((), ())),
                                   preferred_element_type=jnp.float32)

        n_pairs = A_HEADS // 2
        ahead = 2
        pending = [scores(p) for p in range(ahead)]
        for p in range(n_pairs):
            sl = slice(p * LANES, (p + 1) * LANES)
            v2 = jnp.concatenate([r[:, sl] for r in v_refs], axis=0)
            if p + ahead < n_pairs:
                pending.append(scores(p + ahead))
            s_all = pending.pop(0)
            probs = []
            for hh in range(2):
                s = s_all[hh * A_TQ:(hh + 1) * A_TQ] + bias_ref[2 * p + hh]
                if start is not None:
                    s = s + start
                s = s.astype(v2.dtype)
                d = s - jnp.max(s, axis=-1, keepdims=True)
                probs.append(jnp.exp2(d.astype(jnp.float32)).astype(v2.dtype))
            v_ones = jnp.concatenate([v2, jnp.ones_like(v2)], axis=1)
            r = jnp.dot(jnp.concatenate(probs, axis=0), v_ones,
                        preferred_element_type=jnp.float32)
            out = r[:, :LANES] / r[:, LANES:]
            o_ref[:, sl] = jnp.where(low, out[:A_TQ], out[A_TQ:]).astype(o_ref.dtype)

    @pl.when(tile_in_seq >= A_KBLOCKS - 1)
    def _():
        heads(None)

    @pl.when(tile_in_seq < A_KBLOCKS - 1)
    def _():
        heads(_start_mask(tile_in_seq, A_KBLOCKS, A_TQ))


def _band_kv_map(j, col, n_kblocks, tiles_per_seq):
    def index_map(t, *_):
        i = t % tiles_per_seq
        return (t - i + jnp.maximum(i - (n_kblocks - 1) + j, 0), col)
    return index_map


def _attn_a(qkv, bias, weights, l, w_in_next=None, *, m, seq):
    tiles_per_seq = seq // A_TQ
    steps = m // A_TQ
    kv_spec = lambda j, col: pl.BlockSpec(
        (A_TQ, A_WIDTH), _band_kv_map(j, col, A_KBLOCKS, tiles_per_seq))
    slab32 = lambda w, layer: pl.BlockSpec((None, w.shape[1] // steps, w.shape[2]),
                                           lambda t: (layer, t, 0))
    slab16 = lambda rows, width: pl.BlockSpec((rows // steps, width), lambda t: (t, 0))
    in_specs = ([pl.BlockSpec((A_TQ, A_WIDTH), lambda t: (t, QKV_QA))]
                + [kv_spec(j, QKV_KA) for j in range(A_KBLOCKS)]
                + [kv_spec(j, QKV_VA) for j in range(A_KBLOCKS)]
                + [pl.BlockSpec((A_HEADS, A_TQ, A_KBLOCKS * A_TQ), lambda t: (l, 0, 0),
                                pipeline_mode=pl.Buffered(1))]
                + [slab32(w, l) for w in weights])
    out_specs = ([pl.BlockSpec((A_TQ, A_WIDTH), lambda t: (t, 0))]
                 + [slab16(w.shape[1], w.shape[2]) for w in weights])
    out_shape = ([jax.ShapeDtypeStruct((m, A_WIDTH), jnp.bfloat16)]
                 + [jax.ShapeDtypeStruct(w.shape[1:], jnp.bfloat16) for w in weights])
    args = [qkv] * (2 * A_KBLOCKS + 1) + [bias, *weights]
    if w_in_next is not None:
        w_in, layer = w_in_next
        in_specs.append(slab32(w_in, layer))
        args.append(w_in)
        widths = [len(_proj_first_blocks()) * MXU_COLS] * (PROJ_TN // MXU_COLS)
        widths.append(2 * B_KV_WIDTH)
        out_specs += [slab16(w_in.shape[1], width) for width in widths]
        out_shape += [jax.ShapeDtypeStruct((w_in.shape[1], width), jnp.bfloat16)
                      for width in widths]
    return pl.pallas_call(
        functools.partial(_attn_a_kernel, tiles_per_seq=tiles_per_seq,
                          n_cast=len(weights), packing=w_in_next is not None),
        grid=(steps,),
        in_specs=in_specs,
        out_specs=out_specs,
        out_shape=out_shape,
        compiler_params=_params(("parallel",)),
        name="attn_a",
    )(*args)


def _dup_half(x, g):
    swapped = pltpu.roll(x, HEAD_DIM, 1)
    low = _low_half()
    return jnp.where(low, x, swapped) if g == 0 else jnp.where(low, swapped, x)


def _attn_b_kernel(sink_ref, q_ref, *refs, tiles_per_seq):
    k_refs = refs[:B_KBLOCKS]
    v_refs = refs[B_KBLOCKS:2 * B_KBLOCKS]
    bias_ref, o_ref = refs[2 * B_KBLOCKS:]
    start = _start_mask(pl.program_id(0) % tiles_per_seq, B_KBLOCKS, B_TQ)
    low = _low_half()
    k2 = jnp.concatenate([r[...] for r in k_refs], axis=0).astype(jnp.float32)
    v2 = jnp.concatenate([r[...] for r in v_refs], axis=0).astype(jnp.float32)
    pairs = B_GROUP // 2
    for g in range(B_KV_HEADS):
        kd = _dup_half(k2, g).astype(jnp.bfloat16)
        vd = _dup_half(v2, g).astype(jnp.bfloat16)
        stacked = []
        for p in range(pairs):
            c0 = (g * pairs + p) * LANES
            q2 = q_ref[:, c0:c0 + LANES]
            stacked.append(jnp.where(low, q2, 0))
            stacked.append(jnp.where(low, 0, q2))
        qs = jnp.concatenate(stacked, axis=0)
        s_all = lax.dot_general(qs, kd, (((1,), (1,)), ((), ())),
                                preferred_element_type=jnp.float32)
        probs, sink_terms = [], []
        for hl in range(B_GROUP):
            h = g * B_GROUP + hl
            s = s_all[hl * B_TQ:(hl + 1) * B_TQ] + bias_ref[h] + start
            sink = sink_ref[h]
            mx = jnp.maximum(jnp.max(s, axis=-1, keepdims=True), sink)
            probs.append(jnp.exp2(s - mx).astype(jnp.bfloat16))
            sink_terms.append(jnp.exp2(sink - mx))
        v_ones = jnp.concatenate([vd, jnp.ones_like(vd)], axis=1)
        r = jnp.dot(jnp.concatenate(probs, axis=0), v_ones,
                    preferred_element_type=jnp.float32)
        outs = []
        for hl in range(B_GROUP):
            rh = r[hl * B_TQ:(hl + 1) * B_TQ]
            outs.append(rh[:, :LANES] / (rh[:, LANES:] + sink_terms[hl]))
        for p in range(pairs):
            c0 = (g * pairs + p) * LANES
            o_ref[:, c0:c0 + LANES] = jnp.where(
                low, outs[2 * p], outs[2 * p + 1]).astype(o_ref.dtype)


def _attn_b(qkv, kvb16, bias, sinks, *, m, seq):
    tiles_per_seq = seq // B_TQ
    kv_spec = lambda j, col: pl.BlockSpec(
        (B_TQ, LANES), _band_kv_map(j, col, B_KBLOCKS, tiles_per_seq))
    return pl.pallas_call(
        functools.partial(_attn_b_kernel, tiles_per_seq=tiles_per_seq),
        grid_spec=pltpu.PrefetchScalarGridSpec(
            num_scalar_prefetch=1,
            grid=(m // B_TQ,),
            in_specs=[pl.BlockSpec((B_TQ, B_WIDTH), lambda t, s: (t, QKV_QB))]
            + [kv_spec(j, 0) for j in range(B_KBLOCKS)]
            + [kv_spec(j, 1) for j in range(B_KBLOCKS)]
            + [pl.BlockSpec((B_HEADS, B_TQ, B_KBLOCKS * B_TQ), lambda t, s: (0, 0, 0))],
            out_specs=pl.BlockSpec((B_TQ, B_WIDTH), lambda t, s: (t, 0)),
        ),
        out_shape=jax.ShapeDtypeStruct((m, B_WIDTH), jnp.bfloat16),
        compiler_params=_params(("parallel",)),
        name="attn_b",
    )(sinks, qkv, *([kvb16] * (2 * B_KBLOCKS)), bias)


def _pad_top(x, t_new):
    return jnp.concatenate([jnp.zeros((LANES - t_new, x.shape[1]), x.dtype), x], axis=0)


def _roll_in(cache_t, new_rows, out_ref, layer, t_new):
    n = cache_t.shape[1]
    shifted = pltpu.roll(cache_t, n - t_new, 1)
    lane = lax.broadcasted_iota(jnp.int32, (1, LANES), 1)
    tail = jnp.where(lane < LANES - t_new, shifted[:, n - LANES:],
                     _pad_top(new_rows, t_new).T)
    if n > LANES:
        out_ref[layer, :, :n - LANES] = shifted[:, :n - LANES]
    out_ref[layer, :, n - LANES:] = tail


def _attn_sample_kernel(qa_ref, qb_ref, kan_ref, van_ref, kbn_ref, vbn_ref,
                        cak_ref, cav_ref, cbk_ref, cbv_ref,
                        bias_ac_ref, bias_an_ref, bias_bc_ref, bias_bn_ref, sink_ref,
                        *refs, t_new, layer, rolling):
    if rolling:
        prev_refs, (oa_ref, ob_ref), out_refs = refs[:4], refs[4:6], refs[6:]
        for cache_ref, new_ref, prev_ref, out_ref in zip(
                (cak_ref, cav_ref, cbk_ref, cbv_ref), (kan_ref, van_ref, kbn_ref, vbn_ref),
                prev_refs, out_refs):
            for l2 in range(layer + 1):
                rows_new = new_ref[...] if l2 == layer else prev_ref[l2]
                _roll_in(cache_ref[l2], rows_new, out_ref, l2, t_new)
        cak_ref, cav_ref, cbk_ref, cbv_ref = (
            r.at[layer] for r in (cak_ref, cav_ref, cbk_ref, cbv_ref))
    else:
        oa_ref, ob_ref = refs
    heads = A_HEADS
    rows = heads * t_new
    width = heads * HEAD_DIM
    bf16 = jnp.bfloat16
    row_head = lax.broadcasted_iota(jnp.int32, (rows, width), 0) // t_new
    lane_head = lax.broadcasted_iota(jnp.int32, (rows, width), 1) // HEAD_DIM
    own = row_head == lane_head

    def stack_q(q):
        return jnp.where(own, jnp.concatenate([q] * heads, axis=0), 0)

    def nt_dot(a, b):
        return lax.dot_general(a, b, (((1,), (1,)), ((), ())),
                               preferred_element_type=jnp.float32)

    def finish(s_c, s_n, vt_c, v_n, sink):
        mx = jnp.maximum(jnp.max(s_c, axis=-1, keepdims=True),
                         jnp.max(s_n, axis=-1, keepdims=True))
        if sink is not None:
            mx = jnp.maximum(mx, sink)
        e_c = jnp.exp2(s_c - mx)
        e_n = jnp.exp2(s_n - mx)
        denom = jnp.sum(e_c, axis=-1, keepdims=True) + jnp.sum(e_n, axis=-1, keepdims=True)
        if sink is not None:
            denom = denom + jnp.exp2(sink - mx)
        o = nt_dot(e_c.astype(bf16), vt_c)
        o = o + jnp.dot(e_n.astype(bf16), v_n, preferred_element_type=jnp.float32)
        o = jnp.where(own, o / denom, 0.0)
        return jnp.sum(o.reshape(heads, t_new, width), axis=0)

    k_n = _pad_top(kan_ref[...], t_new).astype(bf16)
    v_n = _pad_top(van_ref[...], t_new).astype(bf16)
    qs = stack_q(qa_ref[...])
    s_c = jnp.dot(qs, cak_ref[...].astype(bf16),
                  preferred_element_type=jnp.float32) + bias_ac_ref[...]
    s_n = nt_dot(qs, k_n) + bias_an_ref[...]
    oa_ref[...] = finish(s_c, s_n, cav_ref[...].astype(bf16), v_n, None).astype(oa_ref.dtype)

    def expansion(shape, src_axis):
        src = lax.broadcasted_iota(jnp.int32, shape, src_axis)
        dst = lax.broadcasted_iota(jnp.int32, shape, 1 - src_axis)
        return ((src % HEAD_DIM == dst % HEAD_DIM)
                & (src // HEAD_DIM == dst // (B_GROUP * HEAD_DIM))).astype(bf16)

    def widen(x):
        return jnp.dot(x.astype(bf16), expansion((B_KV_WIDTH, width), 0),
                       preferred_element_type=jnp.float32).astype(bf16)

    def widen_t(xt):
        return jnp.dot(expansion((width, B_KV_WIDTH), 1), xt.astype(bf16),
                       preferred_element_type=jnp.float32).astype(bf16)

    k_n = widen(_pad_top(kbn_ref[...], t_new))
    v_n = widen(_pad_top(vbn_ref[...], t_new))
    qs = stack_q(qb_ref[...])
    s_c = jnp.dot(qs, widen_t(cbk_ref[...]),
                  preferred_element_type=jnp.float32) + bias_bc_ref[...]
    s_n = nt_dot(qs, k_n) + bias_bn_ref[...]
    ob_ref[...] = finish(s_c, s_n, widen_t(cbv_ref[...]), v_n, sink_ref[...]).astype(ob_ref.dtype)


def _attn_sample(qkv, kva32, kvb32, caches, l, bias_ac, bias_an, bias_bc, bias_bn,
                 sink_col, prev_new=None, *, t_new):
    m = qkv.shape[0]
    cak, cav, cbk, cbv = caches
    a_len, b_len = cak.shape[-1], cbk.shape[-1]
    rows = A_HEADS * t_new
    rolling = prev_new is not None
    const = lambda shape: pl.BlockSpec(shape, lambda b: (0,) * len(shape))
    if rolling:
        assert l == cak.shape[0] - 1 and prev_new[0].shape[0] == l
        cache_spec = lambda c: pl.BlockSpec((c.shape[0], None) + c.shape[2:],
                                            lambda b: (0, b, 0, 0))
    else:
        cache_spec = lambda c: pl.BlockSpec((None, None) + c.shape[2:],
                                            lambda b: (l, b, 0, 0))
    in_specs = [
        pl.BlockSpec((t_new, A_WIDTH), lambda b: (b, QKV_QA)),
        pl.BlockSpec((t_new, B_WIDTH), lambda b: (b, QKV_QB)),
        pl.BlockSpec((t_new, A_WIDTH), lambda b: (b, 0)),
        pl.BlockSpec((t_new, A_WIDTH), lambda b: (b, 1)),
        pl.BlockSpec((t_new, B_KV_WIDTH), lambda b: (b, 0)),
        pl.BlockSpec((t_new, B_KV_WIDTH), lambda b: (b, 1)),
        cache_spec(cak), cache_spec(cav), cache_spec(cbk), cache_spec(cbv),
        pl.BlockSpec((rows, a_len), lambda b: (l, 0)),
        pl.BlockSpec((rows, LANES), lambda b: (l, 0)),
        const((rows, b_len)), const((rows, LANES)),
        const((rows, 1)),
    ]
    args = [qkv, qkv, kva32, kva32, kvb32, kvb32, cak, cav, cbk, cbv,
            bias_ac, bias_an, bias_bc, bias_bn, sink_col]
    out_specs = [
        pl.BlockSpec((t_new, A_WIDTH), lambda b: (b, 0)),
        pl.BlockSpec((t_new, B_WIDTH), lambda b: (b, 0)),
    ]
    out_shape = [
        jax.ShapeDtypeStruct((m, A_WIDTH), jnp.bfloat16),
        jax.ShapeDtypeStruct((m, B_WIDTH), jnp.bfloat16),
    ]
    if rolling:
        new_a, new_b = prev_new
        prev_spec = lambda width, col: pl.BlockSpec((l, t_new, width), lambda b: (0, b, col))
        in_specs += [prev_spec(A_WIDTH, 0), prev_spec(A_WIDTH, 1),
                     prev_spec(B_KV_WIDTH, 0), prev_spec(B_KV_WIDTH, 1)]
        args += [new_a, new_a, new_b, new_b]
        out_specs += [cache_spec(c) for c in caches]
        out_shape += [jax.ShapeDtypeStruct(c.shape, jnp.float32) for c in caches]
    return pl.pallas_call(
        functools.partial(_attn_sample_kernel, t_new=t_new, layer=l, rolling=rolling),
        grid=(m // t_new,),
        in_specs=in_specs,
        out_specs=out_specs,
        out_shape=out_shape,
        compiler_params=_params(("parallel",)),
        name="attn_sample",
    )(*args)


def _prompt_caches_kernel(*refs, depth):
    in_refs, (ak_ref, av_ref, bk_ref, bv_ref) = refs[:4 * depth], refs[4 * depth:]
    for k in range(depth):
        @pl.when(pl.program_id(0) == k)
        def _(k=k):
            ka_ref, va_ref, kb_ref, vb_ref = in_refs[4 * k:4 * k + 4]
            ak_ref[...] = ka_ref[...].T
            av_ref[...] = va_ref[...].T
            bk_ref[...] = kb_ref[...].T
            bv_ref[...] = vb_ref[...].T


def _prompt_caches(kva32s, kvb32s, *, batch, seq, na, nb):
    depth = len(kva32s)

    def rows_map(k, tile, col):
        def index_map(l, b):
            bb = jnp.where(l < k, 0, jnp.where(l > k, batch - 1, b))
            return ((bb + 1) * tile - 1, col)
        return index_map

    in_specs, args = [], []
    for k in range(depth):
        in_specs += [
            pl.BlockSpec((na, A_WIDTH), rows_map(k, seq // na, 0)),
            pl.BlockSpec((na, A_WIDTH), rows_map(k, seq // na, 1)),
            pl.BlockSpec((nb, B_KV_WIDTH), rows_map(k, seq // nb, 0)),
            pl.BlockSpec((nb, B_KV_WIDTH), rows_map(k, seq // nb, 1)),
        ]
        args += [kva32s[k], kva32s[k], kvb32s[k], kvb32s[k]]
    out_shapes = [(depth, batch, A_WIDTH, na)] * 2 + [(depth, batch, B_KV_WIDTH, nb)] * 2
    return pl.pallas_call(
        functools.partial(_prompt_caches_kernel, depth=depth),
        grid=(depth, batch),
        in_specs=in_specs,
        out_specs=[pl.BlockSpec((None, None) + s[2:], lambda l, b: (l, b, 0, 0))
                   for s in out_shapes],
        out_shape=[jax.ShapeDtypeStruct(s, jnp.float32) for s in out_shapes],
        compiler_params=_params(("arbitrary", "arbitrary")),
        name="prompt_caches",
    )(*args)


def _merge_kernel(oa_ref, ob_ref, ga_ref, gb_ref, x_ref, wa_ref, wb_ref, wo_ref,
                  g_ref, gnext_ref, y_ref, h_ref):
    tm = x_ref.shape[0]
    slab = min(tm, MERGE_SLAB)
    for r0 in range(0, tm, slab):
        rows = slice(r0, r0 + slab)
        ta = jnp.dot(oa_ref[rows], wa_ref[...], preferred_element_type=jnp.float32)
        tb = jnp.dot(ob_ref[rows], wb_ref[...], preferred_element_type=jnp.float32)
        mixed = (jax.nn.sigmoid(ga_ref[rows].astype(jnp.float32)) * ta
                 + jax.nn.sigmoid(gb_ref[rows].astype(jnp.float32)) * tb)
        z = jnp.dot(mixed.astype(jnp.bfloat16), wo_ref[...],
                    preferred_element_type=jnp.float32)
        y = x_ref[rows] + _rms_scale(z, g_ref[...])
        y_ref[rows] = y
        h_ref[rows] = _rms_scale(y, gnext_ref[...]).astype(h_ref.dtype)


def _merge(oa, ob, gates, x, wa, wb, wo, g, g_next, l, *, tm):
    m, d = x.shape
    resident = lambda w: pl.BlockSpec(w.shape, lambda i: (0, 0),
                                      pipeline_mode=pl.Buffered(1))
    return pl.pallas_call(
        _merge_kernel,
        grid=(m // tm,),
        in_specs=[
            pl.BlockSpec((tm, A_WIDTH), lambda i: (i, 0)),
            pl.BlockSpec((tm, B_WIDTH), lambda i: (i, 0)),
            pl.BlockSpec((tm, d), lambda i: (i, 0)),
            pl.BlockSpec((tm, d), lambda i: (i, 1)),
            pl.BlockSpec((tm, d), lambda i: (i, 0)),
            resident(wa), resident(wb), resident(wo),
            _layer_vec_spec(l, d),
            _layer_vec_spec(l, d),
        ],
        out_specs=[pl.BlockSpec((tm, d), lambda i: (i, 0))] * 2,
        out_shape=[jax.ShapeDtypeStruct((m, d), jnp.float32),
                   jax.ShapeDtypeStruct((m, d), jnp.bfloat16)],
        compiler_params=_params(("parallel",)),
        name="merge",
    )(oa, ob, gates, gates, x, wa, wb, wo, g, g_next)


def _ffn_kernel(x_ref, h_ref, wu_ref, wd_ref, gpost_ref, y_ref):
    f = pl.program_id(1)
    last = pl.num_programs(1) - 1

    @pl.when(f == 0)
    def _():
        y_ref[...] = jnp.zeros_like(y_ref)

    def partial(rows):
        u = jnp.dot(h_ref[rows], wu_ref[...], preferred_element_type=jnp.float32)
        u = jnp.square(jnp.maximum(u, 0.0)).astype(jnp.bfloat16)
        return y_ref[rows] + jnp.dot(u, wd_ref[...], preferred_element_type=jnp.float32)

    @pl.when(f < last)
    def _():
        y_ref[...] = partial(slice(None))

    @pl.when(f == last)
    def _():
        tm = x_ref.shape[0]
        slab = tm // 2 if tm >= 2 * MXU_COLS else tm
        for r0 in range(0, tm, slab):
            rows = slice(r0, r0 + slab)
            y_ref[rows] = x_ref[rows] + _rms_scale(partial(rows), gpost_ref[...])


def _ffn(x, h, wu, wd, gpost, l, *, tm, tf):
    m, d = x.shape
    dff = wu.shape[-1]
    return pl.pallas_call(
        _ffn_kernel,
        grid=(m // tm, dff // tf),
        in_specs=[
            pl.BlockSpec((tm, d), lambda i, f: (i, 0)),
            pl.BlockSpec((tm, d), lambda i, f: (i, 0)),
            pl.BlockSpec((d, tf), lambda i, f: (0, f)),
            pl.BlockSpec((tf, d), lambda i, f: (f, 0)),
            _layer_vec_spec(l, d),
        ],
        out_specs=pl.BlockSpec((tm, d), lambda i, f: (i, 0)),
        out_shape=jax.ShapeDtypeStruct((m, d), jnp.float32),
        compiler_params=_params(("parallel", "arbitrary")),
        name="ffn",
    )(x, h, wu, wd, gpost)


def _t5_bucket(rel):
    half = T5_BUCKETS // 2
    exact = half // 2
    ret = jnp.where(rel > 0, half, 0)
    n = jnp.abs(rel)
    large = exact + (jnp.log(jnp.maximum(n, 1).astype(jnp.float32) / exact)
                     / math.log(T5_MAX_DIST / exact) * (half - exact)).astype(jnp.int32)
    large = jnp.minimum(large, half - 1)
    return ret + jnp.where(n < exact, n, large)


def _a_bias_of_rel(table):
    scaled = table.T * LOG2E
    return lambda rel: scaled[:, jnp.clip(rel, -A_REL_CLIP, A_REL_CLIP) + A_REL_CLIP]


def _b_bias_of_rel(table):
    scaled = table.T * LOG2E
    return lambda rel: scaled[:, _t5_bucket(-rel)]


def _hankel(u, q, n):
    heads, k = u.shape
    period = q + n
    u = jnp.pad(u, ((0, 0), (0, period - k)))
    flat = jnp.tile(u, (1, q + 1))[:, :q * (period + 1)]
    return flat.reshape(heads, q, period + 1)[:, :, :n]


def _rel_bias(bias_of_rel, q_len, n_keys, k0):
    k = jnp.arange(q_len + n_keys - 1)
    u = bias_of_rel(k - (n_keys - 1) - k0)
    return _hankel(u, q_len, n_keys)[:, :, ::-1]


def _band_tile(bias_of_rel, tq, n_prev):
    window = tq + n_prev * CHUNK
    band = _rel_bias(bias_of_rel, CHUNK, (n_prev + 1) * CHUNK, -n_prev * CHUNK)
    blocks = []
    for c in range(tq // CHUNK):
        left = c * CHUNK
        right = window - left - band.shape[-1]
        blocks.append(jnp.pad(band, ((0, 0), (0, 0), (left, right)),
                              constant_values=NEG_INF))
    return jnp.concatenate(blocks, axis=1)


def _sample_bias(bias_of_rel, t_new, n_cache):
    full = _rel_bias(bias_of_rel, t_new, n_cache + t_new, -n_cache)
    full = full.reshape(-1, n_cache + t_new)
    new = jnp.pad(full[:, n_cache:], ((0, 0), (LANES - t_new, 0)),
                  constant_values=NEG_INF)
    return full[:, :n_cache], new


def _cache_t(c):
    d, s, r, h, e = c.shape
    return jnp.transpose(c, (0, 1, 3, 4, 2)).reshape(d, s, h * e, r)


def _cache_untranspose(c, heads):
    d, s, w, r = c.shape
    return jnp.transpose(c.reshape(d, s, heads, w // heads, r), (0, 1, 4, 2, 3))


def kernel(x_prompt, x_sample, cache_a_k, cache_a_v, cache_b_k, cache_b_v, w_in,
           w_a_out, w_b_out, w_out, a_rel_table, t5_table, b_sinks, g_mix_pre,
           g_mix_post, g_ffn_pre, g_ffn_post, w_up, w_down):
    depth = w_in.shape[0]
    assert depth >= 2, "the last layer's sample kernel rolls the earlier layers' caches"
    batch, seq, d = x_prompt.shape
    dec_batch, t_new, _ = x_sample.shape
    a_len = cache_a_k.shape[2]
    b_len = cache_b_k.shape[2]
    mp = batch * seq
    ms = dec_batch * t_new

    yp = x_prompt.reshape(mp, d)
    ys = x_sample.reshape(ms, d)

    w_pack = None
    later_weights = (w_a_out, w_b_out, w_out, w_up, w_down)
    g_pre = g_mix_pre[:, None, :]
    g_post = g_mix_post[:, None, :]
    gf_pre = g_ffn_pre[:, None, :]
    gf_post = g_ffn_post[:, None, :]
    caches = tuple(_cache_t(c) for c in (cache_a_k, cache_a_v, cache_b_k, cache_b_v))

    b_of_rel = _b_bias_of_rel(t5_table)
    bias_b_tile = _band_tile(b_of_rel, B_TQ, B_LEFT_CHUNKS)
    bias_bc, bias_bn = _sample_bias(b_of_rel, t_new, b_len)
    a_of_rel = _a_bias_of_rel(jnp.concatenate([a_rel_table[l] for l in range(depth)], axis=1))
    bias_a_tile = _band_tile(a_of_rel, A_TQ, A_LEFT_CHUNKS)
    bias_ac, bias_an = _sample_bias(a_of_rel, t_new, a_len)

    prompt_kva, prompt_kvb, sample_kva, sample_kvb = [], [], [], []
    for l in range(depth):
        sinks = b_sinks[l] * LOG2E

        if w_pack is None:
            sample_proj, w_pack = _in_proj(ys, g_pre, w_in, l, tm=ms, from_f32=True)
        else:
            sample_proj = _in_proj(ys, g_pre, w_pack, l, tm=ms)

        qkv, kvb16, gates, kva32, kvb32 = _in_proj(yp, g_pre, w_pack, l, tm=1024)
        oa, wa16, wb16, wo16, wu16, wd16, *next_pack = _attn_a(
            qkv, bias_a_tile, later_weights, l, (w_in, l + 1) if l + 1 < depth else None,
            m=mp, seq=seq)
        w_pack = tuple(next_pack)
        ob = _attn_b(qkv, kvb16, bias_b_tile, sinks, m=mp, seq=seq)
        yp, hp = _merge(oa, ob, gates, yp, wa16, wb16, wo16, g_post, gf_pre, l, tm=512)
        yp = _ffn(yp, hp, wu16, wd16, gf_post, l, tm=512, tf=1024)
        prompt_kva.append(kva32)
        prompt_kvb.append(kvb32)

        qkv, _, gates, kva32, kvb32 = sample_proj
        sink_col = jnp.repeat(sinks, t_new)[:, None]
        if l + 1 < depth:
            oa, ob = _attn_sample(qkv, kva32, kvb32, caches, l, bias_ac, bias_an, bias_bc,
                                  bias_bn, sink_col, t_new=t_new)
            sample_kva.append(kva32)
            sample_kvb.append(kvb32)
        else:
            prev_new = (jnp.stack(sample_kva), jnp.stack(sample_kvb))
            oa, ob, aks, avs, bks, bvs = _attn_sample(
                qkv, kva32, kvb32, caches, l, bias_ac, bias_an, bias_bc, bias_bn,
                sink_col, prev_new, t_new=t_new)
        ys, hs = _merge(oa, ob, gates, ys, wa16, wb16, wo16, g_post, gf_pre, l, tm=ms)
        ys = _ffn(ys, hs, wu16, wd16, gf_post, l, tm=ms, tf=1024)

    akp, avp, bkp, bvp = _prompt_caches(prompt_kva, prompt_kvb, batch=batch, seq=seq,
                                        na=min(a_len, seq), nb=min(b_len, seq))
    return (yp.reshape(batch, seq, d), ys.reshape(dec_batch, t_new, d),
            _cache_untranspose(akp, A_HEADS), _cache_untranspose(avp, A_HEADS),
            _cache_untranspose(bkp, B_KV_HEADS), _cache_untranspose(bvp, B_KV_HEADS),
            _cache_untranspose(aks, A_HEADS), _cache_untranspose(avs, A_HEADS),
            _cache_untranspose(bks, B_KV_HEADS), _cache_untranspose(bvs, B_KV_HEADS))
```

```python
import functools
import math

import jax
import jax.numpy as jnp
from jax import lax
from jax.experimental import pallas as pl
from jax.experimental.pallas import tpu as pltpu

D_MODEL = 2048
CHUNK = 64
HEAD_DIM = 64
A_HEADS = 16
A_WIDTH = A_HEADS * HEAD_DIM
A_LEFT_CHUNKS = 8
A_REL_CLIP = 256
B_HEADS = 16
B_KV_HEADS = 2
B_GROUP = B_HEADS // B_KV_HEADS
B_WIDTH = B_HEADS * HEAD_DIM
B_KV_WIDTH = B_KV_HEADS * HEAD_DIM
B_LEFT_CHUNKS = 2
T5_BUCKETS = 32
T5_MAX_DIST = 128
EPS = 1e-6
NEG_INF = -1e30
LOG2E = math.log2(math.e)
Q_SCALE = HEAD_DIM ** -0.5 * LOG2E

LANES = 128
MXU_COLS = 256

COL_QA = 0
COL_KA = A_WIDTH
COL_VA = 2 * A_WIDTH
COL_QB = 3 * A_WIDTH
COL_KB = 3 * A_WIDTH + B_WIDTH
COL_GA = COL_KB + 2 * B_KV_WIDTH
PROJ_TN = 4 * MXU_COLS
QKV_KA, QKV_VA, QKV_QA, QKV_QB = 0, 1, 2, 3

A_TQ = 256
B_TQ = 128
A_KBLOCKS = A_LEFT_CHUNKS * CHUNK // A_TQ + 1
B_KBLOCKS = B_LEFT_CHUNKS * CHUNK // B_TQ + 1

MERGE_SLAB = 256

VMEM_LIMIT = 56 * 1024 * 1024


def _params(sem, vmem=VMEM_LIMIT):
    return pltpu.CompilerParams(dimension_semantics=sem, vmem_limit_bytes=vmem)


def _rms_scale(x, g):
    return x * lax.rsqrt(jnp.mean(x * x, axis=-1, keepdims=True) + EPS) * g


def _layer_vec_spec(l, d):
    return pl.BlockSpec((None, 1, d), lambda *_: (l, 0, 0))


PROJ_KV_STEPS = 2
PROJ_Q_STEPS = 4
PROJ_GATE_STEP0 = 4


def _proj_first_blocks():
    n_w = PROJ_TN // MXU_COLS
    blocks = [c // MXU_COLS for c in (COL_KA, COL_VA, COL_QA, COL_QB)]
    return blocks + [COL_GA // MXU_COLS + g * n_w for g in range(2 * D_MODEL // PROJ_TN)]


def _in_proj_kernel(x_ref, g_ref, *refs, from_f32):
    n_w = PROJ_TN // MXU_COLS
    w_refs, wkvb_ref = refs[:n_w], refs[n_w]
    qkv_ref, kvb16_ref, gates_ref, kva32_ref, kvb32_ref = refs[n_w + 1:n_w + 6]
    pack_refs, h_ref = refs[n_w + 6:-1], refs[-1]
    j = pl.program_id(1)
    tm = x_ref.shape[0]

    def weight(k):
        w_ref = wkvb_ref if k == n_w else w_refs[k]
        if not from_f32:
            return w_ref[...]
        w = w_ref[...].astype(jnp.bfloat16)
        pack_refs[k][...] = w
        return w

    def tile(out16_ref, out32_ref, scale=None, rows=slice(None)):
        h = h_ref[rows]
        for k in range(n_w):
            cols = slice(k * MXU_COLS, (k + 1) * MXU_COLS)
            acc = jnp.dot(h, weight(k), preferred_element_type=jnp.float32)
            if out32_ref is not None:
                out32_ref[rows, cols] = acc
            if scale is not None:
                acc = acc * scale
            out16_ref[rows, cols] = acc.astype(out16_ref.dtype)

    @pl.when(j == 0)
    def _():
        slab = tm // 2 if tm >= 2 * MXU_COLS else tm
        for r0 in range(0, tm, slab):
            rows = slice(r0, r0 + slab)
            h_ref[rows] = _rms_scale(x_ref[rows], g_ref[...]).astype(h_ref.dtype)
            tile(qkv_ref, kva32_ref, rows=rows)

    @pl.when((j > 0) & (j < PROJ_KV_STEPS))
    def _():
        tile(qkv_ref, kva32_ref)

    @pl.when((j >= PROJ_KV_STEPS) & (j < PROJ_Q_STEPS))
    def _():
        tile(qkv_ref, None, Q_SCALE)

    @pl.when(j == PROJ_Q_STEPS - 1)
    def _():
        acc = jnp.dot(h_ref[...], weight(n_w), preferred_element_type=jnp.float32)
        kvb16_ref[...] = acc.astype(kvb16_ref.dtype)
        kvb32_ref[...] = acc

    @pl.when(j >= PROJ_GATE_STEP0)
    def _():
        tile(gates_ref, None)


def _in_proj(x, g, w, l, *, tm, from_f32=False):
    m, d = x.shape
    n_w = PROJ_TN // MXU_COLS
    n_gate = 2 * D_MODEL // PROJ_TN
    steps = PROJ_GATE_STEP0 + n_gate
    first = _proj_first_blocks()
    kvb_n = 2 * B_KV_WIDTH

    def f32_map(k):
        def index_map(i, j):
            blk = first[-1] + k
            for step in range(steps - 2, -1, -1):
                blk = jnp.where(j == step, first[step] + k, blk)
            return (l, 0, blk)
        return index_map

    packed_spec = pl.BlockSpec((d, MXU_COLS), lambda i, j: (0, j))
    packed_kvb_spec = pl.BlockSpec((d, kvb_n), lambda i, j: (0, 0))
    if from_f32:
        assert m == tm, "the packed weights are written once per row tile"
        w_specs = [pl.BlockSpec((None, d, MXU_COLS), f32_map(k)) for k in range(n_w)]
        w_specs.append(pl.BlockSpec((None, d, kvb_n), lambda i, j: (l, 0, COL_KB // kvb_n)))
        w_args = [w] * (n_w + 1)
        pack_specs = [packed_spec] * n_w + [packed_kvb_spec]
        pack_shapes = [jax.ShapeDtypeStruct((d, steps * MXU_COLS), jnp.bfloat16)] * n_w
        pack_shapes.append(jax.ShapeDtypeStruct((d, kvb_n), jnp.bfloat16))
    else:
        w_specs = [packed_spec] * n_w + [packed_kvb_spec]
        w_args = list(w)
        pack_specs, pack_shapes = [], []
    outs = pl.pallas_call(
        functools.partial(_in_proj_kernel, from_f32=from_f32),
        grid=(m // tm, steps),
        in_specs=[
            pl.BlockSpec((tm, d), lambda i, j: (i, 0)),
            _layer_vec_spec(l, d),
        ] + w_specs,
        out_specs=[
            pl.BlockSpec((tm, PROJ_TN), lambda i, j: (i, jnp.minimum(j, PROJ_Q_STEPS - 1))),
            pl.BlockSpec((tm, kvb_n), lambda i, j: (i, 0)),
            pl.BlockSpec((tm, PROJ_TN),
                         lambda i, j: (i, jnp.clip(j - PROJ_GATE_STEP0, 0, n_gate - 1))),
            pl.BlockSpec((tm, PROJ_TN), lambda i, j: (i, jnp.minimum(j, PROJ_KV_STEPS - 1))),
            pl.BlockSpec((tm, kvb_n), lambda i, j: (i, 0)),
        ] + pack_specs,
        out_shape=[
            jax.ShapeDtypeStruct((m, PROJ_Q_STEPS * PROJ_TN), jnp.bfloat16),
            jax.ShapeDtypeStruct((m, kvb_n), jnp.bfloat16),
            jax.ShapeDtypeStruct((m, 2 * D_MODEL), jnp.bfloat16),
            jax.ShapeDtypeStruct((m, PROJ_KV_STEPS * PROJ_TN), jnp.float32),
            jax.ShapeDtypeStruct((m, kvb_n), jnp.float32),
        ] + pack_shapes,
        scratch_shapes=[pltpu.VMEM((tm, d), jnp.bfloat16)],
        compiler_params=_params(("parallel", "arbitrary")),
        name="in_proj",
    )(x, g, *w_args)
    return (outs[:5], tuple(outs[5:])) if from_f32 else outs


def _low_half():
    return lax.broadcasted_iota(jnp.int32, (1, LANES), 1) < HEAD_DIM


def _start_mask(tile_in_seq, n_kblocks, tq):
    lane = lax.broadcasted_iota(jnp.int32, (1, n_kblocks * tq), 1)
    first_valid = jnp.maximum(n_kblocks - 1 - tile_in_seq, 0) * tq
    return jnp.where(lane >= first_valid, 0.0, NEG_INF)


def _pack_w_in_slab(w32_ref, pack_refs):
    slab = w32_ref[...].astype(jnp.bfloat16)
    block = lambda b: slab[:, b * MXU_COLS:(b + 1) * MXU_COLS]
    for k, pack_ref in enumerate(pack_refs[:-1]):
        pack_ref[...] = jnp.concatenate([block(b + k) for b in _proj_first_blocks()], axis=1)
    pack_refs[-1][...] = block(COL_KB // MXU_COLS)


def _attn_a_kernel(q_ref, *refs, tiles_per_seq, n_cast, packing):
    k_refs = refs[:A_KBLOCKS]
    v_refs = refs[A_KBLOCKS:2 * A_KBLOCKS]
    bias_ref = refs[2 * A_KBLOCKS]
    rest = refs[2 * A_KBLOCKS + 1:]
    w32_refs, rest = rest[:n_cast], rest[n_cast:]
    if packing:
        w_in_ref, rest = rest[0], rest[1:]
    o_ref, w16_refs, pack_refs = rest[0], rest[1:1 + n_cast], rest[1 + n_cast:]
    tile_in_seq = pl.program_id(0) % tiles_per_seq
    low = _low_half()

    def heads(start):
        for w32_ref, w16_ref in zip(w32_refs, w16_refs):
            w16_ref[...] = w32_ref[...].astype(w16_ref.dtype)
        if packing:
            _pack_w_in_slab(w_in_ref, pack_refs)
        def scores(p):
            sl = slice(p * LANES, (p + 1) * LANES)
            q2 = q_ref[:, sl]
            k2 = jnp.concatenate([r[:, sl] for r in k_refs], axis=0)
            qs = jnp.concatenate([jnp.where(low, q2, 0), jnp.where(low, 0, q2)], axis=0)
            return lax.dot_general(qs, k2, (((1,), (1,)), ((), ())),
                                   preferred_element_type=jnp.float32)

        n_pairs = A_HEADS // 2
        ahead = 2
        pending = [scores(p) for p in range(ahead)]
        for p in range(n_pairs):
            sl = slice(p * LANES, (p + 1) * LANES)
            v2 = jnp.concatenate([r[:, sl] for r in v_refs], axis=0)
            if p + ahead < n_pairs:
                pending.append(scores(p + ahead))
            s_all = pending.pop(0)
            probs = []
            for hh in range(2):
                s = s_all[hh * A_TQ:(hh + 1) * A_TQ] + bias_ref[2 * p + hh]
                if start is not None:
                    s = s + start
                s = s.astype(v2.dtype)
                probs.append(jnp.exp2(s - jnp.max(s, axis=-1, keepdims=True)))
            v_ones = jnp.concatenate([v2, jnp.ones_like(v2)], axis=1)
            r = jnp.dot(jnp.concatenate(probs, axis=0), v_ones,
                        preferred_element_type=jnp.float32)
            out = r[:, :LANES] / r[:, LANES:]
            o_ref[:, sl] = jnp.where(low, out[:A_TQ], out[A_TQ:]).astype(o_ref.dtype)

    @pl.when(tile_in_seq >= A_KBLOCKS - 1)
    def _():
        heads(None)

    @pl.when(tile_in_seq < A_KBLOCKS - 1)
    def _():
        heads(_start_mask(tile_in_seq, A_KBLOCKS, A_TQ))


def _band_kv_map(j, col, n_kblocks, tiles_per_seq):
    def index_map(t, *_):
        i = t % tiles_per_seq
        return (t - i + jnp.maximum(i - (n_kblocks - 1) + j, 0), col)
    return index_map


def _attn_a(qkv, bias, weights, l, w_in_next=None, *, m, seq):
    tiles_per_seq = seq // A_TQ
    steps = m // A_TQ
    kv_spec = lambda j, col: pl.BlockSpec(
        (A_TQ, A_WIDTH), _band_kv_map(j, col, A_KBLOCKS, tiles_per_seq))
    slab32 = lambda w, layer: pl.BlockSpec((None, w.shape[1] // steps, w.shape[2]),
                                           lambda t: (layer, t, 0))
    slab16 = lambda rows, width: pl.BlockSpec((rows // steps, width), lambda t: (t, 0))
    in_specs = ([pl.BlockSpec((A_TQ, A_WIDTH), lambda t: (t, QKV_QA))]
                + [kv_spec(j, QKV_KA) for j in range(A_KBLOCKS)]
                + [kv_spec(j, QKV_VA) for j in range(A_KBLOCKS)]
                + [pl.BlockSpec((A_HEADS, A_TQ, A_KBLOCKS * A_TQ), lambda t: (l, 0, 0),
                                pipeline_mode=pl.Buffered(1))]
                + [slab32(w, l) for w in weights])
    out_specs = ([pl.BlockSpec((A_TQ, A_WIDTH), lambda t: (t, 0))]
                 + [slab16(w.shape[1], w.shape[2]) for w in weights])
    out_shape = ([jax.ShapeDtypeStruct((m, A_WIDTH), jnp.bfloat16)]
                 + [jax.ShapeDtypeStruct(w.shape[1:], jnp.bfloat16) for w in weights])
    args = [qkv] * (2 * A_KBLOCKS + 1) + [bias, *weights]
    if w_in_next is not None:
        w_in, layer = w_in_next
        in_specs.append(slab32(w_in, layer))
        args.append(w_in)
        widths = [len(_proj_first_blocks()) * MXU_COLS] * (PROJ_TN // MXU_COLS)
        widths.append(2 * B_KV_WIDTH)
        out_specs += [slab16(w_in.shape[1], width) for width in widths]
        out_shape += [jax.ShapeDtypeStruct((w_in.shape[1], width), jnp.bfloat16)
                      for width in widths]
    return pl.pallas_call(
        functools.partial(_attn_a_kernel, tiles_per_seq=tiles_per_seq,
                          n_cast=len(weights), packing=w_in_next is not None),
        grid=(steps,),
        in_specs=in_specs,
        out_specs=out_specs,
        out_shape=out_shape,
        compiler_params=_params(("parallel",)),
        name="attn_a",
    )(*args)


def _dup_half(x, g):
    swapped = pltpu.roll(x, HEAD_DIM, 1)
    low = _low_half()
    return jnp.where(low, x, swapped) if g == 0 else jnp.where(low, swapped, x)


def _attn_b_kernel(sink_ref, q_ref, *refs, tiles_per_seq):
    k_refs = refs[:B_KBLOCKS]
    v_refs = refs[B_KBLOCKS:2 * B_KBLOCKS]
    bias_ref, o_ref = refs[2 * B_KBLOCKS:]
    start = _start_mask(pl.program_id(0) % tiles_per_seq, B_KBLOCKS, B_TQ)
    low = _low_half()
    k2 = jnp.concatenate([r[...] for r in k_refs], axis=0).astype(jnp.float32)
    v2 = jnp.concatenate([r[...] for r in v_refs], axis=0).astype(jnp.float32)
    pairs = B_GROUP // 2
    for g in range(B_KV_HEADS):
        kd = _dup_half(k2, g).astype(jnp.bfloat16)
        vd = _dup_half(v2, g).astype(jnp.bfloat16)
        stacked = []
        for p in range(pairs):
            c0 = (g * pairs + p) * LANES
            q2 = q_ref[:, c0:c0 + LANES]
            stacked.append(jnp.where(low, q2, 0))
            stacked.append(jnp.where(low, 0, q2))
        qs = jnp.concatenate(stacked, axis=0)
        s_all = lax.dot_general(qs, kd, (((1,), (1,)), ((), ())),
                                preferred_element_type=jnp.float32)
        probs, sink_terms = [], []
        for hl in range(B_GROUP):
            h = g * B_GROUP + hl
            s = s_all[hl * B_TQ:(hl + 1) * B_TQ] + bias_ref[h] + start
            sink = sink_ref[h]
            mx = jnp.maximum(jnp.max(s, axis=-1, keepdims=True), sink)
            probs.append(jnp.exp2(s - mx).astype(jnp.bfloat16))
            sink_terms.append(jnp.exp2(sink - mx))
        v_ones = jnp.concatenate([vd, jnp.ones_like(vd)], axis=1)
        r = jnp.dot(jnp.concatenate(probs, axis=0), v_ones,
                    preferred_element_type=jnp.float32)
        outs = []
        for hl in range(B_GROUP):
            rh = r[hl * B_TQ:(hl + 1) * B_TQ]
            outs.append(rh[:, :LANES] / (rh[:, LANES:] + sink_terms[hl]))
        for p in range(pairs):
            c0 = (g * pairs + p) * LANES
            o_ref[:, c0:c0 + LANES] = jnp.where(
                low, outs[2 * p], outs[2 * p + 1]).astype(o_ref.dtype)


def _attn_b(qkv, kvb16, bias, sinks, *, m, seq):
    tiles_per_seq = seq // B_TQ
    kv_spec = lambda j, col: pl.BlockSpec(
        (B_TQ, LANES), _band_kv_map(j, col, B_KBLOCKS, tiles_per_seq))
    return pl.pallas_call(
        functools.partial(_attn_b_kernel, tiles_per_seq=tiles_per_seq),
        grid_spec=pltpu.PrefetchScalarGridSpec(
            num_scalar_prefetch=1,
            grid=(m // B_TQ,),
            in_specs=[pl.BlockSpec((B_TQ, B_WIDTH), lambda t, s: (t, QKV_QB))]
            + [kv_spec(j, 0) for j in range(B_KBLOCKS)]
            + [kv_spec(j, 1) for j in range(B_KBLOCKS)]
            + [pl.BlockSpec((B_HEADS, B_TQ, B_KBLOCKS * B_TQ), lambda t, s: (0, 0, 0))],
            out_specs=pl.BlockSpec((B_TQ, B_WIDTH), lambda t, s: (t, 0)),
        ),
        out_shape=jax.ShapeDtypeStruct((m, B_WIDTH), jnp.bfloat16),
        compiler_params=_params(("parallel",)),
        name="attn_b",
    )(sinks, qkv, *([kvb16] * (2 * B_KBLOCKS)), bias)


def _pad_top(x, t_new):
    return jnp.concatenate([jnp.zeros((LANES - t_new, x.shape[1]), x.dtype), x], axis=0)


def _roll_in(cache_t, new_rows, out_ref, layer, t_new):
    n = cache_t.shape[1]
    shifted = pltpu.roll(cache_t, n - t_new, 1)
    lane = lax.broadcasted_iota(jnp.int32, (1, LANES), 1)
    tail = jnp.where(lane < LANES - t_new, shifted[:, n - LANES:],
                     _pad_top(new_rows, t_new).T)
    if n > LANES:
        out_ref[layer, :, :n - LANES] = shifted[:, :n - LANES]
    out_ref[layer, :, n - LANES:] = tail


def _attn_sample_kernel(qa_ref, qb_ref, kan_ref, van_ref, kbn_ref, vbn_ref,
                        cak_ref, cav_ref, cbk_ref, cbv_ref,
                        bias_ac_ref, bias_an_ref, bias_bc_ref, bias_bn_ref, sink_ref,
                        *refs, t_new, layer, rolling):
    if rolling:
        prev_refs, (oa_ref, ob_ref), out_refs = refs[:4], refs[4:6], refs[6:]
        for cache_ref, new_ref, prev_ref, out_ref in zip(
                (cak_ref, cav_ref, cbk_ref, cbv_ref), (kan_ref, van_ref, kbn_ref, vbn_ref),
                prev_refs, out_refs):
            for l2 in range(layer + 1):
                rows_new = new_ref[...] if l2 == layer else prev_ref[l2]
                _roll_in(cache_ref[l2], rows_new, out_ref, l2, t_new)
        cak_ref, cav_ref, cbk_ref, cbv_ref = (
            r.at[layer] for r in (cak_ref, cav_ref, cbk_ref, cbv_ref))
    else:
        oa_ref, ob_ref = refs
    heads = A_HEADS
    rows = heads * t_new
    width = heads * HEAD_DIM
    bf16 = jnp.bfloat16
    row_head = lax.broadcasted_iota(jnp.int32, (rows, width), 0) // t_new
    lane_head = lax.broadcasted_iota(jnp.int32, (rows, width), 1) // HEAD_DIM
    own = row_head == lane_head

    def stack_q(q):
        return jnp.where(own, jnp.concatenate([q] * heads, axis=0), 0)

    def nt_dot(a, b):
        return lax.dot_general(a, b, (((1,), (1,)), ((), ())),
                               preferred_element_type=jnp.float32)

    def finish(s_c, s_n, vt_c, v_n, sink):
        mx = jnp.maximum(jnp.max(s_c, axis=-1, keepdims=True),
                         jnp.max(s_n, axis=-1, keepdims=True))
        if sink is not None:
            mx = jnp.maximum(mx, sink)
        e_c = jnp.exp2(s_c - mx)
        e_n = jnp.exp2(s_n - mx)
        denom = jnp.sum(e_c, axis=-1, keepdims=True) + jnp.sum(e_n, axis=-1, keepdims=True)
        if sink is not None:
            denom = denom + jnp.exp2(sink - mx)
        o = nt_dot(e_c.astype(bf16), vt_c)
        o = o + jnp.dot(e_n.astype(bf16), v_n, preferred_element_type=jnp.float32)
        o = jnp.where(own, o / denom, 0.0)
        return jnp.sum(o.reshape(heads, t_new, width), axis=0)

    k_n = _pad_top(kan_ref[...], t_new).astype(bf16)
    v_n = _pad_top(van_ref[...], t_new).astype(bf16)
    qs = stack_q(qa_ref[...])
    s_c = jnp.dot(qs, cak_ref[...].astype(bf16),
                  preferred_element_type=jnp.float32) + bias_ac_ref[...]
    s_n = nt_dot(qs, k_n) + bias_an_ref[...]
    oa_ref[...] = finish(s_c, s_n, cav_ref[...].astype(bf16), v_n, None).astype(oa_ref.dtype)

    def expansion(shape, src_axis):
        src = lax.broadcasted_iota(jnp.int32, shape, src_axis)
        dst = lax.broadcasted_iota(jnp.int32, shape, 1 - src_axis)
        return ((src % HEAD_DIM == dst % HEAD_DIM)
                & (src // HEAD_DIM == dst // (B_GROUP * HEAD_DIM))).astype(bf16)

    def widen(x):
        return jnp.dot(x.astype(bf16), expansion((B_KV_WIDTH, width), 0),
                       preferred_element_type=jnp.float32).astype(bf16)

    def widen_t(xt):
        return jnp.dot(expansion((width, B_KV_WIDTH), 1), xt.astype(bf16),
                       preferred_element_type=jnp.float32).astype(bf16)

    k_n = widen(_pad_top(kbn_ref[...], t_new))
    v_n = widen(_pad_top(vbn_ref[...], t_new))
    qs = stack_q(qb_ref[...])
    s_c = jnp.dot(qs, widen_t(cbk_ref[...]),
                  preferred_element_type=jnp.float32) + bias_bc_ref[...]
    s_n = nt_dot(qs, k_n) + bias_bn_ref[...]
    ob_ref[...] = finish(s_c, s_n, widen_t(cbv_ref[...]), v_n, sink_ref[...]).astype(ob_ref.dtype)


def _attn_sample(qkv, kva32, kvb32, caches, l, bias_ac, bias_an, bias_bc, bias_bn,
                 sink_col, prev_new=None, *, t_new):
    m = qkv.shape[0]
    cak, cav, cbk, cbv = caches
    a_len, b_len = cak.shape[-1], cbk.shape[-1]
    rows = A_HEADS * t_new
    rolling = prev_new is not None
    const = lambda shape: pl.BlockSpec(shape, lambda b: (0,) * len(shape))
    if rolling:
        assert l == cak.shape[0] - 1 and prev_new[0].shape[0] == l
        cache_spec = lambda c: pl.BlockSpec((c.shape[0], None) + c.shape[2:],
                                            lambda b: (0, b, 0, 0))
    else:
        cache_spec = lambda c: pl.BlockSpec((None, None) + c.shape[2:],
                                            lambda b: (l, b, 0, 0))
    in_specs = [
        pl.BlockSpec((t_new, A_WIDTH), lambda b: (b, QKV_QA)),
        pl.BlockSpec((t_new, B_WIDTH), lambda b: (b, QKV_QB)),
        pl.BlockSpec((t_new, A_WIDTH), lambda b: (b, 0)),
        pl.BlockSpec((t_new, A_WIDTH), lambda b: (b, 1)),
        pl.BlockSpec((t_new, B_KV_WIDTH), lambda b: (b, 0)),
        pl.BlockSpec((t_new, B_KV_WIDTH), lambda b: (b, 1)),
        cache_spec(cak), cache_spec(cav), cache_spec(cbk), cache_spec(cbv),
        pl.BlockSpec((rows, a_len), lambda b: (l, 0)),
        pl.BlockSpec((rows, LANES), lambda b: (l, 0)),
        const((rows, b_len)), const((rows, LANES)),
        const((rows, 1)),
    ]
    args = [qkv, qkv, kva32, kva32, kvb32, kvb32, cak, cav, cbk, cbv,
            bias_ac, bias_an, bias_bc, bias_bn, sink_col]
    out_specs = [
        pl.BlockSpec((t_new, A_WIDTH), lambda b: (b, 0)),
        pl.BlockSpec((t_new, B_WIDTH), lambda b: (b, 0)),
    ]
    out_shape = [
        jax.ShapeDtypeStruct((m, A_WIDTH), jnp.bfloat16),
        jax.ShapeDtypeStruct((m, B_WIDTH), jnp.bfloat16),
    ]
    if rolling:
        new_a, new_b = prev_new
        prev_spec = lambda width, col: pl.BlockSpec((l, t_new, width), lambda b: (0, b, col))
        in_specs += [prev_spec(A_WIDTH, 0), prev_spec(A_WIDTH, 1),
                     prev_spec(B_KV_WIDTH, 0), prev_spec(B_KV_WIDTH, 1)]
        args += [new_a, new_a, new_b, new_b]
        out_specs += [cache_spec(c) for c in caches]
        out_shape += [jax.ShapeDtypeStruct(c.shape, jnp.float32) for c in caches]
    return pl.pallas_call(
        functools.partial(_attn_sample_kernel, t_new=t_new, layer=l, rolling=rolling),
        grid=(m // t_new,),
        in_specs=in_specs,
        out_specs=out_specs,
        out_shape=out_shape,
        compiler_params=_params(("parallel",)),
        name="attn_sample",
    )(*args)


def _prompt_caches_kernel(*refs, depth):
    in_refs, (ak_ref, av_ref, bk_ref, bv_ref) = refs[:4 * depth], refs[4 * depth:]
    for k in range(depth):
        @pl.when(pl.program_id(0) == k)
        def _(k=k):
            ka_ref, va_ref, kb_ref, vb_ref = in_refs[4 * k:4 * k + 4]
            ak_ref[...] = ka_ref[...].T
            av_ref[...] = va_ref[...].T
            bk_ref[...] = kb_ref[...].T
            bv_ref[...] = vb_ref[...].T


def _prompt_caches(kva32s, kvb32s, *, batch, seq, na, nb):
    depth = len(kva32s)

    def rows_map(k, tile, col):
        def index_map(l, b):
            bb = jnp.where(l < k, 0, jnp.where(l > k, batch - 1, b))
            return ((bb + 1) * tile - 1, col)
        return index_map

    in_specs, args = [], []
    for k in range(depth):
        in_specs += [
            pl.BlockSpec((na, A_WIDTH), rows_map(k, seq // na, 0)),
            pl.BlockSpec((na, A_WIDTH), rows_map(k, seq // na, 1)),
            pl.BlockSpec((nb, B_KV_WIDTH), rows_map(k, seq // nb, 0)),
            pl.BlockSpec((nb, B_KV_WIDTH), rows_map(k, seq // nb, 1)),
        ]
        args += [kva32s[k], kva32s[k], kvb32s[k], kvb32s[k]]
    out_shapes = [(depth, batch, A_WIDTH, na)] * 2 + [(depth, batch, B_KV_WIDTH, nb)] * 2
    return pl.pallas_call(
        functools.partial(_prompt_caches_kernel, depth=depth),
        grid=(depth, batch),
        in_specs=in_specs,
        out_specs=[pl.BlockSpec((None, None) + s[2:], lambda l, b: (l, b, 0, 0))
                   for s in out_shapes],
        out_shape=[jax.ShapeDtypeStruct(s, jnp.float32) for s in out_shapes],
        compiler_params=_params(("arbitrary", "arbitrary")),
        name="prompt_caches",
    )(*args)


def _merge_kernel(oa_ref, ob_ref, ga_ref, gb_ref, x_ref, wa_ref, wb_ref, wo_ref,
                  g_ref, gnext_ref, y_ref, h_ref):
    tm = x_ref.shape[0]
    slab = min(tm, MERGE_SLAB)
    for r0 in range(0, tm, slab):
        rows = slice(r0, r0 + slab)
        ta = jnp.dot(oa_ref[rows], wa_ref[...], preferred_element_type=jnp.float32)
        tb = jnp.dot(ob_ref[rows], wb_ref[...], preferred_element_type=jnp.float32)
        mixed = (jax.nn.sigmoid(ga_ref[rows].astype(jnp.float32)) * ta
                 + jax.nn.sigmoid(gb_ref[rows].astype(jnp.float32)) * tb)
        z = jnp.dot(mixed.astype(jnp.bfloat16), wo_ref[...],
                    preferred_element_type=jnp.float32)
        y = x_ref[rows] + _rms_scale(z, g_ref[...])
        y_ref[rows] = y
        h_ref[rows] = _rms_scale(y, gnext_ref[...]).astype(h_ref.dtype)


def _merge(oa, ob, gates, x, wa, wb, wo, g, g_next, l, *, tm):
    m, d = x.shape
    resident = lambda w: pl.BlockSpec(w.shape, lambda i: (0, 0),
                                      pipeline_mode=pl.Buffered(1))
    return pl.pallas_call(
        _merge_kernel,
        grid=(m // tm,),
        in_specs=[
            pl.BlockSpec((tm, A_WIDTH), lambda i: (i, 0)),
            pl.BlockSpec((tm, B_WIDTH), lambda i: (i, 0)),
            pl.BlockSpec((tm, d), lambda i: (i, 0)),
            pl.BlockSpec((tm, d), lambda i: (i, 1)),
            pl.BlockSpec((tm, d), lambda i: (i, 0)),
            resident(wa), resident(wb), resident(wo),
            _layer_vec_spec(l, d),
            _layer_vec_spec(l, d),
        ],
        out_specs=[pl.BlockSpec((tm, d), lambda i: (i, 0))] * 2,
        out_shape=[jax.ShapeDtypeStruct((m, d), jnp.float32),
                   jax.ShapeDtypeStruct((m, d), jnp.bfloat16)],
        compiler_params=_params(("parallel",)),
        name="merge",
    )(oa, ob, gates, gates, x, wa, wb, wo, g, g_next)


def _ffn_kernel(x_ref, h_ref, wu_ref, wd_ref, gpost_ref, y_ref):
    f = pl.program_id(1)
    last = pl.num_programs(1) - 1

    @pl.when(f == 0)
    def _():
        y_ref[...] = jnp.zeros_like(y_ref)

    def partial(rows):
        u = jnp.dot(h_ref[rows], wu_ref[...], preferred_element_type=jnp.float32)
        u = jnp.square(jnp.maximum(u, 0.0)).astype(jnp.bfloat16)
        return y_ref[rows] + jnp.dot(u, wd_ref[...], preferred_element_type=jnp.float32)

    @pl.when(f < last)
    def _():
        y_ref[...] = partial(slice(None))

    @pl.when(f == last)
    def _():
        tm = x_ref.shape[0]
        slab = tm // 2 if tm >= 2 * MXU_COLS else tm
        for r0 in range(0, tm, slab):
            rows = slice(r0, r0 + slab)
            y_ref[rows] = x_ref[rows] + _rms_scale(partial(rows), gpost_ref[...])


def _ffn(x, h, wu, wd, gpost, l, *, tm, tf):
    m, d = x.shape
    dff = wu.shape[-1]
    return pl.pallas_call(
        _ffn_kernel,
        grid=(m // tm, dff // tf),
        in_specs=[
            pl.BlockSpec((tm, d), lambda i, f: (i, 0)),
            pl.BlockSpec((tm, d), lambda i, f: (i, 0)),
            pl.BlockSpec((d, tf), lambda i, f: (0, f)),
            pl.BlockSpec((tf, d), lambda i, f: (f, 0)),
            _layer_vec_spec(l, d),
        ],
        out_specs=pl.BlockSpec((tm, d), lambda i, f: (i, 0)),
        out_shape=jax.ShapeDtypeStruct((m, d), jnp.float32),
        compiler_params=_params(("parallel", "arbitrary")),
        name="ffn",
    )(x, h, wu, wd, gpost)


def _t5_bucket(rel):
    half = T5_BUCKETS // 2
    exact = half // 2
    ret = jnp.where(rel > 0, half, 0)
    n = jnp.abs(rel)
    large = exact + (jnp.log(jnp.maximum(n, 1).astype(jnp.float32) / exact)
                     / math.log(T5_MAX_DIST / exact) * (half - exact)).astype(jnp.int32)
    large = jnp.minimum(large, half - 1)
    return ret + jnp.where(n < exact, n, large)


def _a_bias_of_rel(table):
    scaled = table.T * LOG2E
    return lambda rel: scaled[:, jnp.clip(rel, -A_REL_CLIP, A_REL_CLIP) + A_REL_CLIP]


def _b_bias_of_rel(table):
    scaled = table.T * LOG2E
    return lambda rel: scaled[:, _t5_bucket(-rel)]


def _hankel(u, q, n):
    heads, k = u.shape
    period = q + n
    u = jnp.pad(u, ((0, 0), (0, period - k)))
    flat = jnp.tile(u, (1, q + 1))[:, :q * (period + 1)]
    return flat.reshape(heads, q, period + 1)[:, :, :n]


def _rel_bias(bias_of_rel, q_len, n_keys, k0):
    k = jnp.arange(q_len + n_keys - 1)
    u = bias_of_rel(k - (n_keys - 1) - k0)
    return _hankel(u, q_len, n_keys)[:, :, ::-1]


def _band_tile(bias_of_rel, tq, n_prev):
    window = tq + n_prev * CHUNK
    band = _rel_bias(bias_of_rel, CHUNK, (n_prev + 1) * CHUNK, -n_prev * CHUNK)
    blocks = []
    for c in range(tq // CHUNK):
        left = c * CHUNK
        right = window - left - band.shape[-1]
        blocks.append(jnp.pad(band, ((0, 0), (0, 0), (left, right)),
                              constant_values=NEG_INF))
    return jnp.concatenate(blocks, axis=1)


def _sample_bias(bias_of_rel, t_new, n_cache):
    full = _rel_bias(bias_of_rel, t_new, n_cache + t_new, -n_cache)
    full = full.reshape(-1, n_cache + t_new)
    new = jnp.pad(full[:, n_cache:], ((0, 0), (LANES - t_new, 0)),
                  constant_values=NEG_INF)
    return full[:, :n_cache], new


def _cache_t(c):
    d, s, r, h, e = c.shape
    return jnp.transpose(c, (0, 1, 3, 4, 2)).reshape(d, s, h * e, r)


def _cache_untranspose(c, heads):
    d, s, w, r = c.shape
    return jnp.transpose(c.reshape(d, s, heads, w // heads, r), (0, 1, 4, 2, 3))


def kernel(x_prompt, x_sample, cache_a_k, cache_a_v, cache_b_k, cache_b_v, w_in,
           w_a_out, w_b_out, w_out, a_rel_table, t5_table, b_sinks, g_mix_pre,
           g_mix_post, g_ffn_pre, g_ffn_post, w_up, w_down):
    depth = w_in.shape[0]
    assert depth >= 2, "the last layer's sample kernel rolls the earlier layers' caches"
    batch, seq, d = x_prompt.shape
    dec_batch, t_new, _ = x_sample.shape
    a_len = cache_a_k.shape[2]
    b_len = cache_b_k.shape[2]
    mp = batch * seq
    ms = dec_batch * t_new

    yp = x_prompt.reshape(mp, d)
    ys = x_sample.reshape(ms, d)

    w_pack = None
    later_weights = (w_a_out, w_b_out, w_out, w_up, w_down)
    g_pre = g_mix_pre[:, None, :]
    g_post = g_mix_post[:, None, :]
    gf_pre = g_ffn_pre[:, None, :]
    gf_post = g_ffn_post[:, None, :]
    caches = tuple(_cache_t(c) for c in (cache_a_k, cache_a_v, cache_b_k, cache_b_v))

    b_of_rel = _b_bias_of_rel(t5_table)
    bias_b_tile = _band_tile(b_of_rel, B_TQ, B_LEFT_CHUNKS)
    bias_bc, bias_bn = _sample_bias(b_of_rel, t_new, b_len)
    a_of_rel = _a_bias_of_rel(jnp.concatenate([a_rel_table[l] for l in range(depth)], axis=1))
    bias_a_tile = _band_tile(a_of_rel, A_TQ, A_LEFT_CHUNKS)
    bias_ac, bias_an = _sample_bias(a_of_rel, t_new, a_len)

    prompt_kva, prompt_kvb, sample_kva, sample_kvb = [], [], [], []
    for l in range(depth):
        sinks = b_sinks[l] * LOG2E

        if w_pack is None:
            sample_proj, w_pack = _in_proj(ys, g_pre, w_in, l, tm=ms, from_f32=True)
        else:
            sample_proj = _in_proj(ys, g_pre, w_pack, l, tm=ms)

        qkv, kvb16, gates, kva32, kvb32 = _in_proj(yp, g_pre, w_pack, l, tm=1024)
        oa, wa16, wb16, wo16, wu16, wd16, *next_pack = _attn_a(
            qkv, bias_a_tile, later_weights, l, (w_in, l + 1) if l + 1 < depth else None,
            m=mp, seq=seq)
        w_pack = tuple(next_pack)
        ob = _attn_b(qkv, kvb16, bias_b_tile, sinks, m=mp, seq=seq)
        yp, hp = _merge(oa, ob, gates, yp, wa16, wb16, wo16, g_post, gf_pre, l, tm=512)
        yp = _ffn(yp, hp, wu16, wd16, gf_post, l, tm=512, tf=1024)
        prompt_kva.append(kva32)
        prompt_kvb.append(kvb32)

        qkv, _, gates, kva32, kvb32 = sample_proj
        sink_col = jnp.repeat(sinks, t_new)[:, None]
        if l + 1 < depth:
            oa, ob = _attn_sample(qkv, kva32, kvb32, caches, l, bias_ac, bias_an, bias_bc,
                                  bias_bn, sink_col, t_new=t_new)
            sample_kva.append(kva32)
            sample_kvb.append(kvb32)
        else:
            prev_new = (jnp.stack(sample_kva), jnp.stack(sample_kvb))
            oa, ob, aks, avs, bks, bvs = _attn_sample(
                qkv, kva32, kvb32, caches, l, bias_ac, bias_an, bias_bc, bias_bn,
                sink_col, prev_new, t_new=t_new)
        ys, hs = _merge(oa, ob, gates, ys, wa16, wb16, wo16, g_post, gf_pre, l, tm=ms)
        ys = _ffn(ys, hs, wu16, wd16, gf_post, l, tm=ms, tf=1024)

    akp, avp, bkp, bvp = _prompt_caches(prompt_kva, prompt_kvb, batch=batch, seq=seq,
                                        na=min(a_len, seq), nb=min(b_len, seq))
    return (yp.reshape(batch, seq, d), ys.reshape(dec_batch, t_new, d),
            _cache_untranspose(akp, A_HEADS), _cache_untranspose(avp, A_HEADS),
            _cache_untranspose(bkp, B_KV_HEADS), _cache_untranspose(bvp, B_KV_HEADS),
            _cache_untranspose(aks, A_HEADS), _cache_untranspose(avs, A_HEADS),
            _cache_untranspose(bks, B_KV_HEADS), _cache_untranspose(bvs, B_KV_HEADS))
```

```python
import functools
import math

import jax
import jax.numpy as jnp
from jax import lax
from jax.experimental import pallas as pl
from jax.experimental.pallas import tpu as pltpu

D_MODEL = 2048
CHUNK = 64
HEAD_DIM = 64
A_HEADS = 16
A_WIDTH = A_HEADS * HEAD_DIM
A_LEFT_CHUNKS = 8
A_REL_CLIP = 256
B_HEADS = 16
B_KV_HEADS = 2
B_GROUP = B_HEADS // B_KV_HEADS
B_WIDTH = B_HEADS * HEAD_DIM
B_KV_WIDTH = B_KV_HEADS * HEAD_DIM
B_LEFT_CHUNKS = 2
T5_BUCKETS = 32
T5_MAX_DIST = 128
EPS = 1e-6
NEG_INF = -1e30
LOG2E = math.log2(math.e)
Q_SCALE = HEAD_DIM ** -0.5 * LOG2E

LANES = 128
MXU_COLS = 256

COL_QA = 0
COL_KA = A_WIDTH
COL_VA = 2 * A_WIDTH
COL_QB = 3 * A_WIDTH
COL_KB = 3 * A_WIDTH + B_WIDTH
COL_GA = COL_KB + 2 * B_KV_WIDTH
PROJ_TN = 4 * MXU_COLS
QKV_KA, QKV_VA, QKV_QA, QKV_QB = 0, 1, 2, 3

A_TQ = 256
B_TQ = 128
A_KBLOCKS = A_LEFT_CHUNKS * CHUNK // A_TQ + 1
B_KBLOCKS = B_LEFT_CHUNKS * CHUNK // B_TQ + 1

MERGE_SLAB = 256

VMEM_LIMIT = 56 * 1024 * 1024


def _params(sem, vmem=VMEM_LIMIT):
    return pltpu.CompilerParams(dimension_semantics=sem, vmem_limit_bytes=vmem)


def _rms_scale(x, g):
    return x * lax.rsqrt(jnp.mean(x * x, axis=-1, keepdims=True) + EPS) * g


def _layer_vec_spec(l, d):
    return pl.BlockSpec((None, 1, d), lambda *_: (l, 0, 0))


PROJ_KV_STEPS = 2
PROJ_Q_STEPS = 4
PROJ_GATE_STEP0 = 4


def _proj_first_blocks():
    n_w = PROJ_TN // MXU_COLS
    blocks = [c // MXU_COLS for c in (COL_KA, COL_VA, COL_QA, COL_QB)]
    return blocks + [COL_GA // MXU_COLS + g * n_w for g in range(2 * D_MODEL // PROJ_TN)]


def _in_proj_kernel(x_ref, g_ref, *refs, from_f32):
    n_w = PROJ_TN // MXU_COLS
    w_refs, wkvb_ref = refs[:n_w], refs[n_w]
    qkv_ref, kvb16_ref, gates_ref, kva32_ref, kvb32_ref = refs[n_w + 1:n_w + 6]
    pack_refs, h_ref = refs[n_w + 6:-1], refs[-1]
    j = pl.program_id(1)
    tm = x_ref.shape[0]

    def weight(k):
        w_ref = wkvb_ref if k == n_w else w_refs[k]
        if not from_f32:
            return w_ref[...]
        w = w_ref[...].astype(jnp.bfloat16)
        pack_refs[k][...] = w
        return w

    def tile(out16_ref, out32_ref, scale=None, rows=slice(None)):
        h = h_ref[rows]
        for k in range(n_w):
            cols = slice(k * MXU_COLS, (k + 1) * MXU_COLS)
            acc = jnp.dot(h, weight(k), preferred_element_type=jnp.float32)
            if out32_ref is not None:
                out32_ref[rows, cols] = acc
            if scale is not None:
                acc = acc * scale
            out16_ref[rows, cols] = acc.astype(out16_ref.dtype)

    @pl.when(j == 0)
    def _():
        slab = tm // 2 if tm >= 2 * MXU_COLS else tm
        for r0 in range(0, tm, slab):
            rows = slice(r0, r0 + slab)
            h_ref[rows] = _rms_scale(x_ref[rows], g_ref[...]).astype(h_ref.dtype)
            tile(qkv_ref, kva32_ref, rows=rows)

    @pl.when((j > 0) & (j < PROJ_KV_STEPS))
    def _():
        tile(qkv_ref, kva32_ref)

    @pl.when((j >= PROJ_KV_STEPS) & (j < PROJ_Q_STEPS))
    def _():
        tile(qkv_ref, None, Q_SCALE)

    @pl.when(j == PROJ_Q_STEPS - 1)
    def _():
        acc = jnp.dot(h_ref[...], weight(n_w), preferred_element_type=jnp.float32)
        kvb16_ref[...] = acc.astype(kvb16_ref.dtype)
        kvb32_ref[...] = acc

    @pl.when(j >= PROJ_GATE_STEP0)
    def _():
        tile(gates_ref, None)


def _in_proj(x, g, w, l, *, tm, from_f32=False):
    m, d = x.shape
    n_w = PROJ_TN // MXU_COLS
    n_gate = 2 * D_MODEL // PROJ_TN
    steps = PROJ_GATE_STEP0 + n_gate
    first = _proj_first_blocks()
    kvb_n = 2 * B_KV_WIDTH

    def f32_map(k):
        def index_map(i, j):
            blk = first[-1] + k
            for step in range(steps - 2, -1, -1):
                blk = jnp.where(j == step, first[step] + k, blk)
            return (l, 0, blk)
        return index_map

    packed_spec = pl.BlockSpec((d, MXU_COLS), lambda i, j: (0, j))
    packed_kvb_spec = pl.BlockSpec((d, kvb_n), lambda i, j: (0, 0))
    if from_f32:
        assert m == tm, "the packed weights are written once per row tile"
        w_specs = [pl.BlockSpec((None, d, MXU_COLS), f32_map(k)) for k in range(n_w)]
        w_specs.append(pl.BlockSpec((None, d, kvb_n), lambda i, j: (l, 0, COL_KB // kvb_n)))
        w_args = [w] * (n_w + 1)
        pack_specs = [packed_spec] * n_w + [packed_kvb_spec]
        pack_shapes = [jax.ShapeDtypeStruct((d, steps * MXU_COLS), jnp.bfloat16)] * n_w
        pack_shapes.append(jax.ShapeDtypeStruct((d, kvb_n), jnp.bfloat16))
    else:
        w_specs = [packed_spec] * n_w + [packed_kvb_spec]
        w_args = list(w)
        pack_specs, pack_shapes = [], []
    outs = pl.pallas_call(
        functools.partial(_in_proj_kernel, from_f32=from_f32),
        grid=(m // tm, steps),
        in_specs=[
            pl.BlockSpec((tm, d), lambda i, j: (i, 0)),
            _layer_vec_spec(l, d),
        ] + w_specs,
        out_specs=[
            pl.BlockSpec((tm, PROJ_TN), lambda i, j: (i, jnp.minimum(j, PROJ_Q_STEPS - 1))),
            pl.BlockSpec((tm, kvb_n), lambda i, j: (i, 0)),
            pl.BlockSpec((tm, PROJ_TN),
                         lambda i, j: (i, jnp.clip(j - PROJ_GATE_STEP0, 0, n_gate - 1))),
            pl.BlockSpec((tm, PROJ_TN), lambda i, j: (i, jnp.minimum(j, PROJ_KV_STEPS - 1))),
            pl.BlockSpec((tm, kvb_n), lambda i, j: (i, 0)),
        ] + pack_specs,
        out_shape=[
            jax.ShapeDtypeStruct((m, PROJ_Q_STEPS * PROJ_TN), jnp.bfloat16),
            jax.ShapeDtypeStruct((m, kvb_n), jnp.bfloat16),
            jax.ShapeDtypeStruct((m, 2 * D_MODEL), jnp.bfloat16),
            jax.ShapeDtypeStruct((m, PROJ_KV_STEPS * PROJ_TN), jnp.float32),
            jax.ShapeDtypeStruct((m, kvb_n), jnp.float32),
        ] + pack_shapes,
        scratch_shapes=[pltpu.VMEM((tm, d), jnp.bfloat16)],
        compiler_params=_params(("parallel", "arbitrary")),
        name="in_proj",
    )(x, g, *w_args)
    return (outs[:5], tuple(outs[5:])) if from_f32 else outs


def _low_half():
    return lax.broadcasted_iota(jnp.int32, (1, LANES), 1) < HEAD_DIM


def _start_mask(tile_in_seq, n_kblocks, tq):
    lane = lax.broadcasted_iota(jnp.int32, (1, n_kblocks * tq), 1)
    first_valid = jnp.maximum(n_kblocks - 1 - tile_in_seq, 0) * tq
    return jnp.where(lane >= first_valid, 0.0, NEG_INF)


def _pack_w_in_slab(w32_ref, pack_refs):
    slab = w32_ref[...].astype(jnp.bfloat16)
    block = lambda b: slab[:, b * MXU_COLS:(b + 1) * MXU_COLS]
    for k, pack_ref in enumerate(pack_refs[:-1]):
        pack_ref[...] = jnp.concatenate([block(b + k) for b in _proj_first_blocks()], axis=1)
    pack_refs[-1][...] = block(COL_KB // MXU_COLS)


def _attn_a_kernel(q_ref, *refs, tiles_per_seq, n_cast, packing):
    k_refs = refs[:A_KBLOCKS]
    v_refs = refs[A_KBLOCKS:2 * A_KBLOCKS]
    bias_ref = refs[2 * A_KBLOCKS]
    rest = refs[2 * A_KBLOCKS + 1:]
    w32_refs, rest = rest[:n_cast], rest[n_cast:]
    if packing:
        w_in_ref, rest = rest[0], rest[1:]
    o_ref, w16_refs, pack_refs = rest[0], rest[1:1 + n_cast], rest[1 + n_cast:]
    tile_in_seq = pl.program_id(0) % tiles_per_seq
    low = _low_half()

    def heads(n_valid):
        first = A_KBLOCKS - n_valid
        for w32_ref, w16_ref in zip(w32_refs, w16_refs):
            w16_ref[...] = w32_ref[...].astype(w16_ref.dtype)
        if packing:
            _pack_w_in_slab(w_in_ref, pack_refs)
        def scores(p):
            sl = slice(p * LANES, (p + 1) * LANES)
            q2 = q_ref[:, sl]
            k2 = jnp.concatenate([r[:, sl] for r in k_refs[first:]], axis=0)
            qs = jnp.concatenate([jnp.where(low, q2, 0), jnp.where(low, 0, q2)], axis=0)
            return lax.dot_general(qs, k2, (((1,), (1,)), ((), ())),
                                   preferred_element_type=jnp.float32)

        n_pairs = A_HEADS // 2
        ahead = 2
        pending = [scores(p) for p in range(ahead)]
        for p in range(n_pairs):
            sl = slice(p * LANES, (p + 1) * LANES)
            v2 = jnp.concatenate([r[:, sl] for r in v_refs[first:]], axis=0)
            if p + ahead < n_pairs:
                pending.append(scores(p + ahead))
            s_all = pending.pop(0)
            probs = []
            for hh in range(2):
                s = s_all[hh * A_TQ:(hh + 1) * A_TQ] + bias_ref[2 * p + hh, :, first * A_TQ:]
                s = s.astype(v2.dtype)
                probs.append(jnp.exp2(s - jnp.max(s, axis=-1, keepdims=True)))
            v_ones = jnp.concatenate([v2, jnp.ones_like(v2)], axis=1)
            r = jnp.dot(jnp.concatenate(probs, axis=0), v_ones,
                        preferred_element_type=jnp.float32)
            out = r[:, :LANES] / r[:, LANES:]
            o_ref[:, sl] = jnp.where(low, out[:A_TQ], out[A_TQ:]).astype(o_ref.dtype)

    for n_valid in range(1, A_KBLOCKS + 1):
        in_case = (tile_in_seq == n_valid - 1 if n_valid < A_KBLOCKS
                   else tile_in_seq >= A_KBLOCKS - 1)
        pl.when(in_case)(functools.partial(heads, n_valid))


def _band_kv_map(j, col, n_kblocks, tiles_per_seq):
    def index_map(t, *_):
        i = t % tiles_per_seq
        return (t - i + jnp.maximum(i - (n_kblocks - 1) + j, 0), col)
    return index_map


def _attn_a(qkv, bias, weights, l, w_in_next=None, *, m, seq):
    tiles_per_seq = seq // A_TQ
    steps = m // A_TQ
    kv_spec = lambda j, col: pl.BlockSpec(
        (A_TQ, A_WIDTH), _band_kv_map(j, col, A_KBLOCKS, tiles_per_seq))
    slab32 = lambda w, layer: pl.BlockSpec((None, w.shape[1] // steps, w.shape[2]),
                                           lambda t: (layer, t, 0))
    slab16 = lambda rows, width: pl.BlockSpec((rows // steps, width), lambda t: (t, 0))
    in_specs = ([pl.BlockSpec((A_TQ, A_WIDTH), lambda t: (t, QKV_QA))]
                + [kv_spec(j, QKV_KA) for j in range(A_KBLOCKS)]
                + [kv_spec(j, QKV_VA) for j in range(A_KBLOCKS)]
                + [pl.BlockSpec((A_HEADS, A_TQ, A_KBLOCKS * A_TQ), lambda t: (l, 0, 0),
                                pipeline_mode=pl.Buffered(1))]
                + [slab32(w, l) for w in weights])
    out_specs = ([pl.BlockSpec((A_TQ, A_WIDTH), lambda t: (t, 0))]
                 + [slab16(w.shape[1], w.shape[2]) for w in weights])
    out_shape = ([jax.ShapeDtypeStruct((m, A_WIDTH), jnp.bfloat16)]
                 + [jax.ShapeDtypeStruct(w.shape[1:], jnp.bfloat16) for w in weights])
    args = [qkv] * (2 * A_KBLOCKS + 1) + [bias, *weights]
    if w_in_next is not None:
        w_in, layer = w_in_next
        in_specs.append(slab32(w_in, layer))
        args.append(w_in)
        widths = [len(_proj_first_blocks()) * MXU_COLS] * (PROJ_TN // MXU_COLS)
        widths.append(2 * B_KV_WIDTH)
        out_specs += [slab16(w_in.shape[1], width) for width in widths]
        out_shape += [jax.ShapeDtypeStruct((w_in.shape[1], width), jnp.bfloat16)
                      for width in widths]
    return pl.pallas_call(
        functools.partial(_attn_a_kernel, tiles_per_seq=tiles_per_seq,
                          n_cast=len(weights), packing=w_in_next is not None),
        grid=(steps,),
        in_specs=in_specs,
        out_specs=out_specs,
        out_shape=out_shape,
        compiler_params=_params(("parallel",)),
        name="attn_a",
    )(*args)


def _dup_half(x, g):
    swapped = pltpu.roll(x, HEAD_DIM, 1)
    low = _low_half()
    return jnp.where(low, x, swapped) if g == 0 else jnp.where(low, swapped, x)


def _attn_b_kernel(sink_ref, q_ref, *refs, tiles_per_seq):
    k_refs = refs[:B_KBLOCKS]
    v_refs = refs[B_KBLOCKS:2 * B_KBLOCKS]
    bias_ref, o_ref = refs[2 * B_KBLOCKS:]
    start = _start_mask(pl.program_id(0) % tiles_per_seq, B_KBLOCKS, B_TQ)
    low = _low_half()
    k2 = jnp.concatenate([r[...] for r in k_refs], axis=0).astype(jnp.float32)
    v2 = jnp.concatenate([r[...] for r in v_refs], axis=0).astype(jnp.float32)
    pairs = B_GROUP // 2
    for g in range(B_KV_HEADS):
        kd = _dup_half(k2, g).astype(jnp.bfloat16)
        vd = _dup_half(v2, g).astype(jnp.bfloat16)
        stacked = []
        for p in range(pairs):
            c0 = (g * pairs + p) * LANES
            q2 = q_ref[:, c0:c0 + LANES]
            stacked.append(jnp.where(low, q2, 0))
            stacked.append(jnp.where(low, 0, q2))
        qs = jnp.concatenate(stacked, axis=0)
        s_all = lax.dot_general(qs, kd, (((1,), (1,)), ((), ())),
                                preferred_element_type=jnp.float32)
        probs, sink_terms = [], []
        for hl in range(B_GROUP):
            h = g * B_GROUP + hl
            s = s_all[hl * B_TQ:(hl + 1) * B_TQ] + bias_ref[h] + start
            sink = sink_ref[h]
            mx = jnp.maximum(jnp.max(s, axis=-1, keepdims=True), sink)
            probs.append(jnp.exp2(s - mx).astype(jnp.bfloat16))
            sink_terms.append(jnp.exp2(sink - mx))
        v_ones = jnp.concatenate([vd, jnp.ones_like(vd)], axis=1)
        r = jnp.dot(jnp.concatenate(probs, axis=0), v_ones,
                    preferred_element_type=jnp.float32)
        outs = []
        for hl in range(B_GROUP):
            rh = r[hl * B_TQ:(hl + 1) * B_TQ]
            outs.append(rh[:, :LANES] / (rh[:, LANES:] + sink_terms[hl]))
        for p in range(pairs):
            c0 = (g * pairs + p) * LANES
            o_ref[:, c0:c0 + LANES] = jnp.where(
                low, outs[2 * p], outs[2 * p + 1]).astype(o_ref.dtype)


def _attn_b(qkv, kvb16, bias, sinks, *, m, seq):
    tiles_per_seq = seq // B_TQ
    kv_spec = lambda j, col: pl.BlockSpec(
        (B_TQ, LANES), _band_kv_map(j, col, B_KBLOCKS, tiles_per_seq))
    return pl.pallas_call(
        functools.partial(_attn_b_kernel, tiles_per_seq=tiles_per_seq),
        grid_spec=pltpu.PrefetchScalarGridSpec(
            num_scalar_prefetch=1,
            grid=(m // B_TQ,),
            in_specs=[pl.BlockSpec((B_TQ, B_WIDTH), lambda t, s: (t, QKV_QB))]
            + [kv_spec(j, 0) for j in range(B_KBLOCKS)]
            + [kv_spec(j, 1) for j in range(B_KBLOCKS)]
            + [pl.BlockSpec((B_HEADS, B_TQ, B_KBLOCKS * B_TQ), lambda t, s: (0, 0, 0))],
            out_specs=pl.BlockSpec((B_TQ, B_WIDTH), lambda t, s: (t, 0)),
        ),
        out_shape=jax.ShapeDtypeStruct((m, B_WIDTH), jnp.bfloat16),
        compiler_params=_params(("parallel",)),
        name="attn_b",
    )(sinks, qkv, *([kvb16] * (2 * B_KBLOCKS)), bias)


def _pad_top(x, t_new):
    return jnp.concatenate([jnp.zeros((LANES - t_new, x.shape[1]), x.dtype), x], axis=0)


def _roll_in(cache_t, new_rows, out_ref, layer, t_new):
    n = cache_t.shape[1]
    shifted = pltpu.roll(cache_t, n - t_new, 1)
    lane = lax.broadcasted_iota(jnp.int32, (1, LANES), 1)
    tail = jnp.where(lane < LANES - t_new, shifted[:, n - LANES:],
                     _pad_top(new_rows, t_new).T)
    if n > LANES:
        out_ref[layer, :, :n - LANES] = shifted[:, :n - LANES]
    out_ref[layer, :, n - LANES:] = tail


def _attn_sample_kernel(qa_ref, qb_ref, kan_ref, van_ref, kbn_ref, vbn_ref,
                        cak_ref, cav_ref, cbk_ref, cbv_ref,
                        bias_ac_ref, bias_an_ref, bias_bc_ref, bias_bn_ref, sink_ref,
                        *refs, t_new, layer, rolling):
    if rolling:
        prev_refs, (oa_ref, ob_ref), out_refs = refs[:4], refs[4:6], refs[6:]
        for cache_ref, new_ref, prev_ref, out_ref in zip(
                (cak_ref, cav_ref, cbk_ref, cbv_ref), (kan_ref, van_ref, kbn_ref, vbn_ref),
                prev_refs, out_refs):
            for l2 in range(layer + 1):
                rows_new = new_ref[...] if l2 == layer else prev_ref[l2]
                _roll_in(cache_ref[l2], rows_new, out_ref, l2, t_new)
        cak_ref, cav_ref, cbk_ref, cbv_ref = (
            r.at[layer] for r in (cak_ref, cav_ref, cbk_ref, cbv_ref))
    else:
        oa_ref, ob_ref = refs
    heads = A_HEADS
    rows = heads * t_new
    width = heads * HEAD_DIM
    bf16 = jnp.bfloat16
    row_head = lax.broadcasted_iota(jnp.int32, (rows, width), 0) // t_new
    lane_head = lax.broadcasted_iota(jnp.int32, (rows, width), 1) // HEAD_DIM
    own = row_head == lane_head

    def stack_q(q):
        return jnp.where(own, jnp.concatenate([q] * heads, axis=0), 0)

    def nt_dot(a, b):
        return lax.dot_general(a, b, (((1,), (1,)), ((), ())),
                               preferred_element_type=jnp.float32)

    def finish(s_c, s_n, vt_c, v_n, sink):
        mx = jnp.maximum(jnp.max(s_c, axis=-1, keepdims=True),
                         jnp.max(s_n, axis=-1, keepdims=True))
        if sink is not None:
            mx = jnp.maximum(mx, sink)
        e_c = jnp.exp2(s_c - mx)
        e_n = jnp.exp2(s_n - mx)
        denom = jnp.sum(e_c, axis=-1, keepdims=True) + jnp.sum(e_n, axis=-1, keepdims=True)
        if sink is not None:
            denom = denom + jnp.exp2(sink - mx)
        o = nt_dot(e_c.astype(bf16), vt_c)
        o = o + jnp.dot(e_n.astype(bf16), v_n, preferred_element_type=jnp.float32)
        o = jnp.where(own, o / denom, 0.0)
        return jnp.sum(o.reshape(heads, t_new, width), axis=0)

    k_n = _pad_top(kan_ref[...], t_new).astype(bf16)
    v_n = _pad_top(van_ref[...], t_new).astype(bf16)
    qs = stack_q(qa_ref[...])
    s_c = jnp.dot(qs, cak_ref[...].astype(bf16),
                  preferred_element_type=jnp.float32) + bias_ac_ref[...]
    s_n = nt_dot(qs, k_n) + bias_an_ref[...]
    oa_ref[...] = finish(s_c, s_n, cav_ref[...].astype(bf16), v_n, None).astype(oa_ref.dtype)

    def expansion(shape, src_axis):
        src = lax.broadcasted_iota(jnp.int32, shape, src_axis)
        dst = lax.broadcasted_iota(jnp.int32, shape, 1 - src_axis)
        return ((src % HEAD_DIM == dst % HEAD_DIM)
                & (src // HEAD_DIM == dst // (B_GROUP * HEAD_DIM))).astype(bf16)

    def widen(x):
        return jnp.dot(x.astype(bf16), expansion((B_KV_WIDTH, width), 0),
                       preferred_element_type=jnp.float32).astype(bf16)

    def widen_t(xt):
        return jnp.dot(expansion((width, B_KV_WIDTH), 1), xt.astype(bf16),
                       preferred_element_type=jnp.float32).astype(bf16)

    k_n = widen(_pad_top(kbn_ref[...], t_new))
    v_n = widen(_pad_top(vbn_ref[...], t_new))
    qs = stack_q(qb_ref[...])
    s_c = jnp.dot(qs, widen_t(cbk_ref[...]),
                  preferred_element_type=jnp.float32) + bias_bc_ref[...]
    s_n = nt_dot(qs, k_n) + bias_bn_ref[...]
    ob_ref[...] = finish(s_c, s_n, widen_t(cbv_ref[...]), v_n, sink_ref[...]).astype(ob_ref.dtype)


def _attn_sample(qkv, kva32, kvb32, caches, l, bias_ac, bias_an, bias_bc, bias_bn,
                 sink_col, prev_new=None, *, t_new):
    m = qkv.shape[0]
    cak, cav, cbk, cbv = caches
    a_len, b_len = cak.shape[-1], cbk.shape[-1]
    rows = A_HEADS * t_new
    rolling = prev_new is not None
    const = lambda shape: pl.BlockSpec(shape, lambda b: (0,) * len(shape))
    if rolling:
        assert l == cak.shape[0] - 1 and prev_new[0].shape[0] == l
        cache_spec = lambda c: pl.BlockSpec((c.shape[0], None) + c.shape[2:],
                                            lambda b: (0, b, 0, 0))
    else:
        cache_spec = lambda c: pl.BlockSpec((None, None) + c.shape[2:],
                                            lambda b: (l, b, 0, 0))
    in_specs = [
        pl.BlockSpec((t_new, A_WIDTH), lambda b: (b, QKV_QA)),
        pl.BlockSpec((t_new, B_WIDTH), lambda b: (b, QKV_QB)),
        pl.BlockSpec((t_new, A_WIDTH), lambda b: (b, 0)),
        pl.BlockSpec((t_new, A_WIDTH), lambda b: (b, 1)),
        pl.BlockSpec((t_new, B_KV_WIDTH), lambda b: (b, 0)),
        pl.BlockSpec((t_new, B_KV_WIDTH), lambda b: (b, 1)),
        cache_spec(cak), cache_spec(cav), cache_spec(cbk), cache_spec(cbv),
        pl.BlockSpec((rows, a_len), lambda b: (l, 0)),
        pl.BlockSpec((rows, LANES), lambda b: (l, 0)),
        const((rows, b_len)), const((rows, LANES)),
        const((rows, 1)),
    ]
    args = [qkv, qkv, kva32, kva32, kvb32, kvb32, cak, cav, cbk, cbv,
            bias_ac, bias_an, bias_bc, bias_bn, sink_col]
    out_specs = [
        pl.BlockSpec((t_new, A_WIDTH), lambda b: (b, 0)),
        pl.BlockSpec((t_new, B_WIDTH), lambda b: (b, 0)),
    ]
    out_shape = [
        jax.ShapeDtypeStruct((m, A_WIDTH), jnp.bfloat16),
        jax.ShapeDtypeStruct((m, B_WIDTH), jnp.bfloat16),
    ]
    if rolling:
        new_a, new_b = prev_new
        prev_spec = lambda width, col: pl.BlockSpec((l, t_new, width), lambda b: (0, b, col))
        in_specs += [prev_spec(A_WIDTH, 0), prev_spec(A_WIDTH, 1),
                     prev_spec(B_KV_WIDTH, 0), prev_spec(B_KV_WIDTH, 1)]
        args += [new_a, new_a, new_b, new_b]
        out_specs += [cache_spec(c) for c in caches]
        out_shape += [jax.ShapeDtypeStruct(c.shape, jnp.float32) for c in caches]
    return pl.pallas_call(
        functools.partial(_attn_sample_kernel, t_new=t_new, layer=l, rolling=rolling),
        grid=(m // t_new,),
        in_specs=in_specs,
        out_specs=out_specs,
        out_shape=out_shape,
        compiler_params=_params(("parallel",)),
        name="attn_sample",
    )(*args)


def _prompt_caches_kernel(*refs, depth):
    in_refs, (ak_ref, av_ref, bk_ref, bv_ref) = refs[:4 * depth], refs[4 * depth:]
    for k in range(depth):
        @pl.when(pl.program_id(0) == k)
        def _(k=k):
            ka_ref, va_ref, kb_ref, vb_ref = in_refs[4 * k:4 * k + 4]
            ak_ref[...] = ka_ref[...].T
            av_ref[...] = va_ref[...].T
            bk_ref[...] = kb_ref[...].T
            bv_ref[...] = vb_ref[...].T


def _prompt_caches(kva32s, kvb32s, *, batch, seq, na, nb):
    depth = len(kva32s)

    def rows_map(k, tile, col):
        def index_map(l, b):
            bb = jnp.where(l < k, 0, jnp.where(l > k, batch - 1, b))
            return ((bb + 1) * tile - 1, col)
        return index_map

    in_specs, args = [], []
    for k in range(depth):
        in_specs += [
            pl.BlockSpec((na, A_WIDTH), rows_map(k, seq // na, 0)),
            pl.BlockSpec((na, A_WIDTH), rows_map(k, seq // na, 1)),
            pl.BlockSpec((nb, B_KV_WIDTH), rows_map(k, seq // nb, 0)),
            pl.BlockSpec((nb, B_KV_WIDTH), rows_map(k, seq // nb, 1)),
        ]
        args += [kva32s[k], kva32s[k], kvb32s[k], kvb32s[k]]
    out_shapes = [(depth, batch, A_WIDTH, na)] * 2 + [(depth, batch, B_KV_WIDTH, nb)] * 2
    return pl.pallas_call(
        functools.partial(_prompt_caches_kernel, depth=depth),
        grid=(depth, batch),
        in_specs=in_specs,
        out_specs=[pl.BlockSpec((None, None) + s[2:], lambda l, b: (l, b, 0, 0))
                   for s in out_shapes],
        out_shape=[jax.ShapeDtypeStruct(s, jnp.float32) for s in out_shapes],
        compiler_params=_params(("arbitrary", "arbitrary")),
        name="prompt_caches",
    )(*args)


def _merge_kernel(oa_ref, ob_ref, ga_ref, gb_ref, x_ref, wa_ref, wb_ref, wo_ref,
                  g_ref, gnext_ref, y_ref, h_ref):
    tm = x_ref.shape[0]
    slab = min(tm, MERGE_SLAB)
    for r0 in range(0, tm, slab):
        rows = slice(r0, r0 + slab)
        ta = jnp.dot(oa_ref[rows], wa_ref[...], preferred_element_type=jnp.float32)
        tb = jnp.dot(ob_ref[rows], wb_ref[...], preferred_element_type=jnp.float32)
        mixed = (jax.nn.sigmoid(ga_ref[rows].astype(jnp.float32)) * ta
                 + jax.nn.sigmoid(gb_ref[rows].astype(jnp.float32)) * tb)
        z = jnp.dot(mixed.astype(jnp.bfloat16), wo_ref[...],
                    preferred_element_type=jnp.float32)
        y = x_ref[rows] + _rms_scale(z, g_ref[...])
        y_ref[rows] = y
        h_ref[rows] = _rms_scale(y, gnext_ref[...]).astype(h_ref.dtype)


def _merge(oa, ob, gates, x, wa, wb, wo, g, g_next, l, *, tm):
    m, d = x.shape
    resident = lambda w: pl.BlockSpec(w.shape, lambda i: (0, 0),
                                      pipeline_mode=pl.Buffered(1))
    return pl.pallas_call(
        _merge_kernel,
        grid=(m // tm,),
        in_specs=[
            pl.BlockSpec((tm, A_WIDTH), lambda i: (i, 0)),
            pl.BlockSpec((tm, B_WIDTH), lambda i: (i, 0)),
            pl.BlockSpec((tm, d), lambda i: (i, 0)),
            pl.BlockSpec((tm, d), lambda i: (i, 1)),
            pl.BlockSpec((tm, d), lambda i: (i, 0)),
            resident(wa), resident(wb), resident(wo),
            _layer_vec_spec(l, d),
            _layer_vec_spec(l, d),
        ],
        out_specs=[pl.BlockSpec((tm, d), lambda i: (i, 0))] * 2,
        out_shape=[jax.ShapeDtypeStruct((m, d), jnp.float32),
                   jax.ShapeDtypeStruct((m, d), jnp.bfloat16)],
        compiler_params=_params(("parallel",)),
        name="merge",
    )(oa, ob, gates, gates, x, wa, wb, wo, g, g_next)


def _ffn_kernel(x_ref, h_ref, wu_ref, wd_ref, gpost_ref, y_ref):
    f = pl.program_id(1)
    last = pl.num_programs(1) - 1

    @pl.when(f == 0)
    def _():
        y_ref[...] = jnp.zeros_like(y_ref)

    def partial(rows):
        u = jnp.dot(h_ref[rows], wu_ref[...], preferred_element_type=jnp.float32)
        u = jnp.square(jnp.maximum(u, 0.0)).astype(jnp.bfloat16)
        return y_ref[rows] + jnp.dot(u, wd_ref[...], preferred_element_type=jnp.float32)

    @pl.when(f < last)
    def _():
        y_ref[...] = partial(slice(None))

    @pl.when(f == last)
    def _():
        tm = x_ref.shape[0]
        slab = tm // 2 if tm >= 2 * MXU_COLS else tm
        for r0 in range(0, tm, slab):
            rows = slice(r0, r0 + slab)
            y_ref[rows] = x_ref[rows] + _rms_scale(partial(rows), gpost_ref[...])


def _ffn(x, h, wu, wd, gpost, l, *, tm, tf):
    m, d = x.shape
    dff = wu.shape[-1]
    return pl.pallas_call(
        _ffn_kernel,
        grid=(m // tm, dff // tf),
        in_specs=[
            pl.BlockSpec((tm, d), lambda i, f: (i, 0)),
            pl.BlockSpec((tm, d), lambda i, f: (i, 0)),
            pl.BlockSpec((d, tf), lambda i, f: (0, f)),
            pl.BlockSpec((tf, d), lambda i, f: (f, 0)),
            _layer_vec_spec(l, d),
        ],
        out_specs=pl.BlockSpec((tm, d), lambda i, f: (i, 0)),
        out_shape=jax.ShapeDtypeStruct((m, d), jnp.float32),
        compiler_params=_params(("parallel", "arbitrary")),
        name="ffn",
    )(x, h, wu, wd, gpost)


def _t5_bucket(rel):
    half = T5_BUCKETS // 2
    exact = half // 2
    ret = jnp.where(rel > 0, half, 0)
    n = jnp.abs(rel)
    large = exact + (jnp.log(jnp.maximum(n, 1).astype(jnp.float32) / exact)
                     / math.log(T5_MAX_DIST / exact) * (half - exact)).astype(jnp.int32)
    large = jnp.minimum(large, half - 1)
    return ret + jnp.where(n < exact, n, large)


def _a_bias_of_rel(table):
    scaled = table.T * LOG2E
    return lambda rel: scaled[:, jnp.clip(rel, -A_REL_CLIP, A_REL_CLIP) + A_REL_CLIP]


def _b_bias_of_rel(table):
    scaled = table.T * LOG2E
    return lambda rel: scaled[:, _t5_bucket(-rel)]


def _hankel(u, q, n):
    heads, k = u.shape
    period = q + n
    u = jnp.pad(u, ((0, 0), (0, period - k)))
    flat = jnp.tile(u, (1, q + 1))[:, :q * (period + 1)]
    return flat.reshape(heads, q, period + 1)[:, :, :n]


def _rel_bias(bias_of_rel, q_len, n_keys, k0):
    k = jnp.arange(q_len + n_keys - 1)
    u = bias_of_rel(k - (n_keys - 1) - k0)
    return _hankel(u, q_len, n_keys)[:, :, ::-1]


def _band_tile(bias_of_rel, tq, n_prev):
    window = tq + n_prev * CHUNK
    band = _rel_bias(bias_of_rel, CHUNK, (n_prev + 1) * CHUNK, -n_prev * CHUNK)
    blocks = []
    for c in range(tq // CHUNK):
        left = c * CHUNK
        right = window - left - band.shape[-1]
        blocks.append(jnp.pad(band, ((0, 0), (0, 0), (left, right)),
                              constant_values=NEG_INF))
    return jnp.concatenate(blocks, axis=1)


def _sample_bias(bias_of_rel, t_new, n_cache):
    full = _rel_bias(bias_of_rel, t_new, n_cache + t_new, -n_cache)
    full = full.reshape(-1, n_cache + t_new)
    new = jnp.pad(full[:, n_cache:], ((0, 0), (LANES - t_new, 0)),
                  constant_values=NEG_INF)
    return full[:, :n_cache], new


def _cache_t(c):
    d, s, r, h, e = c.shape
    return jnp.transpose(c, (0, 1, 3, 4, 2)).reshape(d, s, h * e, r)


def _cache_untranspose(c, heads):
    d, s, w, r = c.shape
    return jnp.transpose(c.reshape(d, s, heads, w // heads, r), (0, 1, 4, 2, 3))


def kernel(x_prompt, x_sample, cache_a_k, cache_a_v, cache_b_k, cache_b_v, w_in,
           w_a_out, w_b_out, w_out, a_rel_table, t5_table, b_sinks, g_mix_pre,
           g_mix_post, g_ffn_pre, g_ffn_post, w_up, w_down):
    depth = w_in.shape[0]
    assert depth >= 2, "the last layer's sample kernel rolls the earlier layers' caches"
    batch, seq, d = x_prompt.shape
    dec_batch, t_new, _ = x_sample.shape
    a_len = cache_a_k.shape[2]
    b_len = cache_b_k.shape[2]
    mp = batch * seq
    ms = dec_batch * t_new

    yp = x_prompt.reshape(mp, d)
    ys = x_sample.reshape(ms, d)

    w_pack = None
    later_weights = (w_a_out, w_b_out, w_out, w_up, w_down)
    g_pre = g_mix_pre[:, None, :]
    g_post = g_mix_post[:, None, :]
    gf_pre = g_ffn_pre[:, None, :]
    gf_post = g_ffn_post[:, None, :]
    caches = tuple(_cache_t(c) for c in (cache_a_k, cache_a_v, cache_b_k, cache_b_v))

    b_of_rel = _b_bias_of_rel(t5_table)
    bias_b_tile = _band_tile(b_of_rel, B_TQ, B_LEFT_CHUNKS)
    bias_bc, bias_bn = _sample_bias(b_of_rel, t_new, b_len)
    a_of_rel = _a_bias_of_rel(jnp.concatenate([a_rel_table[l] for l in range(depth)], axis=1))
    bias_a_tile = _band_tile(a_of_rel, A_TQ, A_LEFT_CHUNKS)
    bias_ac, bias_an = _sample_bias(a_of_rel, t_new, a_len)

    prompt_kva, prompt_kvb, sample_kva, sample_kvb = [], [], [], []
    for l in range(depth):
        sinks = b_sinks[l] * LOG2E

        if w_pack is None:
            sample_proj, w_pack = _in_proj(ys, g_pre, w_in, l, tm=ms, from_f32=True)
        else:
            sample_proj = _in_proj(ys, g_pre, w_pack, l, tm=ms)

        qkv, kvb16, gates, kva32, kvb32 = _in_proj(yp, g_pre, w_pack, l, tm=1024)
        oa, wa16, wb16, wo16, wu16, wd16, *next_pack = _attn_a(
            qkv, bias_a_tile, later_weights, l, (w_in, l + 1) if l + 1 < depth else None,
            m=mp, seq=seq)
        w_pack = tuple(next_pack)
        ob = _attn_b(qkv, kvb16, bias_b_tile, sinks, m=mp, seq=seq)
        yp, hp = _merge(oa, ob, gates, yp, wa16, wb16, wo16, g_post, gf_pre, l, tm=512)
        yp = _ffn(yp, hp, wu16, wd16, gf_post, l, tm=512, tf=1024)
        prompt_kva.append(kva32)
        prompt_kvb.append(kvb32)

        qkv, _, gates, kva32, kvb32 = sample_proj
        sink_col = jnp.repeat(sinks, t_new)[:, None]
        if l + 1 < depth:
            oa, ob = _attn_sample(qkv, kva32, kvb32, caches, l, bias_ac, bias_an, bias_bc,
                                  bias_bn, sink_col, t_new=t_new)
            sample_kva.append(kva32)
            sample_kvb.append(kvb32)
        else:
            prev_new = (jnp.stack(sample_kva), jnp.stack(sample_kvb))
            oa, ob, aks, avs, bks, bvs = _attn_sample(
                qkv, kva32, kvb32, caches, l, bias_ac, bias_an, bias_bc, bias_bn,
                sink_col, prev_new, t_new=t_new)
        ys, hs = _merge(oa, ob, gates, ys, wa16, wb16, wo16, g_post, gf_pre, l, tm=ms)
        ys = _ffn(ys, hs, wu16, wd16, gf_post, l, tm=ms, tf=1024)

    akp, avp, bkp, bvp = _prompt_caches(prompt_kva, prompt_kvb, batch=batch, seq=seq,
                                        na=min(a_len, seq), nb=min(b_len, seq))
    return (yp.reshape(batch, seq, d), ys.reshape(dec_batch, t_new, d),
            _cache_untranspose(akp, A_HEADS), _cache_untranspose(avp, A_HEADS),
            _cache_untranspose(bkp, B_KV_HEADS), _cache_untranspose(bvp, B_KV_HEADS),
            _cache_untranspose(aks, A_HEADS), _cache_untranspose(avs, A_HEADS),
            _cache_untranspose(bks, B_KV_HEADS), _cache_untranspose(bvs, B_KV_HEADS))
```
